```python
import jax, jax.numpy as jnp
from jax import lax
import numpy as np

D_MODEL = 1024
BATCH = 8
SEQ = 2048
DEPTH = 2
DEC_BATCH = 128
DEC_SEQ = 1
PAST_LEN = 16384
PAGE_SIZE = 128

N_MIXERS = 2
N_HG_LAYERS = (DEPTH + 1) // 2
N_CONV_LAYERS = DEPTH // 2
HG_HEADS = 8
HG_DK = D_MODEL // HG_HEADS
HG_DV = D_MODEL // HG_HEADS
HG_CHUNK = 64
CONV_WIDTH = 31
PEER_HEADS = 8
PEER_NKEYS = 128
PEER_N = PEER_NKEYS * PEER_NKEYS
PEER_QDIM = 256
PEER_HALF = PEER_QDIM // 2
PEER_TOPK = 16
PEER_BLOCK = 128
EPS = 1e-6
F32 = jnp.float32

kernel_name = 'hgrn2_conformer_peer_decode_step'


def _rmsnorm(x, g):
    xf = x.astype(F32)
    y = xf * lax.rsqrt(jnp.mean(xf * xf, axis=-1, keepdims=True) + EPS)
    return (y * g.astype(F32)).astype(x.dtype)


def _to_chunks(a, n, c):
    b, _, h, d = a.shape
    return a.reshape(b, n, c, h, d).transpose(1, 0, 2, 3, 4)


def _hgrn_chunkwise(q, logf, k, v, s0):
    b, t, h, dk = q.shape
    dv = v.shape[-1]
    c = min(HG_CHUNK, t)
    n = -(-t // c)
    pad = n * c - t
    if pad:
        padw = ((0, 0), (0, pad), (0, 0), (0, 0))
        q, logf, k, v = [jnp.pad(a, padw) for a in (q, logf, k, v)]
    causal = jnp.tril(jnp.ones((c, c), dtype=bool))[None, :, :, None, None]

    def step(S, inp):
        qc, lc, kc, vc = inp
        bcum = jnp.cumsum(lc, axis=1)
        o_inter = jnp.einsum('bthk,bhkv->bthv', qc * jnp.exp(bcum), S)
        diff = bcum[:, :, None] - bcum[:, None, :]
        decay = jnp.where(causal, jnp.exp(jnp.where(causal, diff, 0.0)), 0.0)
        scores = jnp.einsum('bthk,btshk,bshk->bhts', qc, decay, kc)
        o_intra = jnp.einsum('bhts,bshv->bthv', scores, vc)
        b_last = bcum[:, -1]
        k_dec = kc * jnp.exp(b_last[:, None] - bcum)
        S_new = jnp.exp(b_last)[..., None] * S + jnp.einsum('bshk,bshv->bhkv', k_dec, vc)
        return S_new, o_inter + o_intra

    s_fin, o = lax.scan(step, s0, (_to_chunks(q, n, c), _to_chunks(logf, n, c),
                                   _to_chunks(k, n, c), _to_chunks(v, n, c)))
    o = o.transpose(1, 0, 2, 3, 4).reshape(b, n * c, h, dv)[:, :t]
    return o, s_fin


def _hgrn_mixer(x, s0, w_in, w_out, g_norm, lb):
    B, T, _ = x.shape
    proj = x @ w_in
    qz, fz, iz, gz = jnp.split(proj, 4, axis=-1)
    shk = (B, T, HG_HEADS, HG_DK)
    shv = (B, T, HG_HEADS, HG_DV)
    q = jax.nn.silu(qz.astype(F32)).reshape(shk)
    lbh = lb.reshape(HG_HEADS, HG_DK)
    f = lbh + (1.0 - lbh) * jax.nn.sigmoid(fz.astype(F32).reshape(shk))
    k = 1.0 - f
    v = iz.astype(F32).reshape(shv)
    o, s_new = _hgrn_chunkwise(q, jnp.log(f), k, v, s0.astype(F32))
    o = _rmsnorm(o, g_norm) * jax.nn.silu(gz.astype(F32).reshape(shv))
    y = o.reshape(B, T, D_MODEL).astype(x.dtype) @ w_out
    return y, s_new


def _conformer_conv(x, buf, w_pw1, b_pw1, w_dw, b_dw, ln_g, ln_b, w_pw2, b_pw2):
    h = x @ w_pw1 + b_pw1
    a, gate = jnp.split(h, 2, axis=-1)
    u = a * jax.nn.sigmoid(gate)
    full = jnp.concatenate([buf.astype(u.dtype), u], axis=1)
    y = lax.conv_general_dilated(full, w_dw[:, None, :].astype(u.dtype), (1,), 'VALID',
                                 dimension_numbers=('NWC', 'WIO', 'NWC'),
                                 feature_group_count=D_MODEL) + b_dw
    yf = y.astype(F32)
    mu = jnp.mean(yf, axis=-1, keepdims=True)
    var = jnp.mean(jnp.square(yf - mu), axis=-1, keepdims=True)
    yn = (yf - mu) * lax.rsqrt(var + EPS) * ln_g.astype(F32) + ln_b.astype(F32)
    z = jax.nn.silu(yn).astype(x.dtype)
    return z @ w_pw2 + b_pw2, full[:, -(CONV_WIDTH - 1):]


def _peer(x, w_q, sub_keys, u_tab, v_tab):
    shp = x.shape
    xs = x.reshape(-1, D_MODEL)
    n_tok = xs.shape[0]
    q = (xs @ w_q).astype(F32).reshape(n_tok, PEER_HEADS, 2, PEER_HALF)
    s = jnp.einsum('nhpd,hpkd->nhpk', q, sub_keys.astype(F32))
    s1, i1 = lax.top_k(s[:, :, 0], PEER_TOPK)
    s2, i2 = lax.top_k(s[:, :, 1], PEER_TOPK)
    n_cand = PEER_TOPK * PEER_TOPK
    cand_s = (s1[..., :, None] + s2[..., None, :]).reshape(n_tok, PEER_HEADS, n_cand)
    cand_i = (i1[..., :, None] * PEER_NKEYS + i2[..., None, :]).reshape(n_tok, PEER_HEADS, n_cand)
    top_s, pos = lax.top_k(cand_s, PEER_TOPK)
    idx = jnp.take_along_axis(cand_i, pos, axis=-1)
    gates = jax.nn.softmax(top_s, axis=-1)
    nb = -(-n_tok // PEER_BLOCK)
    pad = nb * PEER_BLOCK - n_tok
    xs_p = jnp.pad(xs, ((0, pad), (0, 0))).reshape(nb, PEER_BLOCK, D_MODEL)
    idx_p = jnp.pad(idx, ((0, pad), (0, 0), (0, 0))).reshape(nb, PEER_BLOCK, PEER_HEADS, PEER_TOPK)
    g_p = jnp.pad(gates, ((0, pad), (0, 0), (0, 0))).reshape(nb, PEER_BLOCK, PEER_HEADS, PEER_TOPK)

    def block(args):
        xb, ib, gb = args
        hid = jax.nn.gelu(jnp.einsum('nhkd,nd->nhk', u_tab[ib], xb).astype(F32), approximate=False)
        return jnp.einsum('nhk,nhkd->nd', (gb * hid).astype(xb.dtype), v_tab[ib])

    out = lax.map(block, (xs_p, idx_p, g_p))
    return out.reshape(nb * PEER_BLOCK, D_MODEL)[:n_tok].reshape(shp).astype(x.dtype)


def setup_inputs(seed: int = 0) -> dict:
    key = jax.random.key(seed)
    ks = jax.random.split(key, 24)
    nrm = jax.random.normal
    D = D_MODEL
    return {
        'x_prompt': nrm(ks[0], (BATCH, SEQ, D), F32),
        'x_sample': nrm(ks[1], (DEC_BATCH, DEC_SEQ, D), F32),
        'state_hgrn': 0.5 * nrm(ks[2], (N_HG_LAYERS, DEC_BATCH, HG_HEADS, HG_DK, HG_DV), F32),
        'state_conv': 0.5 * nrm(ks[3], (N_CONV_LAYERS, DEC_BATCH, CONV_WIDTH - 1, D), F32),
        'norm_mix': 1.0 + 0.01 * nrm(ks[4], (DEPTH, D), F32),
        'norm_ffn': 1.0 + 0.01 * nrm(ks[5], (DEPTH, D), F32),
        'norm_final': 1.0 + 0.01 * nrm(ks[6], (D,), F32),
        'hg_w_in': nrm(ks[7], (N_HG_LAYERS, D, 4 * D), F32) * D ** -0.5,
        'hg_w_out': nrm(ks[8], (N_HG_LAYERS, D, D), F32) * D ** -0.5,
        'hg_gnorm': 1.0 + 0.01 * nrm(ks[9], (N_HG_LAYERS, HG_DV), F32),
        'hg_lb_logits': 0.5 * nrm(ks[10], (DEPTH + 1, HG_HEADS * HG_DK), F32),
        'cv_w_pw1': nrm(ks[11], (N_CONV_LAYERS, D, 2 * D), F32) * D ** -0.5,
        'cv_b_pw1': 0.01 * nrm(ks[12], (N_CONV_LAYERS, 2 * D), F32),
        'cv_w_dw': nrm(ks[13], (N_CONV_LAYERS, CONV_WIDTH, D), F32) * CONV_WIDTH ** -0.5,
        'cv_b_dw': 0.01 * nrm(ks[14], (N_CONV_LAYERS, D), F32),
        'cv_ln_g': 1.0 + 0.01 * nrm(ks[15], (N_CONV_LAYERS, D), F32),
        'cv_ln_b': 0.01 * nrm(ks[16], (N_CONV_LAYERS, D), F32),
        'cv_w_pw2': nrm(ks[17], (N_CONV_LAYERS, D, D), F32) * D ** -0.5,
        'cv_b_pw2': 0.01 * nrm(ks[18], (N_CONV_LAYERS, D), F32),
        'peer_w_q': nrm(ks[19], (DEPTH, D, PEER_HEADS * PEER_QDIM), F32) * D ** -0.5,
        'peer_sub_keys': nrm(ks[20], (DEPTH, PEER_HEADS, 2, PEER_NKEYS, PEER_HALF), F32) * PEER_HALF ** -0.5,
        'peer_u': nrm(ks[21], (DEPTH, PEER_N, D), F32) * D ** -0.5,
        'peer_v': 0.25 * nrm(ks[22], (DEPTH, PEER_N, D), F32),
    }


def reference(x_prompt, x_sample, state_hgrn, state_conv, norm_mix, norm_ffn, norm_final,
              hg_w_in, hg_w_out, hg_gnorm, hg_lb_logits,
              cv_w_pw1, cv_b_pw1, cv_w_dw, cv_b_dw, cv_ln_g, cv_ln_b, cv_w_pw2, cv_b_pw2,
              peer_w_q, peer_sub_keys, peer_u, peer_v):
    xp, xs = x_prompt, x_sample
    lb_all = jnp.cumsum(jax.nn.softmax(hg_lb_logits.astype(F32), axis=0), axis=0)
    hg_p, hg_s, cv_p, cv_s = [], [], [], []
    for i in range(DEPTH):
        hp = _rmsnorm(xp, norm_mix[i])
        hs = _rmsnorm(xs, norm_mix[i])
        j = i // N_MIXERS
        if i % N_MIXERS == 0:
            s0 = jnp.zeros((xp.shape[0], HG_HEADS, HG_DK, HG_DV), F32)
            yp, sp = _hgrn_mixer(hp, s0, hg_w_in[j], hg_w_out[j], hg_gnorm[j], lb_all[i])
            ys, ss = _hgrn_mixer(hs, state_hgrn[j], hg_w_in[j], hg_w_out[j], hg_gnorm[j], lb_all[i])
            hg_p.append(sp)
            hg_s.append(ss)
        else:
            cw = (cv_w_pw1[j], cv_b_pw1[j], cv_w_dw[j], cv_b_dw[j], cv_ln_g[j], cv_ln_b[j],
                  cv_w_pw2[j], cv_b_pw2[j])
            b0 = jnp.zeros((xp.shape[0], CONV_WIDTH - 1, D_MODEL), xp.dtype)
            yp, bp = _conformer_conv(hp, b0, *cw)
            ys, bs = _conformer_conv(hs, state_conv[j], *cw)
            cv_p.append(bp)
            cv_s.append(bs)
        xp = xp + yp
        xs = xs + ys
        xp = xp + _peer(_rmsnorm(xp, norm_ffn[i]), peer_w_q[i], peer_sub_keys[i], peer_u[i], peer_v[i])
        xs = xs + _peer(_rmsnorm(xs, norm_ffn[i]), peer_w_q[i], peer_sub_keys[i], peer_u[i], peer_v[i])
    y_prompt = _rmsnorm(xp, norm_final)
    y_sample = _rmsnorm(xs, norm_final)
    return (y_prompt, y_sample, jnp.stack(hg_p), jnp.stack(hg_s), jnp.stack(cv_p), jnp.stack(cv_s))
```

```python
import functools

import jax
import jax.numpy as jnp
from jax import lax
from jax.experimental import pallas as pl
from jax.experimental.pallas import tpu as pltpu

F32 = jnp.float32
BF16 = jnp.bfloat16
EPS = 1e-6

HEADS = 8
HEAD_DIM = 128
TOPK = 16
CONV_WIDTH = 31
LANES = 128
SUB = 16
VMEM_LIMIT = 56 * 1024 * 1024

NEG_INF = float("-inf")


def _rms(x, g):
    return x * lax.rsqrt(jnp.mean(x * x, axis=-1, keepdims=True) + EPS) * g


def _sigmoid(x):
    return 1.0 / (1.0 + jnp.exp(-x))


def _dot(a, b):
    return jnp.dot(a, b, preferred_element_type=F32)


def _dot_nt(a, b):
    return lax.dot_general(a, b, (((1,), (1,)), ((), ())), preferred_element_type=F32)


def _hgrn_prompt_kernel(x_ref, gmix_ref, win_ref, wout_ref, gn_ref, lb_ref, tril_ref, ones_ref,
                        y_ref, s_ref,
                        st_scr, q_scr, k_scr, v_scr, b_scr, p_scr, a_scr, o_scr):
    C = x_ref.shape[1]
    D = x_ref.shape[2]
    nsub = C // SUB
    ci = pl.program_id(1)

    @pl.when(ci == 0)
    def _():
        st_scr[...] = jnp.zeros_like(st_scr)

    xc = x_ref[0]
    hn = _rms(xc, gmix_ref[...])
    proj = _dot(hn.astype(BF16), win_ref[...])
    qz = proj[:, 0:D]
    fz = proj[:, D:2 * D]
    gz = proj[:, 3 * D:4 * D]
    lb = lb_ref[...]
    f = lb + (1.0 - lb) * _sigmoid(fz)
    q_scr[...] = qz * _sigmoid(qz)
    k_scr[...] = 1.0 - f
    v_scr[...] = proj[:, 2 * D:3 * D]
    b_scr[...] = jnp.dot(tril_ref[...], jnp.log(f), precision=lax.Precision.HIGHEST,
                         preferred_element_type=F32)

    qe = (q_scr[...] * jnp.exp(b_scr[...])).astype(BF16)
    for h in range(HEADS):
        L = slice(h * HEAD_DIM, (h + 1) * HEAD_DIM)
        o_scr[:, L] = _dot_nt(qe[:, L], st_scr[h].astype(BF16))

    t_iota = lax.broadcasted_iota(jnp.int32, (SUB, D), 0)
    for I in range(nsub):
        rows = slice(SUB * I, SUB * (I + 1))
        bI = b_scr[rows, :]
        qI = q_scr[rows, :]
        if I > 0:
            beta = b_scr[SUB * I - 1:SUB * I, :]
            qs = (qI * jnp.exp(bI - beta)).astype(BF16)
            ks = (k_scr[0:SUB * I, :] * jnp.exp(beta - b_scr[0:SUB * I, :])).astype(BF16)
            vs = v_scr[0:SUB * I, :].astype(BF16)
            for h in range(HEADS):
                L = slice(h * HEAD_DIM, (h + 1) * HEAD_DIM)
                att = _dot_nt(qs[:, L], ks[:, L])
                o_scr[rows, L] += _dot(att.astype(BF16), vs[:, L])
        for s in range(SUB):
            r = SUB * I + s
            keep = t_iota >= s
            dec = jnp.exp(jnp.where(keep, bI - b_scr[r:r + 1, :], 0.0))
            p = jnp.where(keep, qI * dec * k_scr[r:r + 1, :], 0.0)
            p_scr[r * SUB:(r + 1) * SUB, :] = p.astype(BF16)

    for h in range(HEADS):
        L = slice(h * HEAD_DIM, (h + 1) * HEAD_DIM)
        a_scr[:, L] = _dot(p_scr[:, L], ones_ref[...])
    for I in range(nsub):
        rows = slice(SUB * I, SUB * (I + 1))
        acc = o_scr[rows, :]
        for s in range(SUB):
            r = SUB * I + s
            acc = acc + a_scr[r * SUB:(r + 1) * SUB, :] * v_scr[r:r + 1, :]
        o_scr[rows, :] = acc

    b_last = b_scr[C - 1:C, :]
    kd = (k_scr[...] * jnp.exp(b_last - b_scr[...])).astype(BF16)
    eb_last = jnp.exp(b_last)
    for h in range(HEADS):
        L = slice(h * HEAD_DIM, (h + 1) * HEAD_DIM)
        vt = v_scr[:, L].T.astype(BF16)
        st_scr[h] = st_scr[h] * eb_last[:, L] + _dot(vt, kd[:, L])

    gn = gn_ref[...]
    for h in range(HEADS):
        L = slice(h * HEAD_DIM, (h + 1) * HEAD_DIM)
        oh = o_scr[:, L]
        gzh = gz[:, L]
        o_scr[:, L] = _rms(oh, gn) * (gzh * _sigmoid(gzh))
    y_ref[0] = xc + _dot(o_scr[...].astype(BF16), wout_ref[...])

    @pl.when(ci == pl.num_programs(1) - 1)
    def _():
        for h in range(HEADS):
            s_ref[0, h] = st_scr[h].T


def _hgrn_prompt(x, gmix, w_in, w_out, gnorm, lb, chunk):
    B, T, D = x.shape
    C = min(chunk, T)
    assert T % C == 0 and C % SUB == 0
    tril = jnp.tril(jnp.ones((C, C), F32))
    ones = jnp.ones((HEAD_DIM, HEAD_DIM), BF16)
    const = lambda shape: pl.BlockSpec(shape, lambda b, c: (0,) * len(shape))
    return pl.pallas_call(
        _hgrn_prompt_kernel,
        grid=(B, T // C),
        in_specs=[
            pl.BlockSpec((1, C, D), lambda b, c: (b, c, 0)),
            const((1, D)), const((D, 4 * D)), const((D, D)), const((1, HEAD_DIM)), const((1, D)),
            const((C, C)), const((HEAD_DIM, HEAD_DIM)),
        ],
        out_specs=[
            pl.BlockSpec((1, C, D), lambda b, c: (b, c, 0)),
            pl.BlockSpec((1, HEADS, HEAD_DIM, HEAD_DIM), lambda b, c: (b, 0, 0, 0)),
        ],
        out_shape=[
            jax.ShapeDtypeStruct((B, T, D), F32),
            jax.ShapeDtypeStruct((B, HEADS, HEAD_DIM, HEAD_DIM), F32),
        ],
        scratch_shapes=[
            pltpu.VMEM((HEADS, HEAD_DIM, HEAD_DIM), F32),
            pltpu.VMEM((C, D), F32),
            pltpu.VMEM((C, D), F32),
            pltpu.VMEM((C, D), F32),
            pltpu.VMEM((C, D), F32),
            pltpu.VMEM((C * SUB, D), BF16),
            pltpu.VMEM((C * SUB, D), F32),
            pltpu.VMEM((C, D), F32),
        ],
        compiler_params=pltpu.CompilerParams(
            dimension_semantics=("arbitrary", "arbitrary"), vmem_limit_bytes=VMEM_LIMIT),
        name="hgrn_prompt",
    )(x, gmix, w_in, w_out, gnorm, lb, tril, ones)


def _hgrn_sample_kernel(x_ref, gmix_ref, win_ref, wout_ref, gn_ref, lb_ref, s_ref,
                        y_ref, snew_ref, proj_scr, o_scr):
    n = pl.program_id(0)
    D = x_ref.shape[1]

    @pl.when(n == 0)
    def _():
        hn = _rms(x_ref[...], gmix_ref[...])
        proj_scr[...] = _dot(hn.astype(BF16), win_ref[...])

    row = proj_scr[pl.ds(n, 1), :]
    qz = row[:, 0:D]
    fz = row[:, D:2 * D]
    v = row[:, 2 * D:3 * D]
    lb = lb_ref[...]
    f = lb + (1.0 - lb) * _sigmoid(fz)
    q = qz * _sigmoid(qz)
    k = 1.0 - f

    def cols(r):
        stacked = jnp.concatenate(
            [r[:, h * HEAD_DIM:(h + 1) * HEAD_DIM] for h in range(HEADS)]
            + [jnp.zeros((HEAD_DIM - HEADS, HEAD_DIM), F32)], axis=0)
        return stacked.T

    fc = cols(f)
    kc = cols(k)
    qc = cols(q)
    outs = []
    for h in range(HEADS):
        L = slice(h * HEAD_DIM, (h + 1) * HEAD_DIM)
        s_new = s_ref[0, h] * fc[:, h:h + 1] + kc[:, h:h + 1] * v[:, L]
        snew_ref[0, h] = s_new
        outs.append(jnp.sum(qc[:, h:h + 1] * s_new, axis=0, keepdims=True))
    o_scr[pl.ds(n, 1), :] = jnp.concatenate(outs, axis=1)

    @pl.when(n == pl.num_programs(0) - 1)
    def _():
        gn = gn_ref[...]
        gz = proj_scr[:, 3 * D:4 * D]
        for h in range(HEADS):
            L = slice(h * HEAD_DIM, (h + 1) * HEAD_DIM)
            gzh = gz[:, L]
            o_scr[:, L] = _rms(o_scr[:, L], gn) * (gzh * _sigmoid(gzh))
        y_ref[...] = x_ref[...] + _dot(o_scr[...].astype(BF16), wout_ref[...])


def _hgrn_sample(x, gmix, w_in, w_out, gnorm, lb, state):
    N, D = x.shape
    const = lambda shape: pl.BlockSpec(shape, lambda n: (0,) * len(shape))
    sblk = pl.BlockSpec((1, HEADS, HEAD_DIM, HEAD_DIM), lambda n: (n, 0, 0, 0))
    return pl.pallas_call(
        _hgrn_sample_kernel,
        grid=(N,),
        in_specs=[const((N, D)), const((1, D)), const((D, 4 * D)), const((D, D)),
                  const((1, HEAD_DIM)), const((1, D)), sblk],
        out_specs=[const((N, D)), sblk],
        out_shape=[jax.ShapeDtypeStruct((N, D), F32),
                   jax.ShapeDtypeStruct((N, HEADS, HEAD_DIM, HEAD_DIM), F32)],
        scratch_shapes=[pltpu.VMEM((N, 4 * D), F32), pltpu.VMEM((N, D), F32)],
        compiler_params=pltpu.CompilerParams(
            dimension_semantics=("arbitrary",), vmem_limit_bytes=VMEM_LIMIT),
        name="hgrn_sample",
    )(x, gmix, w_in, w_out, gnorm, lb, state)


CONV_PAD = 32


def _conv_tail(y, xc, bdw_ref, lng_ref, lnb_ref, pw2_ref, bpw2_ref):
    y = y + bdw_ref[...]
    mu = jnp.mean(y, axis=-1, keepdims=True)
    yc = y - mu
    var = jnp.mean(yc * yc, axis=-1, keepdims=True)
    yn = yc * lax.rsqrt(var + EPS) * lng_ref[...] + lnb_ref[...]
    z = yn * _sigmoid(yn)
    return xc + _dot(z.astype(BF16), pw2_ref[...]) + bpw2_ref[...]


def _glu(xc, gmix_ref, pw1_ref, bpw1_ref):
    D = xc.shape[1]
    hn = _rms(xc, gmix_ref[...])
    h2 = _dot(hn.astype(BF16), pw1_ref[...]) + bpw1_ref[...]
    return h2[:, 0:D] * _sigmoid(h2[:, D:2 * D])


def _conv_prompt_kernel(x_ref, gmix_ref, pw1_ref, bpw1_ref, wdw_ref, bdw_ref, lng_ref, lnb_ref,
                        pw2_ref, bpw2_ref, y_ref, buf_ref, full_scr):
    C = x_ref.shape[1]
    ci = pl.program_id(1)
    hist = CONV_WIDTH - 1
    off = CONV_PAD - hist

    @pl.when(ci == 0)
    def _():
        full_scr[0:CONV_PAD, :] = jnp.zeros((CONV_PAD, full_scr.shape[1]), F32)

    xc = x_ref[0]
    full_scr[CONV_PAD:CONV_PAD + C, :] = _glu(xc, gmix_ref, pw1_ref, bpw1_ref)
    y = full_scr[off:off + C, :] * wdw_ref[0:1, :]
    for j in range(1, CONV_WIDTH):
        y = y + full_scr[off + j:off + j + C, :] * wdw_ref[j:j + 1, :]
    y_ref[0] = _conv_tail(y, xc, bdw_ref, lng_ref, lnb_ref, pw2_ref, bpw2_ref)

    @pl.when(ci == pl.num_programs(1) - 1)
    def _():
        buf_ref[0] = full_scr[C + off:C + CONV_PAD, :]

    full_scr[0:CONV_PAD, :] = full_scr[C:C + CONV_PAD, :]


def _conv_prompt(x, gmix, pw1, bpw1, wdw, bdw, lng, lnb, pw2, bpw2, chunk):
    B, T, D = x.shape
    C = min(chunk, T)
    assert T % C == 0 and C >= CONV_PAD
    hist = CONV_WIDTH - 1
    const = lambda shape: pl.BlockSpec(shape, lambda b, c: (0,) * len(shape))
    return pl.pallas_call(
        _conv_prompt_kernel,
        grid=(B, T // C),
        in_specs=[pl.BlockSpec((1, C, D), lambda b, c: (b, c, 0)),
                  const((1, D)), const((D, 2 * D)), const((1, 2 * D)), const((CONV_WIDTH, D)),
                  const((1, D)), const((1, D)), const((1, D)), const((D, D)), const((1, D))],
        out_specs=[pl.BlockSpec((1, C, D), lambda b, c: (b, c, 0)),
                   pl.BlockSpec((1, hist, D), lambda b, c: (b, 0, 0))],
        out_shape=[jax.ShapeDtypeStruct((B, T, D), F32),
                   jax.ShapeDtypeStruct((B, hist, D), F32)],
        scratch_shapes=[pltpu.VMEM((CONV_PAD + C, D), F32)],
        compiler_params=pltpu.CompilerParams(
            dimension_semantics=("arbitrary", "arbitrary"), vmem_limit_bytes=VMEM_LIMIT),
        name="conv_prompt",
    )(x, gmix, pw1, bpw1, wdw, bdw, lng, lnb, pw2, bpw2)


def _conv_sample_kernel(x_ref, gmix_ref, pw1_ref, bpw1_ref, wdw_ref, bdw_ref, lng_ref, lnb_ref,
                        pw2_ref, bpw2_ref, buf_ref, y_ref, nbuf_ref, u_scr, y_scr):
    nt = x_ref.shape[0]
    hist = CONV_WIDTH - 1
    xc = x_ref[...]
    u_scr[...] = _glu(xc, gmix_ref, pw1_ref, bpw1_ref)
    w_hist = wdw_ref[0:hist, :]
    w_last = wdw_ref[hist:hist + 1, :]

    def body(n, carry):
        buf = buf_ref[n]
        u = u_scr[pl.ds(n, 1), :]
        y_scr[pl.ds(n, 1), :] = jnp.sum(buf * w_hist, axis=0, keepdims=True) + u * w_last
        nbuf_ref[n, 0:hist - 1, :] = buf[1:hist, :]
        nbuf_ref[n, hist - 1:hist, :] = u
        return carry

    lax.fori_loop(0, nt, body, 0)
    y_ref[...] = _conv_tail(y_scr[...], xc, bdw_ref, lng_ref, lnb_ref, pw2_ref, bpw2_ref)


def _conv_sample(x, gmix, pw1, bpw1, wdw, bdw, lng, lnb, pw2, bpw2, buf, block):
    N, D = x.shape
    nt = min(block, N)
    assert N % nt == 0
    hist = CONV_WIDTH - 1
    const = lambda shape: pl.BlockSpec(shape, lambda i: (0,) * len(shape))
    return pl.pallas_call(
        _conv_sample_kernel,
        grid=(N // nt,),
        in_specs=[pl.BlockSpec((nt, D), lambda i: (i, 0)),
                  const((1, D)), const((D, 2 * D)), const((1, 2 * D)), const((CONV_WIDTH, D)),
                  const((1, D)), const((1, D)), const((1, D)), const((D, D)), const((1, D)),
                  pl.BlockSpec((nt, hist, D), lambda i: (i, 0, 0))],
        out_specs=[pl.BlockSpec((nt, D), lambda i: (i, 0)),
                   pl.BlockSpec((nt, hist, D), lambda i: (i, 0, 0))],
        out_shape=[jax.ShapeDtypeStruct((N, D), F32),
                   jax.ShapeDtypeStruct((N, hist, D), F32)],
        scratch_shapes=[pltpu.VMEM((nt, D), F32), pltpu.VMEM((nt, D), F32)],
        compiler_params=pltpu.CompilerParams(
            dimension_semantics=("arbitrary",), vmem_limit_bytes=VMEM_LIMIT),
        name="conv_sample",
    )(x, gmix, pw1, bpw1, wdw, bdw, lng, lnb, pw2, bpw2, buf)


def _topk_ranks(vals, ids, k):
    rank = jnp.full(vals.shape, float(k), F32)
    tops = []
    for r in range(k):
        m = jnp.max(vals, axis=0, keepdims=True)
        first = jnp.min(jnp.where(vals == m, ids, 1e9), axis=0, keepdims=True)
        sel = ids == first
        vals = jnp.where(sel, NEG_INF, vals)
        rank = jnp.where(sel, float(r), rank)
        tops.append(m)
    return rank, tops


def _router_kernel(x_ref, g_ref, wq_ref, keys_ref,
                   xnt_ref, rank2_ref, e2_ref, c1_ref, e1_ref,
                   s_scr, rank_scr, top_scr):
    tn = x_ref.shape[0]
    nch = tn // LANES
    xn = _rms(x_ref[...], g_ref[...])
    xnt = xn.T.astype(BF16)
    xnt_ref[...] = xnt
    qt = _dot(wq_ref[...], xnt).astype(BF16)
    for hp in range(2 * HEADS):
        s_scr[hp] = _dot(keys_ref[hp], qt[hp * HEAD_DIM:(hp + 1) * HEAD_DIM, :])

    key_ids = lax.broadcasted_iota(jnp.int32, (HEAD_DIM, LANES), 0).astype(F32)

    def half_body(it, carry):
        hp = it // nch
        lanes = pl.ds(pl.multiple_of((it % nch) * LANES, LANES), LANES)
        rank, tops = _topk_ranks(s_scr[hp, :, lanes], key_ids, TOPK)
        rank_scr[hp, :, lanes] = rank
        top_scr[hp, :, lanes] = jnp.concatenate(tops, axis=0)
        return carry

    lax.fori_loop(0, 2 * HEADS * nch, half_body, 0)

    row16 = lax.broadcasted_iota(jnp.int32, (16, LANES), 0)
    row8 = lax.broadcasted_iota(jnp.int32, (8, LANES), 0)
    r16 = row16.astype(F32)
    r8 = row8.astype(F32)
    groups_valid = [row16 >= 0, row8 >= 0, row16 >= 2, row8 >= 2,
                    (row8 >= 2) & (row8 <= 4), (row8 >= 2) & (row8 <= 3), row8 == 2]
    groups_ids = [r16, 16.0 + r8, r16 * 16.0, r8 * 16.0 + 1.0, 32.0 + r8, 48.0 + r8, 64.0 + r8]
    cand_ids = jnp.concatenate(
        [jnp.where(v, i, 1e9) for v, i in zip(groups_valid, groups_ids)], axis=0)

    def head_body(it, carry):
        h = it // nch
        lanes = pl.ds(pl.multiple_of((it % nch) * LANES, LANES), LANES)
        v1 = top_scr[2 * h, :, lanes]
        v2 = top_scr[2 * h + 1, :, lanes]
        groups = [v1[0:1] + v2, v1[1:2] + v2[0:8], v1 + v2[0:1], v1[0:8] + v2[1:2],
                  v1[2:3] + v2[0:8], v1[3:4] + v2[0:8], v1[4:5] + v2[0:8]]
        cand = jnp.concatenate(
            [jnp.where(v, g, NEG_INF) for v, g in zip(groups_valid, groups)], axis=0)
        crank, _ = _topk_ranks(cand, cand_ids, TOPK)
        sel = crank < float(TOPK)
        z = jnp.sum(jnp.where(sel, jnp.exp(jnp.where(sel, cand - cand[0:1], 0.0)), 0.0),
                    axis=0, keepdims=True)
        self32 = jnp.where(sel, 1.0, 0.0)
        cnt = lambda a, b: jnp.sum(self32[a:b], axis=0, keepdims=True)
        per_r1 = (self32[24:40]
                  + jnp.concatenate([self32[40:48], jnp.zeros((8, LANES), F32)], axis=0)
                  + jnp.where(row16 == 0, cnt(0, 16), 0.0)
                  + jnp.where(row16 == 1, cnt(16, 24), 0.0)
                  + jnp.where(row16 == 2, cnt(48, 56), 0.0)
                  + jnp.where(row16 == 3, cnt(56, 64), 0.0)
                  + jnp.where(row16 == 4, cnt(64, 72), 0.0))
        rank1 = rank_scr[2 * h, :, lanes]
        rank2 = rank_scr[2 * h + 1, :, lanes]
        c1 = jnp.zeros((HEAD_DIM, LANES), F32)
        for r in range(TOPK):
            c1 = jnp.where(rank1 == float(r), per_r1[r:r + 1], c1)
        s1 = s_scr[2 * h, :, lanes]
        s2 = s_scr[2 * h + 1, :, lanes]
        in1 = rank1 < float(TOPK)
        in2 = rank2 < float(TOPK)
        e1 = jnp.where(in1, jnp.exp(jnp.where(in1, s1 - v1[0:1], 0.0)), 0.0)
        e2 = jnp.where(in2, jnp.exp(jnp.where(in2, s2 - v2[0:1], 0.0)), 0.0) / z
        rank2_ref[h, :, lanes] = rank2
        e2_ref[h, :, lanes] = e2
        c1_ref[h, :, lanes] = c1
        e1_ref[h, :, lanes] = e1
        return carry

    lax.fori_loop(0, HEADS * nch, head_body, 0)


def _router(x, g, wq_t, keys, tn):
    N, D = x.shape
    assert N % tn == 0 and tn % LANES == 0
    const = lambda shape: pl.BlockSpec(shape, lambda i: (0,) * len(shape))
    gate_blk = pl.BlockSpec((HEADS, HEAD_DIM, tn), lambda i: (0, 0, i))
    gate_shape = jax.ShapeDtypeStruct((HEADS, HEAD_DIM, N), F32)
    return pl.pallas_call(
        _router_kernel,
        grid=(N // tn,),
        in_specs=[pl.BlockSpec((tn, D), lambda i: (i, 0)), const((1, D)),
                  const(wq_t.shape), const(keys.shape)],
        out_specs=[pl.BlockSpec((D, tn), lambda i: (0, i)),
                   gate_blk, gate_blk, gate_blk, gate_blk],
        out_shape=[jax.ShapeDtypeStruct((D, N), BF16),
                   gate_shape, gate_shape, gate_shape, gate_shape],
        scratch_shapes=[pltpu.VMEM((2 * HEADS, HEAD_DIM, tn), F32),
                        pltpu.VMEM((2 * HEADS, HEAD_DIM, tn), F32),
                        pltpu.VMEM((2 * HEADS, TOPK, tn), F32)],
        compiler_params=pltpu.CompilerParams(
            dimension_semantics=("arbitrary",), vmem_limit_bytes=VMEM_LIMIT),
        name="peer_router",
    )(x, g, wq_t, keys)


ROW_TILE = 16


def _peer_dense_kernel(xnt_ref, u_ref, vt_ref, rank2_ref, e2_ref, c1_ref, e1_ref, x_ref, gfin_ref,
                       y_ref, hid_scr, g_scr, acc_scr, *, final_norm):
    j = pl.program_id(1)
    te = u_ref.shape[0]
    nb = te // HEAD_DIM

    @pl.when(j == 0)
    def _():
        acc_scr[...] = jnp.zeros_like(acc_scr)

    hid_scr[...] = _dot(u_ref[...], xnt_ref[...])

    def tile_body(it, carry):
        il = it // (HEAD_DIM // ROW_TILE)
        i1 = j * nb + il
        ro = pl.multiple_of((it % (HEAD_DIM // ROW_TILE)) * ROW_TILE, ROW_TILE)
        rows = pl.ds(pl.multiple_of(il * HEAD_DIM + ro, ROW_TILE), ROW_TILE)
        hid = hid_scr[rows, :]
        w = jnp.zeros(hid.shape, F32)
        for h in range(HEADS):
            c1 = c1_ref[h, pl.ds(i1, 1), :]
            e1 = e1_ref[h, pl.ds(i1, 1), :]
            rk = rank2_ref[h, pl.ds(ro, ROW_TILE), :]
            e2 = e2_ref[h, pl.ds(ro, ROW_TILE), :]
            w = w + jnp.where(rk < c1, e2 * e1, 0.0)
        gelu = 0.5 * hid * (1.0 + lax.erf(hid * 0.7071067811865476))
        g_scr[rows, :] = (w * gelu).astype(BF16)
        return carry

    lax.fori_loop(0, nb * (HEAD_DIM // ROW_TILE), tile_body, 0)
    acc_scr[...] += _dot(vt_ref[...], g_scr[...])

    @pl.when(j == pl.num_programs(1) - 1)
    def _():
        y = x_ref[...] + acc_scr[...].T
        if final_norm:
            y = _rms(y, gfin_ref[...])
        y_ref[...] = y


def _peer_dense(xnt, u, vt, rank2, e2, c1, e1, x, gfin, tn, te, final_norm):
    N, D = x.shape
    E = u.shape[0]
    assert N % tn == 0 and E % te == 0 and te % HEAD_DIM == 0
    gate_blk = pl.BlockSpec((HEADS, HEAD_DIM, tn), lambda i, j: (0, 0, i))
    return pl.pallas_call(
        functools.partial(_peer_dense_kernel, final_norm=final_norm),
        grid=(N // tn, E // te),
        in_specs=[pl.BlockSpec((D, tn), lambda i, j: (0, i)),
                  pl.BlockSpec((te, D), lambda i, j: (j, 0)),
                  pl.BlockSpec((D, te), lambda i, j: (0, j)),
                  gate_blk, gate_blk, gate_blk, gate_blk,
                  pl.BlockSpec((tn, D), lambda i, j: (i, 0)),
                  pl.BlockSpec((1, D), lambda i, j: (0, 0))],
        out_specs=pl.BlockSpec((tn, D), lambda i, j: (i, 0)),
        out_shape=jax.ShapeDtypeStruct((N, D), F32),
        scratch_shapes=[pltpu.VMEM((te, tn), F32), pltpu.VMEM((te, tn), BF16),
                        pltpu.VMEM((D, tn), F32)],
        compiler_params=pltpu.CompilerParams(
            dimension_semantics=("arbitrary", "arbitrary"), vmem_limit_bytes=VMEM_LIMIT),
        name="peer_dense",
    )(xnt, u, vt, rank2, e2, c1, e1, x, gfin)


def _peer(x, g, wq_t, keys, u, vt, gfin, tn, te, final_norm):
    xnt, rank2, e2, c1, e1 = _router(x, g, wq_t, keys, tn)
    return _peer_dense(xnt, u, vt, rank2, e2, c1, e1, x, gfin, tn, te, final_norm)


HGRN_CHUNK = 64
CONV_CHUNK = 256
CONV_SAMPLE_BLOCK = 32
PEER_TOKENS = 512
PEER_EXPERTS = 512


def kernel(x_prompt, x_sample, state_hgrn, state_conv, norm_mix, norm_ffn, norm_final, hg_w_in, hg_w_out, hg_gnorm, hg_lb_logits, cv_w_pw1, cv_b_pw1, cv_w_dw, cv_b_dw, cv_ln_g, cv_ln_b, cv_w_pw2, cv_b_pw2, peer_w_q, peer_sub_keys, peer_u, peer_v):
    B, T, D = x_prompt.shape
    NS = x_sample.shape[0]
    depth = norm_mix.shape[0]
    row = lambda a: a.reshape(1, -1)

    lb_all = jnp.cumsum(jax.nn.softmax(hg_lb_logits.astype(F32), axis=0), axis=0)

    xp = x_prompt
    xs = x_sample.reshape(NS, D)
    hg_p, hg_s, cv_p, cv_s = [], [], [], []
    for i in range(depth):
        j = i // 2
        gmix = row(norm_mix[i])
        if i % 2 == 0:
            w_in = hg_w_in[j].astype(BF16)
            w_out = hg_w_out[j].astype(BF16)
            gn = row(hg_gnorm[j])
            lb = row(lb_all[i])
            xp, sp = _hgrn_prompt(xp, gmix, w_in, w_out, gn, lb, HGRN_CHUNK)
            xs, ss = _hgrn_sample(xs, gmix, w_in, w_out, gn, lb, state_hgrn[j])
            hg_p.append(sp)
            hg_s.append(ss)
        else:
            cw = (cv_w_pw1[j].astype(BF16), row(cv_b_pw1[j]), cv_w_dw[j], row(cv_b_dw[j]),
                  row(cv_ln_g[j]), row(cv_ln_b[j]), cv_w_pw2[j].astype(BF16), row(cv_b_pw2[j]))
            xp, bp = _conv_prompt(xp, gmix, *cw, CONV_CHUNK)
            xs, bs = _conv_sample(xs, gmix, *cw, state_conv[j], CONV_SAMPLE_BLOCK)
            cv_p.append(bp)
            cv_s.append(bs)
        wq_t = peer_w_q[i].T.astype(BF16)
        keys = peer_sub_keys[i].reshape(2 * HEADS, HEAD_DIM, HEAD_DIM).astype(BF16)
        u = peer_u[i].astype(BF16)
        vt = peer_v[i].T.astype(BF16)
        gffn = row(norm_ffn[i])
        gfin = row(norm_final)
        last = i == depth - 1
        xp = _peer(xp.reshape(B * T, D), gffn, wq_t, keys, u, vt, gfin,
                   min(PEER_TOKENS, B * T), PEER_EXPERTS, last).reshape(B, T, D)
        xs = _peer(xs, gffn, wq_t, keys, u, vt, gfin, min(LANES, NS), PEER_EXPERTS, last)
    return (xp, xs.reshape(NS, 1, D), jnp.stack(hg_p), jnp.stack(hg_s),
            jnp.stack(cv_p), jnp.stack(cv_s))
```

```python
import functools

import jax
import jax.numpy as jnp
from jax import lax
from jax.experimental import pallas as pl
from jax.experimental.pallas import tpu as pltpu

F32 = jnp.float32
BF16 = jnp.bfloat16
EPS = 1e-6

HEADS = 8
HEAD_DIM = 128
TOPK = 16
CONV_WIDTH = 31
LANES = 128
SUB = 16
VMEM_LIMIT = 56 * 1024 * 1024

NEG_INF = float("-inf")


def _rms(x, g):
    return x * lax.rsqrt(jnp.mean(x * x, axis=-1, keepdims=True) + EPS) * g


def _sigmoid(x):
    return 1.0 / (1.0 + jnp.exp(-x))


def _dot(a, b):
    return jnp.dot(a, b, preferred_element_type=F32)


def _dot_nt(a, b):
    return lax.dot_general(a, b, (((1,), (1,)), ((), ())), preferred_element_type=F32)


def _hgrn_prompt_kernel(x_ref, gmix_ref, win_ref, wout_ref, gn_ref, lb_ref, tril_ref, ones_ref,
                        y_ref, s_ref,
                        st_scr, q_scr, k_scr, v_scr, b_scr, p_scr, a_scr, o_scr):
    C = x_ref.shape[1]
    D = x_ref.shape[2]
    nsub = C // SUB
    ci = pl.program_id(1)

    @pl.when(ci == 0)
    def _():
        st_scr[...] = jnp.zeros_like(st_scr)

    xc = x_ref[0]
    hn = _rms(xc, gmix_ref[...])
    proj = _dot(hn.astype(BF16), win_ref[...])
    qz = proj[:, 0:D]
    fz = proj[:, D:2 * D]
    gz = proj[:, 3 * D:4 * D]
    lb = lb_ref[...]
    f = lb + (1.0 - lb) * _sigmoid(fz)
    q_scr[...] = qz * _sigmoid(qz)
    k_scr[...] = 1.0 - f
    v_scr[...] = proj[:, 2 * D:3 * D]
    b_scr[...] = jnp.dot(tril_ref[...], jnp.log(f), precision=lax.Precision.HIGHEST,
                         preferred_element_type=F32)

    qe = (q_scr[...] * jnp.exp(b_scr[...])).astype(BF16)
    for h in range(HEADS):
        L = slice(h * HEAD_DIM, (h + 1) * HEAD_DIM)
        o_scr[:, L] = _dot_nt(qe[:, L], st_scr[h].astype(BF16))

    t_iota = lax.broadcasted_iota(jnp.int32, (SUB, D), 0)
    for I in range(nsub):
        rows = slice(SUB * I, SUB * (I + 1))
        bI = b_scr[rows, :]
        qI = q_scr[rows, :]
        if I > 0:
            beta = b_scr[SUB * I - 1:SUB * I, :]
            qs = (qI * jnp.exp(bI - beta)).astype(BF16)
            ks = (k_scr[0:SUB * I, :] * jnp.exp(beta - b_scr[0:SUB * I, :])).astype(BF16)
            vs = v_scr[0:SUB * I, :].astype(BF16)
            for h in range(HEADS):
                L = slice(h * HEAD_DIM, (h + 1) * HEAD_DIM)
                att = _dot_nt(qs[:, L], ks[:, L])
                o_scr[rows, L] += _dot(att.astype(BF16), vs[:, L])
        for s in range(SUB):
            r = SUB * I + s
            keep = t_iota >= s
            dec = jnp.exp(jnp.where(keep, bI - b_scr[r:r + 1, :], 0.0))
            p = jnp.where(keep, qI * dec * k_scr[r:r + 1, :], 0.0)
            p_scr[r * SUB:(r + 1) * SUB, :] = p.astype(BF16)

    for h in range(HEADS):
        L = slice(h * HEAD_DIM, (h + 1) * HEAD_DIM)
        a_scr[:, L] = _dot(p_scr[:, L], ones_ref[...])
    for I in range(nsub):
        rows = slice(SUB * I, SUB * (I + 1))
        acc = o_scr[rows, :]
        for s in range(SUB):
            r = SUB * I + s
            acc = acc + a_scr[r * SUB:(r + 1) * SUB, :] * v_scr[r:r + 1, :]
        o_scr[rows, :] = acc

    b_last = b_scr[C - 1:C, :]
    kd = (k_scr[...] * jnp.exp(b_last - b_scr[...])).astype(BF16)
    eb_last = jnp.exp(b_last)
    for h in range(HEADS):
        L = slice(h * HEAD_DIM, (h + 1) * HEAD_DIM)
        vt = v_scr[:, L].T.astype(BF16)
        st_scr[h] = st_scr[h] * eb_last[:, L] + _dot(vt, kd[:, L])

    gn = gn_ref[...]
    for h in range(HEADS):
        L = slice(h * HEAD_DIM, (h + 1) * HEAD_DIM)
        oh = o_scr[:, L]
        gzh = gz[:, L]
        o_scr[:, L] = _rms(oh, gn) * (gzh * _sigmoid(gzh))
    y_ref[0] = xc + _dot(o_scr[...].astype(BF16), wout_ref[...])

    @pl.when(ci == pl.num_programs(1) - 1)
    def _():
        for h in range(HEADS):
            s_ref[0, h] = st_scr[h].T


def _hgrn_prompt(x, gmix, w_in, w_out, gnorm, lb, chunk):
    B, T, D = x.shape
    C = min(chunk, T)
    assert T % C == 0 and C % SUB == 0
    tril = jnp.tril(jnp.ones((C, C), F32))
    ones = jnp.ones((HEAD_DIM, HEAD_DIM), BF16)
    const = lambda shape: pl.BlockSpec(shape, lambda b, c: (0,) * len(shape))
    return pl.pallas_call(
        _hgrn_prompt_kernel,
        grid=(B, T // C),
        in_specs=[
            pl.BlockSpec((1, C, D), lambda b, c: (b, c, 0)),
            const((1, D)), const((D, 4 * D)), const((D, D)), const((1, HEAD_DIM)), const((1, D)),
            const((C, C)), const((HEAD_DIM, HEAD_DIM)),
        ],
        out_specs=[
            pl.BlockSpec((1, C, D), lambda b, c: (b, c, 0)),
            pl.BlockSpec((1, HEADS, HEAD_DIM, HEAD_DIM), lambda b, c: (b, 0, 0, 0)),
        ],
        out_shape=[
            jax.ShapeDtypeStruct((B, T, D), F32),
            jax.ShapeDtypeStruct((B, HEADS, HEAD_DIM, HEAD_DIM), F32),
        ],
        scratch_shapes=[
            pltpu.VMEM((HEADS, HEAD_DIM, HEAD_DIM), F32),
            pltpu.VMEM((C, D), F32),
            pltpu.VMEM((C, D), F32),
            pltpu.VMEM((C, D), F32),
            pltpu.VMEM((C, D), F32),
            pltpu.VMEM((C * SUB, D), BF16),
            pltpu.VMEM((C * SUB, D), F32),
            pltpu.VMEM((C, D), F32),
        ],
        compiler_params=pltpu.CompilerParams(
            dimension_semantics=("arbitrary", "arbitrary"), vmem_limit_bytes=VMEM_LIMIT),
        name="hgrn_prompt",
    )(x, gmix, w_in, w_out, gnorm, lb, tril, ones)


def _hgrn_sample_kernel(x_ref, gmix_ref, win_ref, wout_ref, gn_ref, lb_ref, s_ref,
                        y_ref, snew_ref, proj_scr, o_scr):
    n = pl.program_id(0)
    D = x_ref.shape[1]

    @pl.when(n == 0)
    def _():
        hn = _rms(x_ref[...], gmix_ref[...])
        proj_scr[...] = _dot(hn.astype(BF16), win_ref[...])

    row = proj_scr[pl.ds(n, 1), :]
    qz = row[:, 0:D]
    fz = row[:, D:2 * D]
    v = row[:, 2 * D:3 * D]
    lb = lb_ref[...]
    f = lb + (1.0 - lb) * _sigmoid(fz)
    q = qz * _sigmoid(qz)
    k = 1.0 - f

    def cols(r):
        stacked = jnp.concatenate(
            [r[:, h * HEAD_DIM:(h + 1) * HEAD_DIM] for h in range(HEADS)]
            + [jnp.zeros((HEAD_DIM - HEADS, HEAD_DIM), F32)], axis=0)
        return stacked.T

    fc = cols(f)
    kc = cols(k)
    qc = cols(q)
    outs = []
    for h in range(HEADS):
        L = slice(h * HEAD_DIM, (h + 1) * HEAD_DIM)
        s_new = s_ref[0, h] * fc[:, h:h + 1] + kc[:, h:h + 1] * v[:, L]
        snew_ref[0, h] = s_new
        outs.append(jnp.sum(qc[:, h:h + 1] * s_new, axis=0, keepdims=True))
    o_scr[pl.ds(n, 1), :] = jnp.concatenate(outs, axis=1)

    @pl.when(n == pl.num_programs(0) - 1)
    def _():
        gn = gn_ref[...]
        gz = proj_scr[:, 3 * D:4 * D]
        for h in range(HEADS):
            L = slice(h * HEAD_DIM, (h + 1) * HEAD_DIM)
            gzh = gz[:, L]
            o_scr[:, L] = _rms(o_scr[:, L], gn) * (gzh * _sigmoid(gzh))
        y_ref[...] = x_ref[...] + _dot(o_scr[...].astype(BF16), wout_ref[...])


def _hgrn_sample(x, gmix, w_in, w_out, gnorm, lb, state):
    N, D = x.shape
    const = lambda shape: pl.BlockSpec(shape, lambda n: (0,) * len(shape))
    sblk = pl.BlockSpec((1, HEADS, HEAD_DIM, HEAD_DIM), lambda n: (n, 0, 0, 0))
    return pl.pallas_call(
        _hgrn_sample_kernel,
        grid=(N,),
        in_specs=[const((N, D)), const((1, D)), const((D, 4 * D)), const((D, D)),
                  const((1, HEAD_DIM)), const((1, D)), sblk],
        out_specs=[const((N, D)), sblk],
        out_shape=[jax.ShapeDtypeStruct((N, D), F32),
                   jax.ShapeDtypeStruct((N, HEADS, HEAD_DIM, HEAD_DIM), F32)],
        scratch_shapes=[pltpu.VMEM((N, 4 * D), F32), pltpu.VMEM((N, D), F32)],
        compiler_params=pltpu.CompilerParams(
            dimension_semantics=("arbitrary",), vmem_limit_bytes=VMEM_LIMIT),
        name="hgrn_sample",
    )(x, gmix, w_in, w_out, gnorm, lb, state)


CONV_PAD = 32


def _conv_tail(y, xc, bdw_ref, lng_ref, lnb_ref, pw2_ref, bpw2_ref):
    y = y + bdw_ref[...]
    mu = jnp.mean(y, axis=-1, keepdims=True)
    yc = y - mu
    var = jnp.mean(yc * yc, axis=-1, keepdims=True)
    yn = yc * lax.rsqrt(var + EPS) * lng_ref[...] + lnb_ref[...]
    z = yn * _sigmoid(yn)
    return xc + _dot(z.astype(BF16), pw2_ref[...]) + bpw2_ref[...]


def _glu(xc, gmix_ref, pw1_ref, bpw1_ref):
    D = xc.shape[1]
    hn = _rms(xc, gmix_ref[...])
    h2 = _dot(hn.astype(BF16), pw1_ref[...]) + bpw1_ref[...]
    return h2[:, 0:D] * _sigmoid(h2[:, D:2 * D])


def _conv_prompt_kernel(x_ref, gmix_ref, pw1_ref, bpw1_ref, wdw_ref, bdw_ref, lng_ref, lnb_ref,
                        pw2_ref, bpw2_ref, y_ref, buf_ref, full_scr):
    C = x_ref.shape[1]
    ci = pl.program_id(1)
    hist = CONV_WIDTH - 1
    off = CONV_PAD - hist

    @pl.when(ci == 0)
    def _():
        full_scr[0:CONV_PAD, :] = jnp.zeros((CONV_PAD, full_scr.shape[1]), F32)

    xc = x_ref[0]
    full_scr[CONV_PAD:CONV_PAD + C, :] = _glu(xc, gmix_ref, pw1_ref, bpw1_ref)
    y = full_scr[off:off + C, :] * wdw_ref[0:1, :]
    for j in range(1, CONV_WIDTH):
        y = y + full_scr[off + j:off + j + C, :] * wdw_ref[j:j + 1, :]
    y_ref[0] = _conv_tail(y, xc, bdw_ref, lng_ref, lnb_ref, pw2_ref, bpw2_ref)

    @pl.when(ci == pl.num_programs(1) - 1)
    def _():
        buf_ref[0] = full_scr[C + off:C + CONV_PAD, :]

    full_scr[0:CONV_PAD, :] = full_scr[C:C + CONV_PAD, :]


def _conv_prompt(x, gmix, pw1, bpw1, wdw, bdw, lng, lnb, pw2, bpw2, chunk):
    B, T, D = x.shape
    C = min(chunk, T)
    assert T % C == 0 and C >= CONV_PAD
    hist = CONV_WIDTH - 1
    const = lambda shape: pl.BlockSpec(shape, lambda b, c: (0,) * len(shape))
    return pl.pallas_call(
        _conv_prompt_kernel,
        grid=(B, T // C),
        in_specs=[pl.BlockSpec((1, C, D), lambda b, c: (b, c, 0)),
                  const((1, D)), const((D, 2 * D)), const((1, 2 * D)), const((CONV_WIDTH, D)),
                  const((1, D)), const((1, D)), const((1, D)), const((D, D)), const((1, D))],
        out_specs=[pl.BlockSpec((1, C, D), lambda b, c: (b, c, 0)),
                   pl.BlockSpec((1, hist, D), lambda b, c: (b, 0, 0))],
        out_shape=[jax.ShapeDtypeStruct((B, T, D), F32),
                   jax.ShapeDtypeStruct((B, hist, D), F32)],
        scratch_shapes=[pltpu.VMEM((CONV_PAD + C, D), F32)],
        compiler_params=pltpu.CompilerParams(
            dimension_semantics=("arbitrary", "arbitrary"), vmem_limit_bytes=VMEM_LIMIT),
        name="conv_prompt",
    )(x, gmix, pw1, bpw1, wdw, bdw, lng, lnb, pw2, bpw2)


def _conv_sample_kernel(x_ref, gmix_ref, pw1_ref, bpw1_ref, wdw_ref, bdw_ref, lng_ref, lnb_ref,
                        pw2_ref, bpw2_ref, buf_ref, y_ref, nbuf_ref, u_scr, y_scr):
    nt = x_ref.shape[0]
    hist = CONV_WIDTH - 1
    xc = x_ref[...]
    u_scr[...] = _glu(xc, gmix_ref, pw1_ref, bpw1_ref)
    w_hist = wdw_ref[0:hist, :]
    w_last = wdw_ref[hist:hist + 1, :]

    def body(n, carry):
        buf = buf_ref[n]
        u = u_scr[pl.ds(n, 1), :]
        y_scr[pl.ds(n, 1), :] = jnp.sum(buf * w_hist, axis=0, keepdims=True) + u * w_last
        nbuf_ref[n, 0:hist - 1, :] = buf[1:hist, :]
        nbuf_ref[n, hist - 1:hist, :] = u
        return carry

    lax.fori_loop(0, nt, body, 0)
    y_ref[...] = _conv_tail(y_scr[...], xc, bdw_ref, lng_ref, lnb_ref, pw2_ref, bpw2_ref)


def _conv_sample(x, gmix, pw1, bpw1, wdw, bdw, lng, lnb, pw2, bpw2, buf, block):
    N, D = x.shape
    nt = min(block, N)
    assert N % nt == 0
    hist = CONV_WIDTH - 1
    const = lambda shape: pl.BlockSpec(shape, lambda i: (0,) * len(shape))
    return pl.pallas_call(
        _conv_sample_kernel,
        grid=(N // nt,),
        in_specs=[pl.BlockSpec((nt, D), lambda i: (i, 0)),
                  const((1, D)), const((D, 2 * D)), const((1, 2 * D)), const((CONV_WIDTH, D)),
                  const((1, D)), const((1, D)), const((1, D)), const((D, D)), const((1, D)),
                  pl.BlockSpec((nt, hist, D), lambda i: (i, 0, 0))],
        out_specs=[pl.BlockSpec((nt, D), lambda i: (i, 0)),
                   pl.BlockSpec((nt, hist, D), lambda i: (i, 0, 0))],
        out_shape=[jax.ShapeDtypeStruct((N, D), F32),
                   jax.ShapeDtypeStruct((N, hist, D), F32)],
        scratch_shapes=[pltpu.VMEM((nt, D), F32), pltpu.VMEM((nt, D), F32)],
        compiler_params=pltpu.CompilerParams(
            dimension_semantics=("arbitrary",), vmem_limit_bytes=VMEM_LIMIT),
        name="conv_sample",
    )(x, gmix, pw1, bpw1, wdw, bdw, lng, lnb, pw2, bpw2, buf)


def _topk_ranks(vals, ids, k):
    rank = jnp.full(vals.shape, float(k), F32)
    tops = []
    for r in range(k):
        m = jnp.max(vals, axis=0, keepdims=True)
        first = jnp.min(jnp.where(vals == m, ids, 1e9), axis=0, keepdims=True)
        sel = ids == first
        vals = jnp.where(sel, NEG_INF, vals)
        rank = jnp.where(sel, float(r), rank)
        tops.append(m)
    return rank, tops


INT_MIN = -2 ** 31


def _sort_key(s):
    b = lax.bitcast_convert_type(jnp.where(s == 0.0, 0.0, s), jnp.int32)
    return b ^ (lax.shift_right_arithmetic(b, 31) & jnp.int32(0x7FFFFFFF))


def _key_value(k):
    b = k ^ (lax.shift_right_arithmetic(k, 31) & jnp.int32(0x7FFFFFFF))
    return lax.bitcast_convert_type(b, F32)


def _topk_keys(keys, k):
    tops = []
    for r in range(k):
        m = jnp.max(keys, axis=0, keepdims=True)
        keys = jnp.where(keys == m, jnp.int32(INT_MIN + r), keys)
        tops.append(m)
    return keys, tops


def _selected(keys, k):
    sel = keys < jnp.int32(INT_MIN + k)
    n = jnp.sum(jnp.where(sel, 1.0, 0.0), axis=0, keepdims=True)
    return sel, jnp.where(n != float(k), 1.0, 0.0)


def _dup_bf16(x):
    u = lax.bitcast_convert_type(x.astype(BF16).astype(F32), jnp.int32)
    return u | lax.shift_right_logical(u, 16)


def _router_kernel(x_ref, g_ref, wq_ref, keys_ref,
                   xnt_ref, rank2_ref, e2_ref, c1_ref, e1_ref,
                   s_scr, rank_scr, top_scr):
    tn = x_ref.shape[0]
    xn = _rms(x_ref[...], g_ref[...])
    xnt = xn.T.astype(BF16)
    xnt_ref[...] = xnt
    qt = _dot(wq_ref[...], xnt).astype(BF16)
    for hp in range(2 * HEADS):
        s_scr[hp] = _dot(keys_ref[hp], qt[hp * HEAD_DIM:(hp + 1) * HEAD_DIM, :])

    outs = (rank2_ref, e2_ref, c1_ref, e1_ref)
    scr = (s_scr, rank_scr, top_scr)
    bad = _select_experts(tn, scr, outs, exact=False)

    @pl.when(jnp.max(bad) > 0.0)
    def _():
        _select_experts(tn, scr, outs, exact=True)


def _select_experts(tn, scr, outs, exact):
    s_scr, rank_scr, top_scr = scr
    rank2_ref, e2_ref, c1_ref, e1_ref = outs
    nch = tn // LANES
    key_ids = lax.broadcasted_iota(jnp.int32, (HEAD_DIM, LANES), 0).astype(F32)

    def half_body(it, bad):
        hp = it // nch
        lanes = pl.ds(pl.multiple_of((it % nch) * LANES, LANES), LANES)
        s = s_scr[hp, :, lanes]
        if exact:
            rank, tops = _topk_ranks(s, key_ids, TOPK)
            top = jnp.concatenate(tops, axis=0)
        else:
            keys, tops = _topk_keys(_sort_key(s), TOPK)
            sel, miss = _selected(keys, TOPK)
            rank = jnp.where(sel, (keys ^ jnp.int32(INT_MIN)).astype(F32), float(TOPK))
            top = _key_value(jnp.concatenate(tops, axis=0))
            bad = jnp.maximum(bad, miss)
        rank_scr[hp, :, lanes] = rank
        top_scr[hp, :, lanes] = top
        return bad

    unroll = 1 if exact else 2
    bad = lax.fori_loop(0, 2 * HEADS * nch, half_body, jnp.zeros((1, LANES), F32),
                        unroll=unroll)

    row16 = lax.broadcasted_iota(jnp.int32, (16, LANES), 0)
    row8 = lax.broadcasted_iota(jnp.int32, (8, LANES), 0)
    r16 = row16.astype(F32)
    r8 = row8.astype(F32)
    groups_valid = [row16 >= 0, row8 >= 0, row16 >= 2, row8 >= 2,
                    (row8 >= 2) & (row8 <= 4), (row8 >= 2) & (row8 <= 3), row8 == 2]
    groups_ids = [r16, 16.0 + r8, r16 * 16.0, r8 * 16.0 + 1.0, 32.0 + r8, 48.0 + r8, 64.0 + r8]
    cand_ids = jnp.concatenate(
        [jnp.where(v, i, 1e9) for v, i in zip(groups_valid, groups_ids)], axis=0)

    def head_body(it, bad):
        h = it // nch
        lanes = pl.ds(pl.multiple_of((it % nch) * LANES, LANES), LANES)
        v1 = top_scr[2 * h, :, lanes]
        v2 = top_scr[2 * h + 1, :, lanes]
        groups = [v1[0:1] + v2, v1[1:2] + v2[0:8], v1 + v2[0:1], v1[0:8] + v2[1:2],
                  v1[2:3] + v2[0:8], v1[3:4] + v2[0:8], v1[4:5] + v2[0:8]]
        cand = jnp.concatenate(
            [jnp.where(v, g, NEG_INF) for v, g in zip(groups_valid, groups)], axis=0)
        if exact:
            crank, _ = _topk_ranks(cand, cand_ids, TOPK)
            sel = crank < float(TOPK)
        else:
            ckeys, _ = _topk_keys(_sort_key(cand), TOPK)
            sel, miss = _selected(ckeys, TOPK)
            bad = jnp.maximum(bad, miss)
        z = jnp.sum(jnp.where(sel, jnp.exp(jnp.where(sel, cand - cand[0:1], 0.0)), 0.0),
                    axis=0, keepdims=True)
        self32 = jnp.where(sel, 1.0, 0.0)
        cnt = lambda a, b: jnp.sum(self32[a:b], axis=0, keepdims=True)
        per_r1 = (self32[24:40]
                  + jnp.concatenate([self32[40:48], jnp.zeros((8, LANES), F32)], axis=0)
                  + jnp.where(row16 == 0, cnt(0, 16), 0.0)
                  + jnp.where(row16 == 1, cnt(16, 24), 0.0)
                  + jnp.where(row16 == 2, cnt(48, 56), 0.0)
                  + jnp.where(row16 == 3, cnt(56, 64), 0.0)
                  + jnp.where(row16 == 4, cnt(64, 72), 0.0))
        rank1 = rank_scr[2 * h, :, lanes]
        rank2 = rank_scr[2 * h + 1, :, lanes]
        c1 = jnp.zeros((HEAD_DIM, LANES), F32)
        for j in range(1, TOPK // 3 + 1):
            reach = jnp.sum(jnp.where(per_r1 >= float(j), 1.0, 0.0), axis=0, keepdims=True)
            c1 = c1 + jnp.where(rank1 < reach, 1.0, 0.0)
        c1 = jnp.where(rank1 == 0.0, per_r1[0:1], jnp.where(rank1 == 1.0, per_r1[1:2], c1))
        s1 = s_scr[2 * h, :, lanes]
        s2 = s_scr[2 * h + 1, :, lanes]
        in1 = rank1 < float(TOPK)
        in2 = rank2 < float(TOPK)
        e1 = jnp.where(in1, jnp.exp(jnp.where(in1, s1 - v1[0:1], 0.0)), 0.0)
        e2 = jnp.where(in2, jnp.exp(jnp.where(in2, s2 - v2[0:1], 0.0)), 0.0) / z
        rank2_ref[h, :, lanes] = rank2.astype(BF16)
        e2_ref[h, :, lanes] = (0.5 * e2).astype(BF16)
        c1_ref[h, :, lanes] = _dup_bf16(c1)
        e1_ref[h, :, lanes] = _dup_bf16(e1)
        return bad

    return lax.fori_loop(0, HEADS * nch, head_body, bad, unroll=unroll)


def _router(x, g, wq_t, keys, tn):
    N, D = x.shape
    assert N % tn == 0 and tn % LANES == 0
    const = lambda shape: pl.BlockSpec(shape, lambda i: (0,) * len(shape))
    gate_blk = pl.BlockSpec((HEADS, HEAD_DIM, tn), lambda i: (0, 0, i))
    gate_shape = lambda dt: jax.ShapeDtypeStruct((HEADS, HEAD_DIM, N), dt)
    return pl.pallas_call(
        _router_kernel,
        grid=(N // tn,),
        in_specs=[pl.BlockSpec((tn, D), lambda i: (i, 0)), const((1, D)),
                  const(wq_t.shape), const(keys.shape)],
        out_specs=[pl.BlockSpec((D, tn), lambda i: (0, i)),
                   gate_blk, gate_blk, gate_blk, gate_blk],
        out_shape=[jax.ShapeDtypeStruct((D, N), BF16),
                   gate_shape(BF16), gate_shape(BF16), gate_shape(jnp.int32),
                   gate_shape(jnp.int32)],
        scratch_shapes=[pltpu.VMEM((2 * HEADS, HEAD_DIM, tn), F32),
                        pltpu.VMEM((2 * HEADS, HEAD_DIM, tn), F32),
                        pltpu.VMEM((2 * HEADS, TOPK, tn), F32)],
        compiler_params=pltpu.CompilerParams(
            dimension_semantics=("arbitrary",), vmem_limit_bytes=VMEM_LIMIT),
        name="peer_router",
    )(x, g, wq_t, keys)


ROW_TILE = 16
MXU_TILE = 256


def _peer_dense_kernel(xnt_ref, u_ref, vt_ref, rank2_ref, e2_ref, c1_ref, e1_ref, x_ref, gfin_ref,
                       y_ref, hid0, hid1, g0, g1, acc_scr, rank2_scr, e2_scr, *, final_norm):
    t = pl.program_id(1)
    te, tn = hid0.shape
    nb = te // HEAD_DIM

    @pl.when(t == 0)
    def _():
        for ref in (hid0, hid1, g0, g1, acc_scr):
            ref[...] = jnp.zeros_like(ref)
        rank2_scr[...] = rank2_ref[...]
        e2_scr[...] = e2_ref[...]

    def gate_group(hid_cur, g_cur, i1, il, ch):
        lanes = slice(ch * LANES, (ch + 1) * LANES)
        bcast = lambda ref, h: pltpu.bitcast(
            jnp.broadcast_to(ref[h, i1:i1 + 1, lanes], (8, LANES)), BF16)
        c1 = [bcast(c1_ref, h) for h in range(HEADS)]
        e1 = [bcast(e1_ref, h) for h in range(HEADS)]
        for r0 in range(0, HEAD_DIM, ROW_TILE):
            keys2 = slice(r0, r0 + ROW_TILE)
            rows = slice(il * HEAD_DIM + r0, il * HEAD_DIM + r0 + ROW_TILE)
            w = jnp.zeros((ROW_TILE, LANES), BF16)
            for h in range(HEADS):
                w = w + jnp.where(rank2_scr[h, keys2, lanes] < c1[h],
                                  e2_scr[h, keys2, lanes] * e1[h], jnp.zeros_like(w))
            hid = hid_cur[rows, lanes]
            gelu2 = hid * (1.0 + lax.erf(hid * 0.7071067811865476))
            g_cur[rows, lanes] = w * gelu2.astype(BF16)

    def stage(hid_new, hid_cur, g_cur, g_prev, i1_base):
        width = min(MXU_TILE, tn)

        def hid_piece(m, n):
            rows = slice(m * MXU_TILE, (m + 1) * MXU_TILE)
            cols = slice(n * width, (n + 1) * width)
            hid_new[rows, cols] = _dot(u_ref[rows, :], xnt_ref[:, cols])

        def out_piece(m, n):
            rows = slice(m * MXU_TILE, (m + 1) * MXU_TILE)
            cols = slice(n * width, (n + 1) * width)
            acc_scr[rows, cols] += _dot(vt_ref[rows, :], g_prev[:, cols])

        n_tiles = tn // width
        pieces = [functools.partial(hid_piece, m, n)
                  for m in range(te // MXU_TILE) for n in range(n_tiles)]
        pieces += [functools.partial(out_piece, m, n)
                   for m in range(acc_scr.shape[0] // MXU_TILE) for n in range(n_tiles)]
        groups = [(il, ch) for il in range(nb) for ch in range(tn // LANES)]
        done = 0
        for k, (il, ch) in enumerate(groups):
            upto = -(-(k * len(pieces) + 1) // len(groups))
            for piece in pieces[done:upto]:
                piece()
            done = upto
            gate_group(hid_cur, g_cur, i1_base + il, il, ch)
        for piece in pieces[done:]:
            piece()

    group = c1_ref.shape[1]

    @pl.when(t % 2 == 0)
    def _():
        stage(hid0, hid1, g1, g0, nb % group)

    @pl.when(t % 2 == 1)
    def _():
        stage(hid1, hid0, g0, g1, 0)

    @pl.when(t == pl.num_programs(1) - 1)
    def _():
        y = x_ref[...] + acc_scr[...].T
        if final_norm:
            y = _rms(y, gfin_ref[...])
        y_ref[...] = y


def _peer_dense(xnt, u, vt, rank2, e2, c1, e1, x, gfin, tn, te, final_norm):
    N, D = x.shape
    E = u.shape[0]
    assert N % tn == 0 and E % te == 0 and te % HEAD_DIM == 0
    nblk = E // te
    nb = te // HEAD_DIM
    group = max(8, nb)
    assert group // nb <= 2 and HEAD_DIM % group == 0
    key2_blk = pl.BlockSpec((HEADS, HEAD_DIM, tn), lambda i, t: (0, 0, i))
    key1_blk = pl.BlockSpec(
        (HEADS, group, tn),
        lambda i, t: (0, jnp.clip(((t - 1) * nb) // group, 0, HEAD_DIM // group - 1), i))
    return pl.pallas_call(
        functools.partial(_peer_dense_kernel, final_norm=final_norm),
        grid=(N // tn, nblk + 2),
        in_specs=[pl.BlockSpec((D, tn), lambda i, t: (0, i)),
                  pl.BlockSpec((te, D), lambda i, t: (jnp.minimum(t, nblk - 1), 0)),
                  pl.BlockSpec((D, te), lambda i, t: (0, jnp.clip(t - 2, 0, nblk - 1))),
                  key2_blk, key2_blk, key1_blk, key1_blk,
                  pl.BlockSpec((tn, D), lambda i, t: (i, 0)),
                  pl.BlockSpec((1, D), lambda i, t: (0, 0))],
        out_specs=pl.BlockSpec((tn, D), lambda i, t: (i, 0)),
        out_shape=jax.ShapeDtypeStruct((N, D), F32),
        scratch_shapes=[pltpu.VMEM((te, tn), F32), pltpu.VMEM((te, tn), F32),
                        pltpu.VMEM((te, tn), BF16), pltpu.VMEM((te, tn), BF16),
                        pltpu.VMEM((D, tn), F32),
                        pltpu.VMEM((HEADS, HEAD_DIM, tn), BF16),
                        pltpu.VMEM((HEADS, HEAD_DIM, tn), BF16)],
        compiler_params=pltpu.CompilerParams(
            dimension_semantics=("arbitrary", "arbitrary"), vmem_limit_bytes=VMEM_LIMIT),
        name="peer_dense",
    )(xnt, u, vt, rank2, e2, c1, e1, x, gfin)


def _peer(x, g, wq_t, keys, u, vt, gfin, tn, te, final_norm):
    xnt, rank2, e2, c1, e1 = _router(x, g, wq_t, keys, tn)
    return _peer_dense(xnt, u, vt, rank2, e2, c1, e1, x, gfin, tn, te, final_norm)


HGRN_CHUNK = 64
CONV_CHUNK = 256
CONV_SAMPLE_BLOCK = 32
PEER_TOKENS = 512
PEER_EXPERTS = 512


def kernel(x_prompt, x_sample, state_hgrn, state_conv, norm_mix, norm_ffn, norm_final, hg_w_in, hg_w_out, hg_gnorm, hg_lb_logits, cv_w_pw1, cv_b_pw1, cv_w_dw, cv_b_dw, cv_ln_g, cv_ln_b, cv_w_pw2, cv_b_pw2, peer_w_q, peer_sub_keys, peer_u, peer_v):
    B, T, D = x_prompt.shape
    NS = x_sample.shape[0]
    depth = norm_mix.shape[0]
    row = lambda a: a.reshape(1, -1)

    lb_all = jnp.cumsum(jax.nn.softmax(hg_lb_logits.astype(F32), axis=0), axis=0)

    xp = x_prompt
    xs = x_sample.reshape(NS, D)
    hg_p, hg_s, cv_p, cv_s = [], [], [], []
    for i in range(depth):
        j = i // 2
        gmix = row(norm_mix[i])
        if i % 2 == 0:
            w_in = hg_w_in[j].astype(BF16)
            w_out = hg_w_out[j].astype(BF16)
            gn = row(hg_gnorm[j])
            lb = row(lb_all[i])
            xp, sp = _hgrn_prompt(xp, gmix, w_in, w_out, gn, lb, HGRN_CHUNK)
            xs, ss = _hgrn_sample(xs, gmix, w_in, w_out, gn, lb, state_hgrn[j])
            hg_p.append(sp)
            hg_s.append(ss)
        else:
            cw = (cv_w_pw1[j].astype(BF16), row(cv_b_pw1[j]), cv_w_dw[j], row(cv_b_dw[j]),
                  row(cv_ln_g[j]), row(cv_ln_b[j]), cv_w_pw2[j].astype(BF16), row(cv_b_pw2[j]))
            xp, bp = _conv_prompt(xp, gmix, *cw, CONV_CHUNK)
            xs, bs = _conv_sample(xs, gmix, *cw, state_conv[j], CONV_SAMPLE_BLOCK)
            cv_p.append(bp)
            cv_s.append(bs)
        wq_t = peer_w_q[i].T.astype(BF16)
        keys = peer_sub_keys[i].reshape(2 * HEADS, HEAD_DIM, HEAD_DIM).astype(BF16)
        u = peer_u[i].astype(BF16)
        vt = peer_v[i].T.astype(BF16)
        gffn = row(norm_ffn[i])
        gfin = row(norm_final)
        last = i == depth - 1
        xp = _peer(xp.reshape(B * T, D), gffn, wq_t, keys, u, vt, gfin,
                   min(PEER_TOKENS, B * T), PEER_EXPERTS, last).reshape(B, T, D)
        xs = _peer(xs, gffn, wq_t, keys, u, vt, gfin, min(LANES, NS), PEER_EXPERTS, last)
    return (xp, xs.reshape(NS, 1, D), jnp.stack(hg_p), jnp.stack(hg_s),
            jnp.stack(cv_p), jnp.stack(cv_s))
```

```python
import functools

import jax
import jax.numpy as jnp
from jax import lax
from jax.experimental import pallas as pl
from jax.experimental.pallas import tpu as pltpu

F32 = jnp.float32
BF16 = jnp.bfloat16
EPS = 1e-6

HEADS = 8
HEAD_DIM = 128
TOPK = 16
CONV_WIDTH = 31
LANES = 128
SUB = 16
VMEM_LIMIT = 56 * 1024 * 1024

NEG_INF = float("-inf")


def _rms(x, g):
    return x * lax.rsqrt(jnp.mean(x * x, axis=-1, keepdims=True) + EPS) * g


def _sigmoid(x):
    return 1.0 / (1.0 + jnp.exp(-x))


def _dot(a, b):
    return jnp.dot(a, b, preferred_element_type=F32)


def _dot_nt(a, b):
    return lax.dot_general(a, b, (((1,), (1,)), ((), ())), preferred_element_type=F32)


def _hgrn_prompt_kernel(x_ref, gmix_ref, win_ref, wout_ref, gn_ref, lb_ref, tril_ref, ones_ref,
                        y_ref, s_ref,
                        st_scr, q_scr, k_scr, v_scr, b_scr, p_scr, a_scr, o_scr):
    C = x_ref.shape[1]
    D = x_ref.shape[2]
    nsub = C // SUB
    ci = pl.program_id(1)

    @pl.when(ci == 0)
    def _():
        st_scr[...] = jnp.zeros_like(st_scr)

    xc = x_ref[0]
    hn = _rms(xc, gmix_ref[...])
    proj = _dot(hn.astype(BF16), win_ref[...])
    qz = proj[:, 0:D]
    fz = proj[:, D:2 * D]
    gz = proj[:, 3 * D:4 * D]
    lb = lb_ref[...]
    f = lb + (1.0 - lb) * _sigmoid(fz)
    q_scr[...] = qz * _sigmoid(qz)
    k_scr[...] = 1.0 - f
    v_scr[...] = proj[:, 2 * D:3 * D]
    b_scr[...] = jnp.dot(tril_ref[...], jnp.log(f), precision=lax.Precision.HIGHEST,
                         preferred_element_type=F32)

    qe = (q_scr[...] * jnp.exp(b_scr[...])).astype(BF16)
    for h in range(HEADS):
        L = slice(h * HEAD_DIM, (h + 1) * HEAD_DIM)
        o_scr[:, L] = _dot_nt(qe[:, L], st_scr[h].astype(BF16))

    t_iota = lax.broadcasted_iota(jnp.int32, (SUB, D), 0)
    for I in range(nsub):
        rows = slice(SUB * I, SUB * (I + 1))
        bI = b_scr[rows, :]
        qI = q_scr[rows, :]
        if I > 0:
            beta = b_scr[SUB * I - 1:SUB * I, :]
            qs = (qI * jnp.exp(bI - beta)).astype(BF16)
            ks = (k_scr[0:SUB * I, :] * jnp.exp(beta - b_scr[0:SUB * I, :])).astype(BF16)
            vs = v_scr[0:SUB * I, :].astype(BF16)
            for h in range(HEADS):
                L = slice(h * HEAD_DIM, (h + 1) * HEAD_DIM)
                att = _dot_nt(qs[:, L], ks[:, L])
                o_scr[rows, L] += _dot(att.astype(BF16), vs[:, L])
        for s in range(SUB):
            r = SUB * I + s
            keep = t_iota >= s
            dec = jnp.exp(jnp.where(keep, bI - b_scr[r:r + 1, :], 0.0))
            p = jnp.where(keep, qI * dec * k_scr[r:r + 1, :], 0.0)
            p_scr[r * SUB:(r + 1) * SUB, :] = p.astype(BF16)

    for h in range(HEADS):
        L = slice(h * HEAD_DIM, (h + 1) * HEAD_DIM)
        a_scr[:, L] = _dot(p_scr[:, L], ones_ref[...])
    for I in range(nsub):
        rows = slice(SUB * I, SUB * (I + 1))
        acc = o_scr[rows, :]
        for s in range(SUB):
            r = SUB * I + s
            acc = acc + a_scr[r * SUB:(r + 1) * SUB, :] * v_scr[r:r + 1, :]
        o_scr[rows, :] = acc

    b_last = b_scr[C - 1:C, :]
    kd = (k_scr[...] * jnp.exp(b_last - b_scr[...])).astype(BF16)
    eb_last = jnp.exp(b_last)
    for h in range(HEADS):
        L = slice(h * HEAD_DIM, (h + 1) * HEAD_DIM)
        vt = v_scr[:, L].T.astype(BF16)
        st_scr[h] = st_scr[h] * eb_last[:, L] + _dot(vt, kd[:, L])

    gn = gn_ref[...]
    for h in range(HEADS):
        L = slice(h * HEAD_DIM, (h + 1) * HEAD_DIM)
        oh = o_scr[:, L]
        gzh = gz[:, L]
        o_scr[:, L] = _rms(oh, gn) * (gzh * _sigmoid(gzh))
    y_ref[0] = xc + _dot(o_scr[...].astype(BF16), wout_ref[...])

    @pl.when(ci == pl.num_programs(1) - 1)
    def _():
        for h in range(HEADS):
            s_ref[0, h] = st_scr[h].T


def _hgrn_prompt(x, gmix, w_in, w_out, gnorm, lb, chunk):
    B, T, D = x.shape
    C = min(chunk, T)
    assert T % C == 0 and C % SUB == 0
    tril = jnp.tril(jnp.ones((C, C), F32))
    ones = jnp.ones((HEAD_DIM, HEAD_DIM), BF16)
    const = lambda shape: pl.BlockSpec(shape, lambda b, c: (0,) * len(shape))
    return pl.pallas_call(
        _hgrn_prompt_kernel,
        grid=(B, T // C),
        in_specs=[
            pl.BlockSpec((1, C, D), lambda b, c: (b, c, 0)),
            const((1, D)), const((D, 4 * D)), const((D, D)), const((1, HEAD_DIM)), const((1, D)),
            const((C, C)), const((HEAD_DIM, HEAD_DIM)),
        ],
        out_specs=[
            pl.BlockSpec((1, C, D), lambda b, c: (b, c, 0)),
            pl.BlockSpec((1, HEADS, HEAD_DIM, HEAD_DIM), lambda b, c: (b, 0, 0, 0)),
        ],
        out_shape=[
            jax.ShapeDtypeStruct((B, T, D), F32),
            jax.ShapeDtypeStruct((B, HEADS, HEAD_DIM, HEAD_DIM), F32),
        ],
        scratch_shapes=[
            pltpu.VMEM((HEADS, HEAD_DIM, HEAD_DIM), F32),
            pltpu.VMEM((C, D), F32),
            pltpu.VMEM((C, D), F32),
            pltpu.VMEM((C, D), F32),
            pltpu.VMEM((C, D), F32),
            pltpu.VMEM((C * SUB, D), BF16),
            pltpu.VMEM((C * SUB, D), F32),
            pltpu.VMEM((C, D), F32),
        ],
        compiler_params=pltpu.CompilerParams(
            dimension_semantics=("arbitrary", "arbitrary"), vmem_limit_bytes=VMEM_LIMIT),
        name="hgrn_prompt",
    )(x, gmix, w_in, w_out, gnorm, lb, tril, ones)


def _hgrn_sample_kernel(x_ref, gmix_ref, win_ref, wout_ref, gn_ref, lb_ref, s_ref,
                        y_ref, snew_ref, proj_scr, o_scr):
    n = pl.program_id(0)
    D = x_ref.shape[1]

    @pl.when(n == 0)
    def _():
        hn = _rms(x_ref[...], gmix_ref[...])
        proj_scr[...] = _dot(hn.astype(BF16), win_ref[...])

    row = proj_scr[pl.ds(n, 1), :]
    qz = row[:, 0:D]
    fz = row[:, D:2 * D]
    v = row[:, 2 * D:3 * D]
    lb = lb_ref[...]
    f = lb + (1.0 - lb) * _sigmoid(fz)
    q = qz * _sigmoid(qz)
    k = 1.0 - f

    def cols(r):
        stacked = jnp.concatenate(
            [r[:, h * HEAD_DIM:(h + 1) * HEAD_DIM] for h in range(HEADS)]
            + [jnp.zeros((HEAD_DIM - HEADS, HEAD_DIM), F32)], axis=0)
        return stacked.T

    fc = cols(f)
    kc = cols(k)
    qc = cols(q)
    outs = []
    for h in range(HEADS):
        L = slice(h * HEAD_DIM, (h + 1) * HEAD_DIM)
        s_new = s_ref[0, h] * fc[:, h:h + 1] + kc[:, h:h + 1] * v[:, L]
        snew_ref[0, h] = s_new
        outs.append(jnp.sum(qc[:, h:h + 1] * s_new, axis=0, keepdims=True))
    o_scr[pl.ds(n, 1), :] = jnp.concatenate(outs, axis=1)

    @pl.when(n == pl.num_programs(0) - 1)
    def _():
        gn = gn_ref[...]
        gz = proj_scr[:, 3 * D:4 * D]
        for h in range(HEADS):
            L = slice(h * HEAD_DIM, (h + 1) * HEAD_DIM)
            gzh = gz[:, L]
            o_scr[:, L] = _rms(o_scr[:, L], gn) * (gzh * _sigmoid(gzh))
        y_ref[...] = x_ref[...] + _dot(o_scr[...].astype(BF16), wout_ref[...])


def _hgrn_sample(x, gmix, w_in, w_out, gnorm, lb, state):
    N, D = x.shape
    const = lambda shape: pl.BlockSpec(shape, lambda n: (0,) * len(shape))
    sblk = pl.BlockSpec((1, HEADS, HEAD_DIM, HEAD_DIM), lambda n: (n, 0, 0, 0))
    return pl.pallas_call(
        _hgrn_sample_kernel,
        grid=(N,),
        in_specs=[const((N, D)), const((1, D)), const((D, 4 * D)), const((D, D)),
                  const((1, HEAD_DIM)), const((1, D)), sblk],
        out_specs=[const((N, D)), sblk],
        out_shape=[jax.ShapeDtypeStruct((N, D), F32),
                   jax.ShapeDtypeStruct((N, HEADS, HEAD_DIM, HEAD_DIM), F32)],
        scratch_shapes=[pltpu.VMEM((N, 4 * D), F32), pltpu.VMEM((N, D), F32)],
        compiler_params=pltpu.CompilerParams(
            dimension_semantics=("arbitrary",), vmem_limit_bytes=VMEM_LIMIT),
        name="hgrn_sample",
    )(x, gmix, w_in, w_out, gnorm, lb, state)


CONV_PAD = 32


def _conv_tail(y, xc, bdw_ref, lng_ref, lnb_ref, pw2_ref, bpw2_ref):
    y = y + bdw_ref[...]
    mu = jnp.mean(y, axis=-1, keepdims=True)
    yc = y - mu
    var = jnp.mean(yc * yc, axis=-1, keepdims=True)
    yn = yc * lax.rsqrt(var + EPS) * lng_ref[...] + lnb_ref[...]
    z = yn * _sigmoid(yn)
    return xc + _dot(z.astype(BF16), pw2_ref[...]) + bpw2_ref[...]


def _glu(xc, gmix_ref, pw1_ref, bpw1_ref):
    D = xc.shape[1]
    hn = _rms(xc, gmix_ref[...])
    h2 = _dot(hn.astype(BF16), pw1_ref[...]) + bpw1_ref[...]
    return h2[:, 0:D] * _sigmoid(h2[:, D:2 * D])


def _conv_prompt_kernel(x_ref, gmix_ref, pw1_ref, bpw1_ref, wdw_ref, bdw_ref, lng_ref, lnb_ref,
                        pw2_ref, bpw2_ref, y_ref, buf_ref, full_scr):
    C = x_ref.shape[1]
    ci = pl.program_id(1)
    hist = CONV_WIDTH - 1
    off = CONV_PAD - hist

    @pl.when(ci == 0)
    def _():
        full_scr[0:CONV_PAD, :] = jnp.zeros((CONV_PAD, full_scr.shape[1]), F32)

    xc = x_ref[0]
    full_scr[CONV_PAD:CONV_PAD + C, :] = _glu(xc, gmix_ref, pw1_ref, bpw1_ref)
    y = full_scr[off:off + C, :] * wdw_ref[0:1, :]
    for j in range(1, CONV_WIDTH):
        y = y + full_scr[off + j:off + j + C, :] * wdw_ref[j:j + 1, :]
    y_ref[0] = _conv_tail(y, xc, bdw_ref, lng_ref, lnb_ref, pw2_ref, bpw2_ref)

    @pl.when(ci == pl.num_programs(1) - 1)
    def _():
        buf_ref[0] = full_scr[C + off:C + CONV_PAD, :]

    full_scr[0:CONV_PAD, :] = full_scr[C:C + CONV_PAD, :]


def _conv_prompt(x, gmix, pw1, bpw1, wdw, bdw, lng, lnb, pw2, bpw2, chunk):
    B, T, D = x.shape
    C = min(chunk, T)
    assert T % C == 0 and C >= CONV_PAD
    hist = CONV_WIDTH - 1
    const = lambda shape: pl.BlockSpec(shape, lambda b, c: (0,) * len(shape))
    return pl.pallas_call(
        _conv_prompt_kernel,
        grid=(B, T // C),
        in_specs=[pl.BlockSpec((1, C, D), lambda b, c: (b, c, 0)),
                  const((1, D)), const((D, 2 * D)), const((1, 2 * D)), const((CONV_WIDTH, D)),
                  const((1, D)), const((1, D)), const((1, D)), const((D, D)), const((1, D))],
        out_specs=[pl.BlockSpec((1, C, D), lambda b, c: (b, c, 0)),
                   pl.BlockSpec((1, hist, D), lambda b, c: (b, 0, 0))],
        out_shape=[jax.ShapeDtypeStruct((B, T, D), F32),
                   jax.ShapeDtypeStruct((B, hist, D), F32)],
        scratch_shapes=[pltpu.VMEM((CONV_PAD + C, D), F32)],
        compiler_params=pltpu.CompilerParams(
            dimension_semantics=("arbitrary", "arbitrary"), vmem_limit_bytes=VMEM_LIMIT),
        name="conv_prompt",
    )(x, gmix, pw1, bpw1, wdw, bdw, lng, lnb, pw2, bpw2)


def _conv_sample_kernel(x_ref, gmix_ref, pw1_ref, bpw1_ref, wdw_ref, bdw_ref, lng_ref, lnb_ref,
                        pw2_ref, bpw2_ref, buf_ref, y_ref, nbuf_ref, u_scr, y_scr):
    nt = x_ref.shape[0]
    hist = CONV_WIDTH - 1
    xc = x_ref[...]
    u_scr[...] = _glu(xc, gmix_ref, pw1_ref, bpw1_ref)
    w_hist = wdw_ref[0:hist, :]
    w_last = wdw_ref[hist:hist + 1, :]

    def body(n, carry):
        buf = buf_ref[n]
        u = u_scr[pl.ds(n, 1), :]
        y_scr[pl.ds(n, 1), :] = jnp.sum(buf * w_hist, axis=0, keepdims=True) + u * w_last
        nbuf_ref[n, 0:hist - 1, :] = buf[1:hist, :]
        nbuf_ref[n, hist - 1:hist, :] = u
        return carry

    lax.fori_loop(0, nt, body, 0)
    y_ref[...] = _conv_tail(y_scr[...], xc, bdw_ref, lng_ref, lnb_ref, pw2_ref, bpw2_ref)


def _conv_sample(x, gmix, pw1, bpw1, wdw, bdw, lng, lnb, pw2, bpw2, buf, block):
    N, D = x.shape
    nt = min(block, N)
    assert N % nt == 0
    hist = CONV_WIDTH - 1
    const = lambda shape: pl.BlockSpec(shape, lambda i: (0,) * len(shape))
    return pl.pallas_call(
        _conv_sample_kernel,
        grid=(N // nt,),
        in_specs=[pl.BlockSpec((nt, D), lambda i: (i, 0)),
                  const((1, D)), const((D, 2 * D)), const((1, 2 * D)), const((CONV_WIDTH, D)),
                  const((1, D)), const((1, D)), const((1, D)), const((D, D)), const((1, D)),
                  pl.BlockSpec((nt, hist, D), lambda i: (i, 0, 0))],
        out_specs=[pl.BlockSpec((nt, D), lambda i: (i, 0)),
                   pl.BlockSpec((nt, hist, D), lambda i: (i, 0, 0))],
        out_shape=[jax.ShapeDtypeStruct((N, D), F32),
                   jax.ShapeDtypeStruct((N, hist, D), F32)],
        scratch_shapes=[pltpu.VMEM((nt, D), F32), pltpu.VMEM((nt, D), F32)],
        compiler_params=pltpu.CompilerParams(
            dimension_semantics=("arbitrary",), vmem_limit_bytes=VMEM_LIMIT),
        name="conv_sample",
    )(x, gmix, pw1, bpw1, wdw, bdw, lng, lnb, pw2, bpw2, buf)


def _topk_ranks(vals, ids, k):
    rank = jnp.full(vals.shape, float(k), F32)
    tops = []
    for r in range(k):
        m = jnp.max(vals, axis=0, keepdims=True)
        first = jnp.min(jnp.where(vals == m, ids, 1e9), axis=0, keepdims=True)
        sel = ids == first
        vals = jnp.where(sel, NEG_INF, vals)
        rank = jnp.where(sel, float(r), rank)
        tops.append(m)
    return rank, tops


INT_MIN = -2 ** 31


def _sort_key(s):
    b = lax.bitcast_convert_type(jnp.where(s == 0.0, 0.0, s), jnp.int32)
    return b ^ (lax.shift_right_arithmetic(b, 31) & jnp.int32(0x7FFFFFFF))


def _key_value(k):
    b = k ^ (lax.shift_right_arithmetic(k, 31) & jnp.int32(0x7FFFFFFF))
    return lax.bitcast_convert_type(b, F32)


def _topk_keys(keys, k):
    tops = []
    for r in range(k):
        m = jnp.max(keys, axis=0, keepdims=True)
        keys = jnp.where(keys == m, jnp.int32(INT_MIN + r), keys)
        tops.append(m)
    return keys, tops


def _selected(keys, k):
    sel = keys < jnp.int32(INT_MIN + k)
    n = jnp.sum(jnp.where(sel, 1.0, 0.0), axis=0, keepdims=True)
    return sel, jnp.where(n != float(k), 1.0, 0.0)


def _dup_bf16(x):
    u = lax.bitcast_convert_type(x.astype(BF16).astype(F32), jnp.int32)
    return u | lax.shift_right_logical(u, 16)


def _router_kernel(x_ref, g_ref, wq_ref, keys_ref,
                   xnt_ref, rank2_ref, e2_ref, c1_ref, e1_ref,
                   s_scr, rank_scr, top_scr, miss_scr):
    tn = x_ref.shape[0]
    xn = _rms(x_ref[...], g_ref[...])
    xnt = xn.T.astype(BF16)
    xnt_ref[...] = xnt
    qt = _dot(wq_ref[...], xnt).astype(BF16)
    for hp in range(2 * HEADS):
        s_scr[hp] = _dot(keys_ref[hp], qt[hp * HEAD_DIM:(hp + 1) * HEAD_DIM, :])

    outs = (rank2_ref, e2_ref, c1_ref, e1_ref)
    scr = (s_scr, rank_scr, top_scr)
    _select_experts(tn, scr, outs, miss_scr)


def _run_with_tie_fallback(n, per_group, problem, miss_scr):
    def fast(g, carry):
        miss = [problem(g * per_group + c, False) for c in range(per_group)]
        miss_scr[pl.ds(g, 1), :] = functools.reduce(jnp.maximum, miss)
        return carry

    lax.fori_loop(0, n // per_group, fast, 0, unroll=2 if per_group == 1 else 1)

    def fix(g, carry):
        @pl.when(jnp.max(miss_scr[pl.ds(g, 1), :]) > 0.0)
        def _():
            for c in range(per_group):
                problem(g * per_group + c, True)
        return carry

    lax.fori_loop(0, n // per_group, fix, 0)


def _select_experts(tn, scr, outs, miss_scr):
    s_scr, rank_scr, top_scr = scr
    rank2_ref, e2_ref, c1_ref, e1_ref = outs
    nch = tn // LANES
    key_ids = lax.broadcasted_iota(jnp.int32, (HEAD_DIM, LANES), 0).astype(F32)

    def half_problem(it, exact):
        hp = it // nch
        lanes = pl.ds(pl.multiple_of((it % nch) * LANES, LANES), LANES)
        s = s_scr[hp, :, lanes]
        miss = None
        if exact:
            rank, tops = _topk_ranks(s, key_ids, TOPK)
            top = jnp.concatenate(tops, axis=0)
        else:
            keys, tops = _topk_keys(_sort_key(s), TOPK)
            sel, miss = _selected(keys, TOPK)
            rank = jnp.where(sel, (keys ^ jnp.int32(INT_MIN)).astype(F32), float(TOPK))
            top = _key_value(jnp.concatenate(tops, axis=0))
        rank_scr[hp, :, lanes] = rank
        top_scr[hp, :, lanes] = top
        return miss

    _run_with_tie_fallback(2 * HEADS * nch, nch, half_problem, miss_scr)

    row16 = lax.broadcasted_iota(jnp.int32, (16, LANES), 0)
    row8 = lax.broadcasted_iota(jnp.int32, (8, LANES), 0)
    r16 = row16.astype(F32)
    r8 = row8.astype(F32)
    groups_valid = [row16 >= 0, row8 >= 0, row16 >= 2, row8 >= 2,
                    (row8 >= 2) & (row8 <= 4), (row8 >= 2) & (row8 <= 3), row8 == 2]
    groups_ids = [r16, 16.0 + r8, r16 * 16.0, r8 * 16.0 + 1.0, 32.0 + r8, 48.0 + r8, 64.0 + r8]
    cand_ids = jnp.concatenate(
        [jnp.where(v, i, 1e9) for v, i in zip(groups_valid, groups_ids)], axis=0)

    def head_problem(it, exact):
        h = it // nch
        lanes = pl.ds(pl.multiple_of((it % nch) * LANES, LANES), LANES)
        v1 = top_scr[2 * h, :, lanes]
        v2 = top_scr[2 * h + 1, :, lanes]
        groups = [v1[0:1] + v2, v1[1:2] + v2[0:8], v1 + v2[0:1], v1[0:8] + v2[1:2],
                  v1[2:3] + v2[0:8], v1[3:4] + v2[0:8], v1[4:5] + v2[0:8]]
        cand = jnp.concatenate(
            [jnp.where(v, g, NEG_INF) for v, g in zip(groups_valid, groups)], axis=0)
        miss = None
        if exact:
            crank, _ = _topk_ranks(cand, cand_ids, TOPK)
            sel = crank < float(TOPK)
        else:
            ckeys, _ = _topk_keys(_sort_key(cand), TOPK)
            sel, miss = _selected(ckeys, TOPK)
        z = jnp.sum(jnp.where(sel, jnp.exp(jnp.where(sel, cand - cand[0:1], 0.0)), 0.0),
                    axis=0, keepdims=True)
        self32 = jnp.where(sel, 1.0, 0.0)
        cnt = lambda a, b: jnp.sum(self32[a:b], axis=0, keepdims=True)
        per_r1 = (self32[24:40]
                  + jnp.concatenate([self32[40:48], jnp.zeros((8, LANES), F32)], axis=0)
                  + jnp.where(row16 == 0, cnt(0, 16), 0.0)
                  + jnp.where(row16 == 1, cnt(16, 24), 0.0)
                  + jnp.where(row16 == 2, cnt(48, 56), 0.0)
                  + jnp.where(row16 == 3, cnt(56, 64), 0.0)
                  + jnp.where(row16 == 4, cnt(64, 72), 0.0))
        rank1 = rank_scr[2 * h, :, lanes]
        rank2 = rank_scr[2 * h + 1, :, lanes]
        c1 = jnp.zeros((HEAD_DIM, LANES), F32)
        for j in range(1, TOPK // 3 + 1):
            reach = jnp.sum(jnp.where(per_r1 >= float(j), 1.0, 0.0), axis=0, keepdims=True)
            c1 = c1 + jnp.where(rank1 < reach, 1.0, 0.0)
        c1 = jnp.where(rank1 == 0.0, per_r1[0:1], jnp.where(rank1 == 1.0, per_r1[1:2], c1))
        s1 = s_scr[2 * h, :, lanes]
        s2 = s_scr[2 * h + 1, :, lanes]
        in1 = rank1 < float(TOPK)
        in2 = rank2 < float(TOPK)
        e1 = jnp.where(in1, jnp.exp(jnp.where(in1, s1 - v1[0:1], 0.0)), 0.0)
        e2 = jnp.where(in2, jnp.exp(jnp.where(in2, s2 - v2[0:1], 0.0)), 0.0) / z
        rank2_ref[h, :, lanes] = rank2.astype(BF16)
        e2_ref[h, :, lanes] = (0.5 * e2).astype(BF16)
        c1_ref[h, :, lanes] = _dup_bf16(c1)
        e1_ref[h, :, lanes] = _dup_bf16(e1)
        return miss

    _run_with_tie_fallback(HEADS * nch, nch, head_problem, miss_scr)


def _router(x, g, wq_t, keys, tn):
    N, D = x.shape
    assert N % tn == 0 and tn % LANES == 0
    const = lambda shape: pl.BlockSpec(shape, lambda i: (0,) * len(shape))
    gate_blk = pl.BlockSpec((HEADS, HEAD_DIM, tn), lambda i: (0, 0, i))
    gate_shape = lambda dt: jax.ShapeDtypeStruct((HEADS, HEAD_DIM, N), dt)
    return pl.pallas_call(
        _router_kernel,
        grid=(N // tn,),
        in_specs=[pl.BlockSpec((tn, D), lambda i: (i, 0)), const((1, D)),
                  const(wq_t.shape), const(keys.shape)],
        out_specs=[pl.BlockSpec((D, tn), lambda i: (0, i)),
                   gate_blk, gate_blk, gate_blk, gate_blk],
        out_shape=[jax.ShapeDtypeStruct((D, N), BF16),
                   gate_shape(BF16), gate_shape(BF16), gate_shape(jnp.int32),
                   gate_shape(jnp.int32)],
        scratch_shapes=[pltpu.VMEM((2 * HEADS, HEAD_DIM, tn), F32),
                        pltpu.VMEM((2 * HEADS, HEAD_DIM, tn), F32),
                        pltpu.VMEM((2 * HEADS, TOPK, tn), F32),
                        pltpu.VMEM((2 * HEADS * (tn // LANES), LANES), F32)],
        compiler_params=pltpu.CompilerParams(
            dimension_semantics=("arbitrary",), vmem_limit_bytes=VMEM_LIMIT),
        name="peer_router",
    )(x, g, wq_t, keys)


ROW_TILE = 16
MXU_TILE = 256


def _peer_dense_kernel(xnt_ref, u_ref, vt_ref, rank2_ref, e2_ref, c1_ref, e1_ref, x_ref, gfin_ref,
                       y_ref, hid0, hid1, g_scr, acc_scr, rank2_scr, e2_scr, *, final_norm):
    t = pl.program_id(1)
    te, tn = hid0.shape
    nb = te // HEAD_DIM

    @pl.when(t == 0)
    def _():
        for ref in (hid0, hid1, acc_scr):
            ref[...] = jnp.zeros_like(ref)
        rank2_scr[...] = rank2_ref[...]
        e2_scr[...] = e2_ref[...]

    def gate_group(hid_cur, i1, il, ch):
        lanes = slice(ch * LANES, (ch + 1) * LANES)
        bcast = lambda ref, h: pltpu.bitcast(
            jnp.broadcast_to(ref[h, i1:i1 + 1, lanes], (8, LANES)), BF16)
        c1 = [bcast(c1_ref, h) for h in range(HEADS)]
        e1 = [bcast(e1_ref, h) for h in range(HEADS)]
        for r0 in range(0, HEAD_DIM, ROW_TILE):
            keys2 = slice(r0, r0 + ROW_TILE)
            rows = slice(il * HEAD_DIM + r0, il * HEAD_DIM + r0 + ROW_TILE)
            w = jnp.zeros((ROW_TILE, LANES), BF16)
            for h in range(HEADS):
                w = w + jnp.where(rank2_scr[h, keys2, lanes] < c1[h],
                                  e2_scr[h, keys2, lanes] * e1[h], jnp.zeros_like(w))
            hid = hid_cur[rows, lanes]
            gelu2 = hid * (1.0 + lax.erf(hid * 0.7071067811865476))
            g_scr[rows, lanes] = w * gelu2.astype(BF16)

    def stage(hid_new, hid_cur, i1_base):
        width = min(MXU_TILE, tn)
        depth = min(MXU_TILE, te)
        for k0 in range(0, te, depth):
            for n0 in range(0, tn, width):
                hid_new[k0:k0 + depth, n0:n0 + width] = _dot(u_ref[k0:k0 + depth, :],
                                                             xnt_ref[:, n0:n0 + width])
                for il in range(k0 // HEAD_DIM, (k0 + depth) // HEAD_DIM):
                    for ch in range(n0 // LANES, (n0 + width) // LANES):
                        gate_group(hid_cur, i1_base + il, il, ch)
                acc_scr[:, n0:n0 + width] += _dot(vt_ref[0, :, k0:k0 + depth],
                                                  g_scr[k0:k0 + depth, n0:n0 + width])

    group = c1_ref.shape[1]

    @pl.when(t % 2 == 0)
    def _():
        stage(hid0, hid1, nb % group)

    @pl.when(t % 2 == 1)
    def _():
        stage(hid1, hid0, 0)

    @pl.when(t == pl.num_programs(1) - 1)
    def _():
        y = x_ref[...] + acc_scr[...].T
        if final_norm:
            y = _rms(y, gfin_ref[...])
        y_ref[...] = y


def _peer_dense(xnt, u, vt, rank2, e2, c1, e1, x, gfin, tn, final_norm):
    N, D = x.shape
    nblk, _, te = vt.shape
    assert N % tn == 0 and u.shape[0] == nblk * te and te % HEAD_DIM == 0
    nb = te // HEAD_DIM
    group = max(8, nb)
    assert group // nb <= 2 and HEAD_DIM % group == 0
    key2_blk = pl.BlockSpec((HEADS, HEAD_DIM, tn), lambda i, t: (0, 0, i))
    key1_blk = pl.BlockSpec(
        (HEADS, group, tn),
        lambda i, t: (0, jnp.clip(((t - 1) * nb) // group, 0, HEAD_DIM // group - 1), i))
    return pl.pallas_call(
        functools.partial(_peer_dense_kernel, final_norm=final_norm),
        grid=(N // tn, nblk + 1),
        in_specs=[pl.BlockSpec((D, tn), lambda i, t: (0, i)),
                  pl.BlockSpec((te, D), lambda i, t: (jnp.minimum(t, nblk - 1), 0)),
                  pl.BlockSpec((1, D, te), lambda i, t: (jnp.maximum(t - 1, 0), 0, 0)),
                  key2_blk, key2_blk, key1_blk, key1_blk,
                  pl.BlockSpec((tn, D), lambda i, t: (i, 0)),
                  pl.BlockSpec((1, D), lambda i, t: (0, 0))],
        out_specs=pl.BlockSpec((tn, D), lambda i, t: (i, 0)),
        out_shape=jax.ShapeDtypeStruct((N, D), F32),
        scratch_shapes=[pltpu.VMEM((te, tn), F32), pltpu.VMEM((te, tn), F32),
                        pltpu.VMEM((te, tn), BF16),
                        pltpu.VMEM((D, tn), F32),
                        pltpu.VMEM((HEADS, HEAD_DIM, tn), BF16),
                        pltpu.VMEM((HEADS, HEAD_DIM, tn), BF16)],
        compiler_params=pltpu.CompilerParams(
            dimension_semantics=("arbitrary", "arbitrary"), vmem_limit_bytes=VMEM_LIMIT),
        name="peer_dense",
    )(xnt, u, vt, rank2, e2, c1, e1, x, gfin)


def _peer(x, g, wq_t, keys, u, vt, gfin, tn, final_norm):
    xnt, rank2, e2, c1, e1 = _router(x, g, wq_t, keys, tn)
    return _peer_dense(xnt, u, vt, rank2, e2, c1, e1, x, gfin, tn, final_norm)


HGRN_CHUNK = 64
CONV_CHUNK = 256
CONV_SAMPLE_BLOCK = 32
PEER_TOKENS = 512
PEER_EXPERTS = 512


def kernel(x_prompt, x_sample, state_hgrn, state_conv, norm_mix, norm_ffn, norm_final, hg_w_in, hg_w_out, hg_gnorm, hg_lb_logits, cv_w_pw1, cv_b_pw1, cv_w_dw, cv_b_dw, cv_ln_g, cv_ln_b, cv_w_pw2, cv_b_pw2, peer_w_q, peer_sub_keys, peer_u, peer_v):
    B, T, D = x_prompt.shape
    NS = x_sample.shape[0]
    depth = norm_mix.shape[0]
    row = lambda a: a.reshape(1, -1)

    lb_all = jnp.cumsum(jax.nn.softmax(hg_lb_logits.astype(F32), axis=0), axis=0)

    xp = x_prompt
    xs = x_sample.reshape(NS, D)
    hg_p, hg_s, cv_p, cv_s = [], [], [], []
    for i in range(depth):
        j = i // 2
        gmix = row(norm_mix[i])
        if i % 2 == 0:
            w_in = hg_w_in[j].astype(BF16)
            w_out = hg_w_out[j].astype(BF16)
            gn = row(hg_gnorm[j])
            lb = row(lb_all[i])
            xp, sp = _hgrn_prompt(xp, gmix, w_in, w_out, gn, lb, HGRN_CHUNK)
            xs, ss = _hgrn_sample(xs, gmix, w_in, w_out, gn, lb, state_hgrn[j])
            hg_p.append(sp)
            hg_s.append(ss)
        else:
            cw = (cv_w_pw1[j].astype(BF16), row(cv_b_pw1[j]), cv_w_dw[j], row(cv_b_dw[j]),
                  row(cv_ln_g[j]), row(cv_ln_b[j]), cv_w_pw2[j].astype(BF16), row(cv_b_pw2[j]))
            xp, bp = _conv_prompt(xp, gmix, *cw, CONV_CHUNK)
            xs, bs = _conv_sample(xs, gmix, *cw, state_conv[j], CONV_SAMPLE_BLOCK)
            cv_p.append(bp)
            cv_s.append(bs)
        wq_t = peer_w_q[i].T.astype(BF16)
        keys = peer_sub_keys[i].reshape(2 * HEADS, HEAD_DIM, HEAD_DIM).astype(BF16)
        u = peer_u[i].astype(BF16)
        vt = peer_v[i].reshape(-1, PEER_EXPERTS, D).transpose(0, 2, 1).astype(BF16)
        gffn = row(norm_ffn[i])
        gfin = row(norm_final)
        last = i == depth - 1
        xp = _peer(xp.reshape(B * T, D), gffn, wq_t, keys, u, vt, gfin,
                   min(PEER_TOKENS, B * T), last).reshape(B, T, D)
        xs = _peer(xs, gffn, wq_t, keys, u, vt, gfin, min(LANES, NS), last)
    return (xp, xs.reshape(NS, 1, D), jnp.stack(hg_p), jnp.stack(hg_s),
            jnp.stack(cv_p), jnp.stack(cv_s))
```

```python
import functools

import jax
import jax.numpy as jnp
from jax import lax
from jax.experimental import pallas as pl
from jax.experimental.pallas import tpu as pltpu

F32 = jnp.float32
BF16 = jnp.bfloat16
EPS = 1e-6

HEADS = 8
HEAD_DIM = 128
TOPK = 16
CONV_WIDTH = 31
LANES = 128
SUB = 16
VMEM_LIMIT = 56 * 1024 * 1024

NEG_INF = float("-inf")


def _rms(x, g):
    return x * lax.rsqrt(jnp.mean(x * x, axis=-1, keepdims=True) + EPS) * g


def _sigmoid(x):
    return 1.0 / (1.0 + jnp.exp(-x))


def _dot(a, b):
    return jnp.dot(a, b, preferred_element_type=F32)


def _dot_nt(a, b):
    return lax.dot_general(a, b, (((1,), (1,)), ((), ())), preferred_element_type=F32)


def _hgrn_prompt_kernel(x_ref, gmix_ref, win_ref, wout_ref, gn_ref, lb_ref, tril_ref, ones_ref,
                        y_ref, s_ref,
                        st_scr, q_scr, k_scr, v_scr, b_scr, p_scr, a_scr, o_scr):
    R, C, D = x_ref.shape
    M = R * C
    nsub = C // SUB
    ci = pl.program_id(1)

    @pl.when(ci == 0)
    def _():
        st_scr[...] = jnp.zeros_like(st_scr)

    xc = x_ref[...].reshape(M, D)
    hn = _rms(xc, gmix_ref[...])
    proj = _dot(hn.astype(BF16), win_ref[...])
    qz = proj[:, 0:D]
    fz = proj[:, D:2 * D]
    gz = proj[:, 3 * D:4 * D]
    lb = lb_ref[...]
    f = lb + (1.0 - lb) * _sigmoid(fz)
    q_scr[...] = qz * _sigmoid(qz)
    k_scr[...] = 1.0 - f
    v_scr[...] = proj[:, 2 * D:3 * D]
    lf = jnp.log(f)
    hi = lf.astype(BF16)
    rem = lf - hi.astype(F32)
    mid = rem.astype(BF16)
    lo = (rem - mid.astype(F32)).astype(BF16)
    tril = tril_ref[...]
    b_scr[...] = _dot(tril, hi) + _dot(tril, mid) + _dot(tril, lo)

    qe = (q_scr[...] * jnp.exp(b_scr[...])).astype(BF16)
    for r in range(R):
        for h in range(HEADS):
            L = slice(h * HEAD_DIM, (h + 1) * HEAD_DIM)
            o_scr[r * C:(r + 1) * C, L] = _dot_nt(qe[r * C:(r + 1) * C, L],
                                                  st_scr[r * HEADS + h].astype(BF16))

    t_iota = lax.broadcasted_iota(jnp.int32, (SUB, D), 0)
    for r in range(R):
        base = r * C
        for I in range(nsub):
            rows = slice(base + SUB * I, base + SUB * (I + 1))
            prev = slice(base, base + SUB * I)
            bI = b_scr[rows, :]
            qI = q_scr[rows, :]
            if I > 0:
                beta = b_scr[base + SUB * I - 1:base + SUB * I, :]
                qs = (qI * jnp.exp(bI - beta)).astype(BF16)
                ks = (k_scr[prev, :] * jnp.exp(beta - b_scr[prev, :])).astype(BF16)
                vs = v_scr[prev, :].astype(BF16)
                for h in range(HEADS):
                    L = slice(h * HEAD_DIM, (h + 1) * HEAD_DIM)
                    att = _dot_nt(qs[:, L], ks[:, L])
                    o_scr[rows, L] += _dot(att.astype(BF16), vs[:, L])
            for s in range(SUB):
                src = base + SUB * I + s
                keep = t_iota >= s
                dec = jnp.exp(jnp.where(keep, bI - b_scr[src:src + 1, :], 0.0))
                p = jnp.where(keep, qI * dec * k_scr[src:src + 1, :], 0.0)
                p_scr[src * SUB:(src + 1) * SUB, :] = p.astype(BF16)

    for h in range(HEADS):
        L = slice(h * HEAD_DIM, (h + 1) * HEAD_DIM)
        a_scr[:, L] = _dot(p_scr[:, L], ones_ref[...])
    for J in range(M // SUB):
        rows = slice(SUB * J, SUB * (J + 1))
        acc = o_scr[rows, :]
        for s in range(SUB):
            src = SUB * J + s
            acc = acc + a_scr[src * SUB:(src + 1) * SUB, :] * v_scr[src:src + 1, :]
        o_scr[rows, :] = acc

    for r in range(R):
        rows = slice(r * C, (r + 1) * C)
        b_last = b_scr[(r + 1) * C - 1:(r + 1) * C, :]
        kd = (k_scr[rows, :] * jnp.exp(b_last - b_scr[rows, :])).astype(BF16)
        eb_last = jnp.exp(b_last)
        for h in range(HEADS):
            L = slice(h * HEAD_DIM, (h + 1) * HEAD_DIM)
            vt = v_scr[rows, L].T.astype(BF16)
            st_scr[r * HEADS + h] = st_scr[r * HEADS + h] * eb_last[:, L] + _dot(vt, kd[:, L])

    gn = gn_ref[...]
    for h in range(HEADS):
        L = slice(h * HEAD_DIM, (h + 1) * HEAD_DIM)
        oh = o_scr[:, L]
        gzh = gz[:, L]
        o_scr[:, L] = _rms(oh, gn) * (gzh * _sigmoid(gzh))
    y_ref[...] = (xc + _dot(o_scr[...].astype(BF16), wout_ref[...])).reshape(R, C, D)

    @pl.when(ci == pl.num_programs(1) - 1)
    def _():
        for r in range(R):
            for h in range(HEADS):
                s_ref[r, h] = st_scr[r * HEADS + h].T


def _hgrn_prompt(x, gmix, w_in, w_out, gnorm, lb, chunk, seqs):
    B, T, D = x.shape
    C = min(chunk, T)
    R = min(seqs, B)
    assert T % C == 0 and C % SUB == 0 and B % R == 0
    M = R * C
    tril = jnp.kron(jnp.eye(R, dtype=F32), jnp.tril(jnp.ones((C, C), F32))).astype(BF16)
    ones = jnp.ones((HEAD_DIM, HEAD_DIM), BF16)
    const = lambda shape: pl.BlockSpec(shape, lambda b, c: (0,) * len(shape))
    return pl.pallas_call(
        _hgrn_prompt_kernel,
        grid=(B // R, T // C),
        in_specs=[
            pl.BlockSpec((R, C, D), lambda b, c: (b, c, 0)),
            const((1, D)), const((D, 4 * D)), const((D, D)), const((1, HEAD_DIM)), const((1, D)),
            const((M, M)), const((HEAD_DIM, HEAD_DIM)),
        ],
        out_specs=[
            pl.BlockSpec((R, C, D), lambda b, c: (b, c, 0)),
            pl.BlockSpec((R, HEADS, HEAD_DIM, HEAD_DIM), lambda b, c: (b, 0, 0, 0)),
        ],
        out_shape=[
            jax.ShapeDtypeStruct((B, T, D), F32),
            jax.ShapeDtypeStruct((B, HEADS, HEAD_DIM, HEAD_DIM), F32),
        ],
        scratch_shapes=[
            pltpu.VMEM((R * HEADS, HEAD_DIM, HEAD_DIM), F32),
            pltpu.VMEM((M, D), F32),
            pltpu.VMEM((M, D), F32),
            pltpu.VMEM((M, D), F32),
            pltpu.VMEM((M, D), F32),
            pltpu.VMEM((M * SUB, D), BF16),
            pltpu.VMEM((M * SUB, D), F32),
            pltpu.VMEM((M, D), F32),
        ],
        compiler_params=pltpu.CompilerParams(
            dimension_semantics=("arbitrary", "arbitrary"), vmem_limit_bytes=VMEM_LIMIT),
        name="hgrn_prompt",
    )(x, gmix, w_in, w_out, gnorm, lb, tril, ones)


def _hgrn_sample_kernel(x_ref, gmix_ref, win_ref, wout_ref, gn_ref, lb_ref, s_ref,
                        y_ref, snew_ref, proj_scr, o_scr):
    n = pl.program_id(0)
    D = x_ref.shape[1]

    @pl.when(n == 0)
    def _():
        hn = _rms(x_ref[...], gmix_ref[...])
        proj_scr[...] = _dot(hn.astype(BF16), win_ref[...])

    row = proj_scr[pl.ds(n, 1), :]
    qz = row[:, 0:D]
    fz = row[:, D:2 * D]
    v = row[:, 2 * D:3 * D]
    lb = lb_ref[...]
    f = lb + (1.0 - lb) * _sigmoid(fz)
    q = qz * _sigmoid(qz)
    k = 1.0 - f

    def cols(r):
        stacked = jnp.concatenate(
            [r[:, h * HEAD_DIM:(h + 1) * HEAD_DIM] for h in range(HEADS)]
            + [jnp.zeros((HEAD_DIM - HEADS, HEAD_DIM), F32)], axis=0)
        return stacked.T

    fc = cols(f)
    kc = cols(k)
    qc = cols(q)
    outs = []
    for h in range(HEADS):
        L = slice(h * HEAD_DIM, (h + 1) * HEAD_DIM)
        s_new = s_ref[0, h] * fc[:, h:h + 1] + kc[:, h:h + 1] * v[:, L]
        snew_ref[0, h] = s_new
        outs.append(jnp.sum(qc[:, h:h + 1] * s_new, axis=0, keepdims=True))
    o_scr[pl.ds(n, 1), :] = jnp.concatenate(outs, axis=1)

    @pl.when(n == pl.num_programs(0) - 1)
    def _():
        gn = gn_ref[...]
        gz = proj_scr[:, 3 * D:4 * D]
        for h in range(HEADS):
            L = slice(h * HEAD_DIM, (h + 1) * HEAD_DIM)
            gzh = gz[:, L]
            o_scr[:, L] = _rms(o_scr[:, L], gn) * (gzh * _sigmoid(gzh))
        y_ref[...] = x_ref[...] + _dot(o_scr[...].astype(BF16), wout_ref[...])


def _hgrn_sample(x, gmix, w_in, w_out, gnorm, lb, state):
    N, D = x.shape
    const = lambda shape: pl.BlockSpec(shape, lambda n: (0,) * len(shape))
    sblk = pl.BlockSpec((1, HEADS, HEAD_DIM, HEAD_DIM), lambda n: (n, 0, 0, 0))
    return pl.pallas_call(
        _hgrn_sample_kernel,
        grid=(N,),
        in_specs=[const((N, D)), const((1, D)), const((D, 4 * D)), const((D, D)),
                  const((1, HEAD_DIM)), const((1, D)), sblk],
        out_specs=[const((N, D)), sblk],
        out_shape=[jax.ShapeDtypeStruct((N, D), F32),
                   jax.ShapeDtypeStruct((N, HEADS, HEAD_DIM, HEAD_DIM), F32)],
        scratch_shapes=[pltpu.VMEM((N, 4 * D), F32), pltpu.VMEM((N, D), F32)],
        compiler_params=pltpu.CompilerParams(
            dimension_semantics=("arbitrary",), vmem_limit_bytes=VMEM_LIMIT),
        name="hgrn_sample",
    )(x, gmix, w_in, w_out, gnorm, lb, state)


CONV_PAD = 32


def _conv_tail(y, xc, bdw_ref, lng_ref, lnb_ref, pw2_ref, bpw2_ref):
    y = y + bdw_ref[...]
    mu = jnp.mean(y, axis=-1, keepdims=True)
    yc = y - mu
    var = jnp.mean(yc * yc, axis=-1, keepdims=True)
    yn = yc * lax.rsqrt(var + EPS) * lng_ref[...] + lnb_ref[...]
    z = yn * _sigmoid(yn)
    return xc + _dot(z.astype(BF16), pw2_ref[...]) + bpw2_ref[...]


def _glu(xc, gmix_ref, pw1_ref, bpw1_ref):
    D = xc.shape[1]
    hn = _rms(xc, gmix_ref[...])
    h2 = _dot(hn.astype(BF16), pw1_ref[...]) + bpw1_ref[...]
    return h2[:, 0:D] * _sigmoid(h2[:, D:2 * D])


def _conv_prompt_kernel(x_ref, gmix_ref, pw1_ref, bpw1_ref, wdw_ref, bdw_ref, lng_ref, lnb_ref,
                        pw2_ref, bpw2_ref, y_ref, buf_ref, full_scr):
    C = x_ref.shape[1]
    ci = pl.program_id(1)
    hist = CONV_WIDTH - 1
    off = CONV_PAD - hist

    @pl.when(ci == 0)
    def _():
        full_scr[0:CONV_PAD, :] = jnp.zeros((CONV_PAD, full_scr.shape[1]), F32)

    xc = x_ref[0]
    full_scr[CONV_PAD:CONV_PAD + C, :] = _glu(xc, gmix_ref, pw1_ref, bpw1_ref)
    y = full_scr[off:off + C, :] * wdw_ref[0:1, :]
    for j in range(1, CONV_WIDTH):
        y = y + full_scr[off + j:off + j + C, :] * wdw_ref[j:j + 1, :]
    y_ref[0] = _conv_tail(y, xc, bdw_ref, lng_ref, lnb_ref, pw2_ref, bpw2_ref)

    @pl.when(ci == pl.num_programs(1) - 1)
    def _():
        buf_ref[0] = full_scr[C + off:C + CONV_PAD, :]

    full_scr[0:CONV_PAD, :] = full_scr[C:C + CONV_PAD, :]


def _conv_prompt(x, gmix, pw1, bpw1, wdw, bdw, lng, lnb, pw2, bpw2, chunk):
    B, T, D = x.shape
    C = min(chunk, T)
    assert T % C == 0 and C >= CONV_PAD
    hist = CONV_WIDTH - 1
    const = lambda shape: pl.BlockSpec(shape, lambda b, c: (0,) * len(shape))
    return pl.pallas_call(
        _conv_prompt_kernel,
        grid=(B, T // C),
        in_specs=[pl.BlockSpec((1, C, D), lambda b, c: (b, c, 0)),
                  const((1, D)), const((D, 2 * D)), const((1, 2 * D)), const((CONV_WIDTH, D)),
                  const((1, D)), const((1, D)), const((1, D)), const((D, D)), const((1, D))],
        out_specs=[pl.BlockSpec((1, C, D), lambda b, c: (b, c, 0)),
                   pl.BlockSpec((1, hist, D), lambda b, c: (b, 0, 0))],
        out_shape=[jax.ShapeDtypeStruct((B, T, D), F32),
                   jax.ShapeDtypeStruct((B, hist, D), F32)],
        scratch_shapes=[pltpu.VMEM((CONV_PAD + C, D), F32)],
        compiler_params=pltpu.CompilerParams(
            dimension_semantics=("arbitrary", "arbitrary"), vmem_limit_bytes=VMEM_LIMIT),
        name="conv_prompt",
    )(x, gmix, pw1, bpw1, wdw, bdw, lng, lnb, pw2, bpw2)


def _conv_sample_kernel(x_ref, gmix_ref, pw1_ref, bpw1_ref, wdw_ref, bdw_ref, lng_ref, lnb_ref,
                        pw2_ref, bpw2_ref, buf_ref, y_ref, nbuf_ref, u_scr, y_scr):
    nt = x_ref.shape[0]
    hist = CONV_WIDTH - 1
    xc = x_ref[...]
    u_scr[...] = _glu(xc, gmix_ref, pw1_ref, bpw1_ref)
    w_hist = wdw_ref[0:hist, :]
    w_last = wdw_ref[hist:hist + 1, :]

    def body(n, carry):
        buf = buf_ref[n]
        u = u_scr[pl.ds(n, 1), :]
        y_scr[pl.ds(n, 1), :] = jnp.sum(buf * w_hist, axis=0, keepdims=True) + u * w_last
        nbuf_ref[n, 0:hist - 1, :] = buf[1:hist, :]
        nbuf_ref[n, hist - 1:hist, :] = u
        return carry

    lax.fori_loop(0, nt, body, 0)
    y_ref[...] = _conv_tail(y_scr[...], xc, bdw_ref, lng_ref, lnb_ref, pw2_ref, bpw2_ref)


def _conv_sample(x, gmix, pw1, bpw1, wdw, bdw, lng, lnb, pw2, bpw2, buf, block):
    N, D = x.shape
    nt = min(block, N)
    assert N % nt == 0
    hist = CONV_WIDTH - 1
    const = lambda shape: pl.BlockSpec(shape, lambda i: (0,) * len(shape))
    return pl.pallas_call(
        _conv_sample_kernel,
        grid=(N // nt,),
        in_specs=[pl.BlockSpec((nt, D), lambda i: (i, 0)),
                  const((1, D)), const((D, 2 * D)), const((1, 2 * D)), const((CONV_WIDTH, D)),
                  const((1, D)), const((1, D)), const((1, D)), const((D, D)), const((1, D)),
                  pl.BlockSpec((nt, hist, D), lambda i: (i, 0, 0))],
        out_specs=[pl.BlockSpec((nt, D), lambda i: (i, 0)),
                   pl.BlockSpec((nt, hist, D), lambda i: (i, 0, 0))],
        out_shape=[jax.ShapeDtypeStruct((N, D), F32),
                   jax.ShapeDtypeStruct((N, hist, D), F32)],
        scratch_shapes=[pltpu.VMEM((nt, D), F32), pltpu.VMEM((nt, D), F32)],
        compiler_params=pltpu.CompilerParams(
            dimension_semantics=("arbitrary",), vmem_limit_bytes=VMEM_LIMIT),
        name="conv_sample",
    )(x, gmix, pw1, bpw1, wdw, bdw, lng, lnb, pw2, bpw2, buf)


def _topk_ranks(vals, ids, k):
    rank = jnp.full(vals.shape, float(k), F32)
    tops = []
    for r in range(k):
        m = jnp.max(vals, axis=0, keepdims=True)
        first = jnp.min(jnp.where(vals == m, ids, 1e9), axis=0, keepdims=True)
        sel = ids == first
        vals = jnp.where(sel, NEG_INF, vals)
        rank = jnp.where(sel, float(r), rank)
        tops.append(m)
    return rank, tops


INT_MIN = -2 ** 31


def _sort_key(s):
    b = lax.bitcast_convert_type(jnp.where(s == 0.0, 0.0, s), jnp.int32)
    return b ^ (lax.shift_right_arithmetic(b, 31) & jnp.int32(0x7FFFFFFF))


def _key_value(k):
    b = k ^ (lax.shift_right_arithmetic(k, 31) & jnp.int32(0x7FFFFFFF))
    return lax.bitcast_convert_type(b, F32)


def _topk_keys(keys, k):
    tops = []
    for r in range(k):
        m = jnp.max(keys, axis=0, keepdims=True)
        keys = jnp.where(keys == m, jnp.int32(INT_MIN + r), keys)
        tops.append(m)
    return keys, tops


def _selected(keys, k):
    sel = keys < jnp.int32(INT_MIN + k)
    n = jnp.sum(jnp.where(sel, 1.0, 0.0), axis=0, keepdims=True)
    return sel, jnp.where(n != float(k), 1.0, 0.0)


def _dup_bf16(x):
    u = lax.bitcast_convert_type(x.astype(BF16).astype(F32), jnp.int32)
    return u | lax.shift_right_logical(u, 16)


def _router_kernel(x_ref, g_ref, wq_ref, keys_ref,
                   xnt_ref, rank2_ref, e2_ref, c1_ref, e1_ref,
                   s_scr, rank_scr, top_scr, miss_scr):
    tn = x_ref.shape[0]
    xn = _rms(x_ref[...], g_ref[...])
    xnt = xn.T.astype(BF16)
    xnt_ref[...] = xnt
    qt = _dot(wq_ref[...], xnt).astype(BF16)
    for hp in range(2 * HEADS):
        s_scr[hp] = _dot(keys_ref[hp], qt[hp * HEAD_DIM:(hp + 1) * HEAD_DIM, :])

    outs = (rank2_ref, e2_ref, c1_ref, e1_ref)
    scr = (s_scr, rank_scr, top_scr)
    _select_experts(tn, scr, outs, miss_scr)


def _run_with_tie_fallback(n, per_group, problem, miss_scr):
    def fast(g, carry):
        miss = [problem(g * per_group + c, False) for c in range(per_group)]
        miss_scr[pl.ds(g, 1), :] = functools.reduce(jnp.maximum, miss)
        return carry

    lax.fori_loop(0, n // per_group, fast, 0, unroll=2 if per_group == 1 else 1)

    def fix(g, carry):
        @pl.when(jnp.max(miss_scr[pl.ds(g, 1), :]) > 0.0)
        def _():
            for c in range(per_group):
                problem(g * per_group + c, True)
        return carry

    lax.fori_loop(0, n // per_group, fix, 0)


def _select_experts(tn, scr, outs, miss_scr):
    s_scr, rank_scr, top_scr = scr
    rank2_ref, e2_ref, c1_ref, e1_ref = outs
    nch = tn // LANES
    key_ids = lax.broadcasted_iota(jnp.int32, (HEAD_DIM, LANES), 0).astype(F32)

    def half_problem(it, exact):
        hp = it // nch
        lanes = pl.ds(pl.multiple_of((it % nch) * LANES, LANES), LANES)
        s = s_scr[hp, :, lanes]
        miss = None
        if exact:
            rank, tops = _topk_ranks(s, key_ids, TOPK)
            top = jnp.concatenate(tops, axis=0)
        else:
            keys, tops = _topk_keys(_sort_key(s), TOPK)
            sel, miss = _selected(keys, TOPK)
            rank = jnp.where(sel, (keys ^ jnp.int32(INT_MIN)).astype(F32), float(TOPK))
            top = _key_value(jnp.concatenate(tops, axis=0))
        rank_scr[hp, :, lanes] = rank
        top_scr[hp, :, lanes] = top
        return miss

    _run_with_tie_fallback(2 * HEADS * nch, nch, half_problem, miss_scr)

    row16 = lax.broadcasted_iota(jnp.int32, (16, LANES), 0)
    row8 = lax.broadcasted_iota(jnp.int32, (8, LANES), 0)
    r16 = row16.astype(F32)
    r8 = row8.astype(F32)
    groups_valid = [row16 >= 0, row8 >= 0, row16 >= 2, row8 >= 2,
                    (row8 >= 2) & (row8 <= 4), (row8 >= 2) & (row8 <= 3), row8 == 2]
    groups_ids = [r16, 16.0 + r8, r16 * 16.0, r8 * 16.0 + 1.0, 32.0 + r8, 48.0 + r8, 64.0 + r8]
    cand_ids = jnp.concatenate(
        [jnp.where(v, i, 1e9) for v, i in zip(groups_valid, groups_ids)], axis=0)

    def head_problem(it, exact):
        h = it // nch
        lanes = pl.ds(pl.multiple_of((it % nch) * LANES, LANES), LANES)
        v1 = top_scr[2 * h, :, lanes]
        v2 = top_scr[2 * h + 1, :, lanes]
        groups = [v1[0:1] + v2, v1[1:2] + v2[0:8], v1 + v2[0:1], v1[0:8] + v2[1:2],
                  v1[2:3] + v2[0:8], v1[3:4] + v2[0:8], v1[4:5] + v2[0:8]]
        cand = jnp.concatenate(
            [jnp.where(v, g, NEG_INF) for v, g in zip(groups_valid, groups)], axis=0)
        miss = None
        if exact:
            crank, _ = _topk_ranks(cand, cand_ids, TOPK)
            sel = crank < float(TOPK)
        else:
            ckeys, _ = _topk_keys(_sort_key(cand), TOPK)
            sel, miss = _selected(ckeys, TOPK)
        z = jnp.sum(jnp.where(sel, jnp.exp(jnp.where(sel, cand - cand[0:1], 0.0)), 0.0),
                    axis=0, keepdims=True)
        self32 = jnp.where(sel, 1.0, 0.0)
        cnt = lambda a, b: jnp.sum(self32[a:b], axis=0, keepdims=True)
        per_r1 = (self32[24:40]
                  + jnp.concatenate([self32[40:48], jnp.zeros((8, LANES), F32)], axis=0)
                  + jnp.where(row16 == 0, cnt(0, 16), 0.0)
                  + jnp.where(row16 == 1, cnt(16, 24), 0.0)
                  + jnp.where(row16 == 2, cnt(48, 56), 0.0)
                  + jnp.where(row16 == 3, cnt(56, 64), 0.0)
                  + jnp.where(row16 == 4, cnt(64, 72), 0.0))
        rank1 = rank_scr[2 * h, :, lanes]
        rank2 = rank_scr[2 * h + 1, :, lanes]
        c1 = jnp.zeros((HEAD_DIM, LANES), F32)
        for j in range(1, TOPK // 3 + 1):
            reach = jnp.sum(jnp.where(per_r1 >= float(j), 1.0, 0.0), axis=0, keepdims=True)
            c1 = c1 + jnp.where(rank1 < reach, 1.0, 0.0)
        c1 = jnp.where(rank1 == 0.0, per_r1[0:1], jnp.where(rank1 == 1.0, per_r1[1:2], c1))
        s1 = s_scr[2 * h, :, lanes]
        s2 = s_scr[2 * h + 1, :, lanes]
        in1 = rank1 < float(TOPK)
        in2 = rank2 < float(TOPK)
        e1 = jnp.where(in1, jnp.exp(jnp.where(in1, s1 - v1[0:1], 0.0)), 0.0)
        e2 = jnp.where(in2, jnp.exp(jnp.where(in2, s2 - v2[0:1], 0.0)), 0.0) / z
        rank2_ref[h, :, lanes] = rank2.astype(BF16)
        e2_ref[h, :, lanes] = (0.5 * e2).astype(BF16)
        c1_ref[h, :, lanes] = _dup_bf16(c1)
        e1_ref[h, :, lanes] = _dup_bf16(e1)
        return miss

    _run_with_tie_fallback(HEADS * nch, nch, head_problem, miss_scr)


def _router(x, g, wq_t, keys, tn):
    N, D = x.shape
    assert N % tn == 0 and tn % LANES == 0
    const = lambda shape: pl.BlockSpec(shape, lambda i: (0,) * len(shape))
    gate_blk = pl.BlockSpec((HEADS, HEAD_DIM, tn), lambda i: (0, 0, i))
    gate_shape = lambda dt: jax.ShapeDtypeStruct((HEADS, HEAD_DIM, N), dt)
    return pl.pallas_call(
        _router_kernel,
        grid=(N // tn,),
        in_specs=[pl.BlockSpec((tn, D), lambda i: (i, 0)), const((1, D)),
                  const(wq_t.shape), const(keys.shape)],
        out_specs=[pl.BlockSpec((D, tn), lambda i: (0, i)),
                   gate_blk, gate_blk, gate_blk, gate_blk],
        out_shape=[jax.ShapeDtypeStruct((D, N), BF16),
                   gate_shape(BF16), gate_shape(BF16), gate_shape(jnp.int32),
                   gate_shape(jnp.int32)],
        scratch_shapes=[pltpu.VMEM((2 * HEADS, HEAD_DIM, tn), F32),
                        pltpu.VMEM((2 * HEADS, HEAD_DIM, tn), F32),
                        pltpu.VMEM((2 * HEADS, TOPK, tn), F32),
                        pltpu.VMEM((2 * HEADS * (tn // LANES), LANES), F32)],
        compiler_params=pltpu.CompilerParams(
            dimension_semantics=("arbitrary",), vmem_limit_bytes=VMEM_LIMIT),
        name="peer_router",
    )(x, g, wq_t, keys)


ROW_TILE = 16
MXU_TILE = 256


def _peer_dense_kernel(xnt_ref, u_ref, vt_ref, rank2_ref, e2_ref, c1_ref, e1_ref, x_ref, gfin_ref,
                       y_ref, hid0, hid1, g_scr, acc_scr, rank2_scr, e2_scr, *, final_norm):
    t = pl.program_id(1)
    te, tn = hid0.shape
    nb = te // HEAD_DIM

    @pl.when(t == 0)
    def _():
        for ref in (hid0, hid1, acc_scr):
            ref[...] = jnp.zeros_like(ref)
        rank2_scr[...] = rank2_ref[...]
        e2_scr[...] = e2_ref[...]

    def gate_group(hid_cur, i1, il, ch):
        lanes = slice(ch * LANES, (ch + 1) * LANES)
        bcast = lambda ref, h: pltpu.bitcast(
            jnp.broadcast_to(ref[h, i1:i1 + 1, lanes], (8, LANES)), BF16)
        c1 = [bcast(c1_ref, h) for h in range(HEADS)]
        e1 = [bcast(e1_ref, h) for h in range(HEADS)]
        for r0 in range(0, HEAD_DIM, ROW_TILE):
            keys2 = slice(r0, r0 + ROW_TILE)
            rows = slice(il * HEAD_DIM + r0, il * HEAD_DIM + r0 + ROW_TILE)
            w = jnp.zeros((ROW_TILE, LANES), BF16)
            for h in range(HEADS):
                w = w + jnp.where(rank2_scr[h, keys2, lanes] < c1[h],
                                  e2_scr[h, keys2, lanes] * e1[h], jnp.zeros_like(w))
            hid = hid_cur[rows, lanes]
            gelu2 = hid * (1.0 + lax.erf(hid * 0.7071067811865476))
            g_scr[rows, lanes] = w * gelu2.astype(BF16)

    def stage(hid_new, hid_cur, i1_base):
        width = min(MXU_TILE, tn)
        depth = min(MXU_TILE, te)
        for k0 in range(0, te, depth):
            for n0 in range(0, tn, width):
                hid_new[k0:k0 + depth, n0:n0 + width] = _dot(u_ref[k0:k0 + depth, :],
                                                             xnt_ref[:, n0:n0 + width])
                for il in range(k0 // HEAD_DIM, (k0 + depth) // HEAD_DIM):
                    for ch in range(n0 // LANES, (n0 + width) // LANES):
                        gate_group(hid_cur, i1_base + il, il, ch)
                acc_scr[:, n0:n0 + width] += _dot(vt_ref[0, :, k0:k0 + depth],
                                                  g_scr[k0:k0 + depth, n0:n0 + width])

    group = c1_ref.shape[1]

    @pl.when(t % 2 == 0)
    def _():
        stage(hid0, hid1, nb % group)

    @pl.when(t % 2 == 1)
    def _():
        stage(hid1, hid0, 0)

    @pl.when(t == pl.num_programs(1) - 1)
    def _():
        y = x_ref[...] + acc_scr[...].T
        if final_norm:
            y = _rms(y, gfin_ref[...])
        y_ref[...] = y


def _peer_dense(xnt, u, vt, rank2, e2, c1, e1, x, gfin, tn, final_norm):
    N, D = x.shape
    nblk, _, te = vt.shape
    assert N % tn == 0 and u.shape[0] == nblk * te and te % HEAD_DIM == 0
    nb = te // HEAD_DIM
    group = max(8, nb)
    assert group // nb <= 2 and HEAD_DIM % group == 0
    key2_blk = pl.BlockSpec((HEADS, HEAD_DIM, tn), lambda i, t: (0, 0, i))
    key1_blk = pl.BlockSpec(
        (HEADS, group, tn),
        lambda i, t: (0, jnp.clip(((t - 1) * nb) // group, 0, HEAD_DIM // group - 1), i))
    return pl.pallas_call(
        functools.partial(_peer_dense_kernel, final_norm=final_norm),
        grid=(N // tn, nblk + 1),
        in_specs=[pl.BlockSpec((D, tn), lambda i, t: (0, i)),
                  pl.BlockSpec((te, D), lambda i, t: (jnp.minimum(t, nblk - 1), 0)),
                  pl.BlockSpec((1, D, te), lambda i, t: (jnp.maximum(t - 1, 0), 0, 0)),
                  key2_blk, key2_blk, key1_blk, key1_blk,
                  pl.BlockSpec((tn, D), lambda i, t: (i, 0)),
                  pl.BlockSpec((1, D), lambda i, t: (0, 0))],
        out_specs=pl.BlockSpec((tn, D), lambda i, t: (i, 0)),
        out_shape=jax.ShapeDtypeStruct((N, D), F32),
        scratch_shapes=[pltpu.VMEM((te, tn), F32), pltpu.VMEM((te, tn), F32),
                        pltpu.VMEM((te, tn), BF16),
                        pltpu.VMEM((D, tn), F32),
                        pltpu.VMEM((HEADS, HEAD_DIM, tn), BF16),
                        pltpu.VMEM((HEADS, HEAD_DIM, tn), BF16)],
        compiler_params=pltpu.CompilerParams(
            dimension_semantics=("arbitrary", "arbitrary"), vmem_limit_bytes=VMEM_LIMIT),
        name="peer_dense",
    )(xnt, u, vt, rank2, e2, c1, e1, x, gfin)


def _peer(x, g, wq_t, keys, u, vt, gfin, tn_router, tn_dense, final_norm):
    xnt, rank2, e2, c1, e1 = _router(x, g, wq_t, keys, tn_router)
    return _peer_dense(xnt, u, vt, rank2, e2, c1, e1, x, gfin, tn_dense, final_norm)


HGRN_CHUNK = 64
HGRN_SEQS = 2
CONV_CHUNK = 256
CONV_SAMPLE_BLOCK = 32
PEER_ROUTER_TOKENS = 512
PEER_DENSE_TOKENS = 1024
PEER_EXPERTS = 512


def kernel(x_prompt, x_sample, state_hgrn, state_conv, norm_mix, norm_ffn, norm_final, hg_w_in, hg_w_out, hg_gnorm, hg_lb_logits, cv_w_pw1, cv_b_pw1, cv_w_dw, cv_b_dw, cv_ln_g, cv_ln_b, cv_w_pw2, cv_b_pw2, peer_w_q, peer_sub_keys, peer_u, peer_v):
    B, T, D = x_prompt.shape
    NS = x_sample.shape[0]
    depth = norm_mix.shape[0]
    row = lambda a: a.reshape(1, -1)

    lb_all = jnp.cumsum(jax.nn.softmax(hg_lb_logits.astype(F32), axis=0), axis=0)

    xp = x_prompt
    xs = x_sample.reshape(NS, D)
    hg_p, hg_s, cv_p, cv_s = [], [], [], []
    for i in range(depth):
        j = i // 2
        gmix = row(norm_mix[i])
        if i % 2 == 0:
            w_in = hg_w_in[j].astype(BF16)
            w_out = hg_w_out[j].astype(BF16)
            gn = row(hg_gnorm[j])
            lb = row(lb_all[i])
            xp, sp = _hgrn_prompt(xp, gmix, w_in, w_out, gn, lb, HGRN_CHUNK, HGRN_SEQS)
            xs, ss = _hgrn_sample(xs, gmix, w_in, w_out, gn, lb, state_hgrn[j])
            hg_p.append(sp)
            hg_s.append(ss)
        else:
            cw = (cv_w_pw1[j].astype(BF16), row(cv_b_pw1[j]), cv_w_dw[j], row(cv_b_dw[j]),
                  row(cv_ln_g[j]), row(cv_ln_b[j]), cv_w_pw2[j].astype(BF16), row(cv_b_pw2[j]))
            xp, bp = _conv_prompt(xp, gmix, *cw, CONV_CHUNK)
            xs, bs = _conv_sample(xs, gmix, *cw, state_conv[j], CONV_SAMPLE_BLOCK)
            cv_p.append(bp)
            cv_s.append(bs)
        wq_t = peer_w_q[i].T.astype(BF16)
        keys = peer_sub_keys[i].reshape(2 * HEADS, HEAD_DIM, HEAD_DIM).astype(BF16)
        u = peer_u[i].astype(BF16)
        vt = peer_v[i].reshape(-1, PEER_EXPERTS, D).transpose(0, 2, 1).astype(BF16)
        gffn = row(norm_ffn[i])
        gfin = row(norm_final)
        last = i == depth - 1
        xp = _peer(xp.reshape(B * T, D), gffn, wq_t, keys, u, vt, gfin,
                   min(PEER_ROUTER_TOKENS, B * T), min(PEER_DENSE_TOKENS, B * T),
                   last).reshape(B, T, D)
        xs = _peer(xs, gffn, wq_t, keys, u, vt, gfin, min(LANES, NS), min(LANES, NS), last)
    return (xp, xs.reshape(NS, 1, D), jnp.stack(hg_p), jnp.stack(hg_s),
            jnp.stack(cv_p), jnp.stack(cv_s))
```

```python
import functools

import jax
import jax.numpy as jnp
from jax import lax
from jax.experimental import pallas as pl
from jax.experimental.pallas import tpu as pltpu

F32 = jnp.float32
BF16 = jnp.bfloat16
EPS = 1e-6

HEADS = 8
HEAD_DIM = 128
TOPK = 16
CONV_WIDTH = 31
LANES = 128
SUB = 16
VMEM_LIMIT = 56 * 1024 * 1024

NEG_INF = float("-inf")


def _rms(x, g):
    return x * lax.rsqrt(jnp.mean(x * x, axis=-1, keepdims=True) + EPS) * g


def _sigmoid(x):
    return 1.0 / (1.0 + jnp.exp(-x))


def _dot(a, b):
    return jnp.dot(a, b, preferred_element_type=F32)


def _dot_nt(a, b):
    return lax.dot_general(a, b, (((1,), (1,)), ((), ())), preferred_element_type=F32)


def _hgrn_prompt_kernel(x_ref, gmix_ref, win_ref, wout_ref, gn_ref, lb_ref, tril_ref, ones_ref,
                        y_ref, s_ref,
                        st_scr, q_scr, k_scr, v_scr, b_scr, p_scr, a_scr, o_scr):
    R, C, D = x_ref.shape
    M = R * C
    nsub = C // SUB
    ci = pl.program_id(1)

    @pl.when(ci == 0)
    def _():
        st_scr[...] = jnp.zeros_like(st_scr)

    xc = x_ref[...].reshape(M, D)
    hn = _rms(xc, gmix_ref[...])
    proj = _dot(hn.astype(BF16), win_ref[...])
    qz = proj[:, 0:D]
    fz = proj[:, D:2 * D]
    gz = proj[:, 3 * D:4 * D]
    lb = lb_ref[...]
    f = lb + (1.0 - lb) * _sigmoid(fz)
    q_scr[...] = qz * _sigmoid(qz)
    k_scr[...] = 1.0 - f
    v_scr[...] = proj[:, 2 * D:3 * D]
    lf = jnp.log(f)
    hi = lf.astype(BF16)
    rem = lf - hi.astype(F32)
    mid = rem.astype(BF16)
    lo = (rem - mid.astype(F32)).astype(BF16)
    tril = tril_ref[...]
    b_scr[...] = _dot(tril, hi) + _dot(tril, mid) + _dot(tril, lo)

    qe = (q_scr[...] * jnp.exp(b_scr[...])).astype(BF16)
    for r in range(R):
        for h in range(HEADS):
            L = slice(h * HEAD_DIM, (h + 1) * HEAD_DIM)
            o_scr[r * C:(r + 1) * C, L] = _dot_nt(qe[r * C:(r + 1) * C, L],
                                                  st_scr[r * HEADS + h].astype(BF16))

    t_iota = lax.broadcasted_iota(jnp.int32, (SUB, D), 0)
    for r in range(R):
        base = r * C
        for I in range(nsub):
            rows = slice(base + SUB * I, base + SUB * (I + 1))
            prev = slice(base, base + SUB * I)
            bI = b_scr[rows, :]
            qI = q_scr[rows, :]
            if I > 0:
                beta = b_scr[base + SUB * I - 1:base + SUB * I, :]
                qs = (qI * jnp.exp(bI - beta)).astype(BF16)
                ks = (k_scr[prev, :] * jnp.exp(beta - b_scr[prev, :])).astype(BF16)
                vs = v_scr[prev, :].astype(BF16)
                for h in range(HEADS):
                    L = slice(h * HEAD_DIM, (h + 1) * HEAD_DIM)
                    att = _dot_nt(qs[:, L], ks[:, L])
                    o_scr[rows, L] += _dot(att.astype(BF16), vs[:, L])
            for s in range(SUB):
                src = base + SUB * I + s
                keep = t_iota >= s
                dec = jnp.exp(jnp.where(keep, bI - b_scr[src:src + 1, :], 0.0))
                p = jnp.where(keep, qI * dec * k_scr[src:src + 1, :], 0.0)
                p_scr[src * SUB:(src + 1) * SUB, :] = p.astype(BF16)

    for h in range(HEADS):
        L = slice(h * HEAD_DIM, (h + 1) * HEAD_DIM)
        a_scr[:, L] = _dot(p_scr[:, L], ones_ref[...])
    for J in range(M // SUB):
        rows = slice(SUB * J, SUB * (J + 1))
        acc = o_scr[rows, :]
        for s in range(SUB):
            src = SUB * J + s
            acc = acc + a_scr[src * SUB:(src + 1) * SUB, :] * v_scr[src:src + 1, :]
        o_scr[rows, :] = acc

    for r in range(R):
        rows = slice(r * C, (r + 1) * C)
        b_last = b_scr[(r + 1) * C - 1:(r + 1) * C, :]
        kd = (k_scr[rows, :] * jnp.exp(b_last - b_scr[rows, :])).astype(BF16)
        eb_last = jnp.exp(b_last)
        for h in range(HEADS):
            L = slice(h * HEAD_DIM, (h + 1) * HEAD_DIM)
            vt = v_scr[rows, L].T.astype(BF16)
            st_scr[r * HEADS + h] = st_scr[r * HEADS + h] * eb_last[:, L] + _dot(vt, kd[:, L])

    gn = gn_ref[...]
    for h in range(HEADS):
        L = slice(h * HEAD_DIM, (h + 1) * HEAD_DIM)
        oh = o_scr[:, L]
        gzh = gz[:, L]
        o_scr[:, L] = _rms(oh, gn) * (gzh * _sigmoid(gzh))
    y_ref[...] = (xc + _dot(o_scr[...].astype(BF16), wout_ref[...])).reshape(R, C, D)

    @pl.when(ci == pl.num_programs(1) - 1)
    def _():
        for r in range(R):
            for h in range(HEADS):
                s_ref[r, h] = st_scr[r * HEADS + h].T


def _hgrn_prompt(x, gmix, w_in, w_out, gnorm, lb, chunk, seqs):
    B, T, D = x.shape
    C = min(chunk, T)
    R = min(seqs, B)
    assert T % C == 0 and C % SUB == 0 and B % R == 0
    M = R * C
    tril = jnp.kron(jnp.eye(R, dtype=F32), jnp.tril(jnp.ones((C, C), F32))).astype(BF16)
    ones = jnp.ones((HEAD_DIM, HEAD_DIM), BF16)
    const = lambda shape: pl.BlockSpec(shape, lambda b, c: (0,) * len(shape))
    return pl.pallas_call(
        _hgrn_prompt_kernel,
        grid=(B // R, T // C),
        in_specs=[
            pl.BlockSpec((R, C, D), lambda b, c: (b, c, 0)),
            const((1, D)), const((D, 4 * D)), const((D, D)), const((1, HEAD_DIM)), const((1, D)),
            const((M, M)), const((HEAD_DIM, HEAD_DIM)),
        ],
        out_specs=[
            pl.BlockSpec((R, C, D), lambda b, c: (b, c, 0)),
            pl.BlockSpec((R, HEADS, HEAD_DIM, HEAD_DIM), lambda b, c: (b, 0, 0, 0)),
        ],
        out_shape=[
            jax.ShapeDtypeStruct((B, T, D), F32),
            jax.ShapeDtypeStruct((B, HEADS, HEAD_DIM, HEAD_DIM), F32),
        ],
        scratch_shapes=[
            pltpu.VMEM((R * HEADS, HEAD_DIM, HEAD_DIM), F32),
            pltpu.VMEM((M, D), F32),
            pltpu.VMEM((M, D), F32),
            pltpu.VMEM((M, D), F32),
            pltpu.VMEM((M, D), F32),
            pltpu.VMEM((M * SUB, D), BF16),
            pltpu.VMEM((M * SUB, D), F32),
            pltpu.VMEM((M, D), F32),
        ],
        compiler_params=pltpu.CompilerParams(
            dimension_semantics=("arbitrary", "arbitrary"), vmem_limit_bytes=VMEM_LIMIT),
        name="hgrn_prompt",
    )(x, gmix, w_in, w_out, gnorm, lb, tril, ones)


def _hgrn_sample_kernel(x_ref, gmix_ref, win_ref, wout_ref, gn_ref, lb_ref, s_ref,
                        y_ref, snew_ref, proj_scr, o_scr):
    n = pl.program_id(0)
    D = x_ref.shape[1]

    @pl.when(n == 0)
    def _():
        hn = _rms(x_ref[...], gmix_ref[...])
        proj_scr[...] = _dot(hn.astype(BF16), win_ref[...])

    row = proj_scr[pl.ds(n, 1), :]
    qz = row[:, 0:D]
    fz = row[:, D:2 * D]
    v = row[:, 2 * D:3 * D]
    lb = lb_ref[...]
    f = lb + (1.0 - lb) * _sigmoid(fz)
    q = qz * _sigmoid(qz)
    k = 1.0 - f

    def cols(r):
        stacked = jnp.concatenate(
            [r[:, h * HEAD_DIM:(h + 1) * HEAD_DIM] for h in range(HEADS)]
            + [jnp.zeros((HEAD_DIM - HEADS, HEAD_DIM), F32)], axis=0)
        return stacked.T

    fc = cols(f)
    kc = cols(k)
    qc = cols(q)
    outs = []
    for h in range(HEADS):
        L = slice(h * HEAD_DIM, (h + 1) * HEAD_DIM)
        s_new = s_ref[0, h] * fc[:, h:h + 1] + kc[:, h:h + 1] * v[:, L]
        snew_ref[0, h] = s_new
        outs.append(jnp.sum(qc[:, h:h + 1] * s_new, axis=0, keepdims=True))
    o_scr[pl.ds(n, 1), :] = jnp.concatenate(outs, axis=1)

    @pl.when(n == pl.num_programs(0) - 1)
    def _():
        gn = gn_ref[...]
        gz = proj_scr[:, 3 * D:4 * D]
        for h in range(HEADS):
            L = slice(h * HEAD_DIM, (h + 1) * HEAD_DIM)
            gzh = gz[:, L]
            o_scr[:, L] = _rms(o_scr[:, L], gn) * (gzh * _sigmoid(gzh))
        y_ref[...] = x_ref[...] + _dot(o_scr[...].astype(BF16), wout_ref[...])


def _hgrn_sample(x, gmix, w_in, w_out, gnorm, lb, state):
    N, D = x.shape
    const = lambda shape: pl.BlockSpec(shape, lambda n: (0,) * len(shape))
    sblk = pl.BlockSpec((1, HEADS, HEAD_DIM, HEAD_DIM), lambda n: (n, 0, 0, 0))
    return pl.pallas_call(
        _hgrn_sample_kernel,
        grid=(N,),
        in_specs=[const((N, D)), const((1, D)), const((D, 4 * D)), const((D, D)),
                  const((1, HEAD_DIM)), const((1, D)), sblk],
        out_specs=[const((N, D)), sblk],
        out_shape=[jax.ShapeDtypeStruct((N, D), F32),
                   jax.ShapeDtypeStruct((N, HEADS, HEAD_DIM, HEAD_DIM), F32)],
        scratch_shapes=[pltpu.VMEM((N, 4 * D), F32), pltpu.VMEM((N, D), F32)],
        compiler_params=pltpu.CompilerParams(
            dimension_semantics=("arbitrary",), vmem_limit_bytes=VMEM_LIMIT),
        name="hgrn_sample",
    )(x, gmix, w_in, w_out, gnorm, lb, state)


CONV_PAD = 32


def _conv_tail(y, xc, bdw_ref, lng_ref, lnb_ref, pw2_ref, bpw2_ref):
    y = y + bdw_ref[...]
    mu = jnp.mean(y, axis=-1, keepdims=True)
    yc = y - mu
    var = jnp.mean(yc * yc, axis=-1, keepdims=True)
    yn = yc * lax.rsqrt(var + EPS) * lng_ref[...] + lnb_ref[...]
    z = yn * _sigmoid(yn)
    return xc + _dot(z.astype(BF16), pw2_ref[...]) + bpw2_ref[...]


def _glu(xc, gmix_ref, pw1_ref, bpw1_ref):
    D = xc.shape[1]
    hn = _rms(xc, gmix_ref[...])
    h2 = _dot(hn.astype(BF16), pw1_ref[...]) + bpw1_ref[...]
    return h2[:, 0:D] * _sigmoid(h2[:, D:2 * D])


def _conv_prompt_kernel(x_ref, gmix_ref, pw1_ref, bpw1_ref, wdw_ref, bdw_ref, lng_ref, lnb_ref,
                        pw2_ref, bpw2_ref, y_ref, buf_ref, full_scr):
    C = x_ref.shape[1]
    ci = pl.program_id(1)
    hist = CONV_WIDTH - 1
    off = CONV_PAD - hist

    @pl.when(ci == 0)
    def _():
        full_scr[0:CONV_PAD, :] = jnp.zeros((CONV_PAD, full_scr.shape[1]), F32)

    xc = x_ref[0]
    full_scr[CONV_PAD:CONV_PAD + C, :] = _glu(xc, gmix_ref, pw1_ref, bpw1_ref)
    y = full_scr[off:off + C, :] * wdw_ref[0:1, :]
    for j in range(1, CONV_WIDTH):
        y = y + full_scr[off + j:off + j + C, :] * wdw_ref[j:j + 1, :]
    y_ref[0] = _conv_tail(y, xc, bdw_ref, lng_ref, lnb_ref, pw2_ref, bpw2_ref)

    @pl.when(ci == pl.num_programs(1) - 1)
    def _():
        buf_ref[0] = full_scr[C + off:C + CONV_PAD, :]

    full_scr[0:CONV_PAD, :] = full_scr[C:C + CONV_PAD, :]


def _conv_prompt(x, gmix, pw1, bpw1, wdw, bdw, lng, lnb, pw2, bpw2, chunk):
    B, T, D = x.shape
    C = min(chunk, T)
    assert T % C == 0 and C >= CONV_PAD
    hist = CONV_WIDTH - 1
    const = lambda shape: pl.BlockSpec(shape, lambda b, c: (0,) * len(shape))
    return pl.pallas_call(
        _conv_prompt_kernel,
        grid=(B, T // C),
        in_specs=[pl.BlockSpec((1, C, D), lambda b, c: (b, c, 0)),
                  const((1, D)), const((D, 2 * D)), const((1, 2 * D)), const((CONV_WIDTH, D)),
                  const((1, D)), const((1, D)), const((1, D)), const((D, D)), const((1, D))],
        out_specs=[pl.BlockSpec((1, C, D), lambda b, c: (b, c, 0)),
                   pl.BlockSpec((1, hist, D), lambda b, c: (b, 0, 0))],
        out_shape=[jax.ShapeDtypeStruct((B, T, D), F32),
                   jax.ShapeDtypeStruct((B, hist, D), F32)],
        scratch_shapes=[pltpu.VMEM((CONV_PAD + C, D), F32)],
        compiler_params=pltpu.CompilerParams(
            dimension_semantics=("arbitrary", "arbitrary"), vmem_limit_bytes=VMEM_LIMIT),
        name="conv_prompt",
    )(x, gmix, pw1, bpw1, wdw, bdw, lng, lnb, pw2, bpw2)


def _conv_sample_kernel(x_ref, gmix_ref, pw1_ref, bpw1_ref, wdw_ref, bdw_ref, lng_ref, lnb_ref,
                        pw2_ref, bpw2_ref, buf_ref, y_ref, nbuf_ref, u_scr, y_scr):
    nt = x_ref.shape[0]
    hist = CONV_WIDTH - 1
    xc = x_ref[...]
    u_scr[...] = _glu(xc, gmix_ref, pw1_ref, bpw1_ref)
    w_hist = wdw_ref[0:hist, :]
    w_last = wdw_ref[hist:hist + 1, :]

    def body(n, carry):
        buf = buf_ref[n]
        u = u_scr[pl.ds(n, 1), :]
        y_scr[pl.ds(n, 1), :] = jnp.sum(buf * w_hist, axis=0, keepdims=True) + u * w_last
        nbuf_ref[n, 0:hist - 1, :] = buf[1:hist, :]
        nbuf_ref[n, hist - 1:hist, :] = u
        return carry

    lax.fori_loop(0, nt, body, 0)
    y_ref[...] = _conv_tail(y_scr[...], xc, bdw_ref, lng_ref, lnb_ref, pw2_ref, bpw2_ref)


def _conv_sample(x, gmix, pw1, bpw1, wdw, bdw, lng, lnb, pw2, bpw2, buf, block):
    N, D = x.shape
    nt = min(block, N)
    assert N % nt == 0
    hist = CONV_WIDTH - 1
    const = lambda shape: pl.BlockSpec(shape, lambda i: (0,) * len(shape))
    return pl.pallas_call(
        _conv_sample_kernel,
        grid=(N // nt,),
        in_specs=[pl.BlockSpec((nt, D), lambda i: (i, 0)),
                  const((1, D)), const((D, 2 * D)), const((1, 2 * D)), const((CONV_WIDTH, D)),
                  const((1, D)), const((1, D)), const((1, D)), const((D, D)), const((1, D)),
                  pl.BlockSpec((nt, hist, D), lambda i: (i, 0, 0))],
        out_specs=[pl.BlockSpec((nt, D), lambda i: (i, 0)),
                   pl.BlockSpec((nt, hist, D), lambda i: (i, 0, 0))],
        out_shape=[jax.ShapeDtypeStruct((N, D), F32),
                   jax.ShapeDtypeStruct((N, hist, D), F32)],
        scratch_shapes=[pltpu.VMEM((nt, D), F32), pltpu.VMEM((nt, D), F32)],
        compiler_params=pltpu.CompilerParams(
            dimension_semantics=("arbitrary",), vmem_limit_bytes=VMEM_LIMIT),
        name="conv_sample",
    )(x, gmix, pw1, bpw1, wdw, bdw, lng, lnb, pw2, bpw2, buf)


def _topk_ranks(vals, ids, k):
    rank = jnp.full(vals.shape, float(k), F32)
    tops = []
    for r in range(k):
        m = jnp.max(vals, axis=0, keepdims=True)
        first = jnp.min(jnp.where(vals == m, ids, 1e9), axis=0, keepdims=True)
        sel = ids == first
        vals = jnp.where(sel, NEG_INF, vals)
        rank = jnp.where(sel, float(r), rank)
        tops.append(m)
    return rank, tops


INT_MIN = -2 ** 31


def _sort_key(s):
    b = lax.bitcast_convert_type(jnp.where(s == 0.0, 0.0, s), jnp.int32)
    return b ^ (lax.shift_right_arithmetic(b, 31) & jnp.int32(0x7FFFFFFF))


def _key_value(k):
    b = k ^ (lax.shift_right_arithmetic(k, 31) & jnp.int32(0x7FFFFFFF))
    return lax.bitcast_convert_type(b, F32)


def _topk_keys(keys, k):
    tops = []
    for r in range(k):
        m = jnp.max(keys, axis=0, keepdims=True)
        keys = jnp.where(keys == m, jnp.int32(INT_MIN + r), keys)
        tops.append(m)
    return keys, tops


def _selected(keys, k):
    sel = keys < jnp.int32(INT_MIN + k)
    n = jnp.sum(jnp.where(sel, 1.0, 0.0), axis=0, keepdims=True)
    return sel, jnp.where(n != float(k), 1.0, 0.0)


def _dup_bf16(x):
    u = lax.bitcast_convert_type(x.astype(BF16).astype(F32), jnp.int32)
    return u | lax.shift_right_logical(u, 16)


def _router_kernel(x_ref, g_ref, wq_ref, keys_ref,
                   xnt_ref, rank2_ref, e2_ref, c1_ref, e1_ref,
                   s_scr, rank_scr, top_scr, miss_scr):
    tn = x_ref.shape[0]
    xn = _rms(x_ref[...], g_ref[...])
    xnt = xn.T.astype(BF16)
    xnt_ref[...] = xnt
    qt = _dot(wq_ref[...], xnt).astype(BF16)
    for hp in range(2 * HEADS):
        s_scr[hp] = _dot(keys_ref[hp], qt[hp * HEAD_DIM:(hp + 1) * HEAD_DIM, :])

    outs = (rank2_ref, e2_ref, c1_ref, e1_ref)
    scr = (s_scr, rank_scr, top_scr)
    _select_experts(tn, scr, outs, miss_scr)


def _run_with_tie_fallback(n, per_group, problem, miss_scr):
    def fast(g, carry):
        miss = [problem(g * per_group + c, False) for c in range(per_group)]
        miss_scr[pl.ds(g, 1), :] = functools.reduce(jnp.maximum, miss)
        return carry

    lax.fori_loop(0, n // per_group, fast, 0, unroll=2 if per_group == 1 else 1)

    def fix(g, carry):
        @pl.when(jnp.max(miss_scr[pl.ds(g, 1), :]) > 0.0)
        def _():
            for c in range(per_group):
                problem(g * per_group + c, True)
        return carry

    lax.fori_loop(0, n // per_group, fix, 0)


def _select_experts(tn, scr, outs, miss_scr):
    s_scr, rank_scr, top_scr = scr
    rank2_ref, e2_ref, c1_ref, e1_ref = outs
    nch = tn // LANES
    key_ids = lax.broadcasted_iota(jnp.int32, (HEAD_DIM, LANES), 0).astype(F32)

    def half_problem(it, exact):
        hp = it // nch
        lanes = pl.ds(pl.multiple_of((it % nch) * LANES, LANES), LANES)
        s = s_scr[hp, :, lanes]
        miss = None
        if exact:
            rank, tops = _topk_ranks(s, key_ids, TOPK)
            top = jnp.concatenate(tops, axis=0)
        else:
            keys, tops = _topk_keys(_sort_key(s), TOPK)
            sel, miss = _selected(keys, TOPK)
            rank = jnp.where(sel, (keys ^ jnp.int32(INT_MIN)).astype(F32), float(TOPK))
            top = _key_value(jnp.concatenate(tops, axis=0))
        rank_scr[hp, :, lanes] = rank
        top_scr[hp, :, lanes] = top
        return miss

    _run_with_tie_fallback(2 * HEADS * nch, nch, half_problem, miss_scr)

    row16 = lax.broadcasted_iota(jnp.int32, (16, LANES), 0)
    row8 = lax.broadcasted_iota(jnp.int32, (8, LANES), 0)
    r16 = row16.astype(F32)
    r8 = row8.astype(F32)
    groups_valid = [row16 >= 0, row8 >= 0, row16 >= 2, row8 >= 2,
                    (row8 >= 2) & (row8 <= 4), (row8 >= 2) & (row8 <= 3), row8 == 2]
    groups_ids = [r16, 16.0 + r8, r16 * 16.0, r8 * 16.0 + 1.0, 32.0 + r8, 48.0 + r8, 64.0 + r8]
    cand_ids = jnp.concatenate(
        [jnp.where(v, i, 1e9) for v, i in zip(groups_valid, groups_ids)], axis=0)

    def head_problem(it, exact):
        h = it // nch
        lanes = pl.ds(pl.multiple_of((it % nch) * LANES, LANES), LANES)
        v1 = top_scr[2 * h, :, lanes]
        v2 = top_scr[2 * h + 1, :, lanes]
        groups = [v1[0:1] + v2, v1[1:2] + v2[0:8], v1 + v2[0:1], v1[0:8] + v2[1:2],
                  v1[2:3] + v2[0:8], v1[3:4] + v2[0:8], v1[4:5] + v2[0:8]]
        cand = jnp.concatenate(
            [jnp.where(v, g, NEG_INF) for v, g in zip(groups_valid, groups)], axis=0)
        miss = None
        if exact:
            crank, _ = _topk_ranks(cand, cand_ids, TOPK)
            sel = crank < float(TOPK)
        else:
            ckeys, _ = _topk_keys(_sort_key(cand), TOPK)
            sel, miss = _selected(ckeys, TOPK)
        z = jnp.sum(jnp.where(sel, jnp.exp(jnp.where(sel, cand - cand[0:1], 0.0)), 0.0),
                    axis=0, keepdims=True)
        self32 = jnp.where(sel, 1.0, 0.0)
        cnt = lambda a, b: jnp.sum(self32[a:b], axis=0, keepdims=True)
        per_r1 = (self32[24:40]
                  + jnp.concatenate([self32[40:48], jnp.zeros((8, LANES), F32)], axis=0)
                  + jnp.where(row16 == 0, cnt(0, 16), 0.0)
                  + jnp.where(row16 == 1, cnt(16, 24), 0.0)
                  + jnp.where(row16 == 2, cnt(48, 56), 0.0)
                  + jnp.where(row16 == 3, cnt(56, 64), 0.0)
                  + jnp.where(row16 == 4, cnt(64, 72), 0.0))
        rank1 = rank_scr[2 * h, :, lanes]
        rank2 = rank_scr[2 * h + 1, :, lanes]
        c1 = jnp.zeros((HEAD_DIM, LANES), F32)
        for j in range(1, TOPK // 3 + 1):
            reach = jnp.sum(jnp.where(per_r1 >= float(j), 1.0, 0.0), axis=0, keepdims=True)
            c1 = c1 + jnp.where(rank1 < reach, 1.0, 0.0)
        c1 = jnp.where(rank1 == 0.0, per_r1[0:1], jnp.where(rank1 == 1.0, per_r1[1:2], c1))
        s1 = s_scr[2 * h, :, lanes]
        s2 = s_scr[2 * h + 1, :, lanes]
        in1 = rank1 < float(TOPK)
        in2 = rank2 < float(TOPK)
        e1 = jnp.where(in1, jnp.exp(jnp.where(in1, s1 - v1[0:1], 0.0)), 0.0)
        e2 = jnp.where(in2, jnp.exp(jnp.where(in2, s2 - v2[0:1], 0.0)), 0.0) / z
        rank2_ref[h, :, lanes] = rank2.astype(BF16)
        e2_ref[h, :, lanes] = (0.5 * e2).astype(BF16)
        c1_ref[h, :, lanes] = _dup_bf16(c1)
        e1_ref[h, :, lanes] = _dup_bf16(e1)
        return miss

    _run_with_tie_fallback(HEADS * nch, nch, head_problem, miss_scr)


def _router(x, g, wq_t, keys, tn):
    N, D = x.shape
    assert N % tn == 0 and tn % LANES == 0
    const = lambda shape: pl.BlockSpec(shape, lambda i: (0,) * len(shape))
    gate_blk = pl.BlockSpec((HEADS, HEAD_DIM, tn), lambda i: (0, 0, i))
    gate_shape = lambda dt: jax.ShapeDtypeStruct((HEADS, HEAD_DIM, N), dt)
    return pl.pallas_call(
        _router_kernel,
        grid=(N // tn,),
        in_specs=[pl.BlockSpec((tn, D), lambda i: (i, 0)), const((1, D)),
                  const(wq_t.shape), const(keys.shape)],
        out_specs=[pl.BlockSpec((D, tn), lambda i: (0, i)),
                   gate_blk, gate_blk, gate_blk, gate_blk],
        out_shape=[jax.ShapeDtypeStruct((D, N), BF16),
                   gate_shape(BF16), gate_shape(BF16), gate_shape(jnp.int32),
                   gate_shape(jnp.int32)],
        scratch_shapes=[pltpu.VMEM((2 * HEADS, HEAD_DIM, tn), F32),
                        pltpu.VMEM((2 * HEADS, HEAD_DIM, tn), F32),
                        pltpu.VMEM((2 * HEADS, TOPK, tn), F32),
                        pltpu.VMEM((2 * HEADS * (tn // LANES), LANES), F32)],
        compiler_params=pltpu.CompilerParams(
            dimension_semantics=("arbitrary",), vmem_limit_bytes=VMEM_LIMIT),
        name="peer_router",
    )(x, g, wq_t, keys)


ROW_TILE = 16
MXU_TILE = 256


def _peer_dense_kernel(xnt_ref, u_ref, vt_ref, rank2_ref, e2_ref, c1_ref, e1_ref, x_ref, gfin_ref,
                       y_ref, hid0, hid1, g_scr, acc_scr, key2_scr, *, final_norm):
    t = pl.program_id(1)
    te, tn = hid0.shape
    nb = te // HEAD_DIM

    def key2_rows(ch, rt, slot):
        tile = (ch * (HEAD_DIM // ROW_TILE) + rt) * 2 * HEADS + slot
        return slice(tile * ROW_TILE, (tile + 1) * ROW_TILE)

    @pl.when(t == 0)
    def _():
        for ref in (hid0, hid1, acc_scr):
            ref[...] = jnp.zeros_like(ref)
        for ch in range(tn // LANES):
            lanes = slice(ch * LANES, (ch + 1) * LANES)
            for rt in range(HEAD_DIM // ROW_TILE):
                keys2 = slice(rt * ROW_TILE, (rt + 1) * ROW_TILE)
                for h in range(HEADS):
                    key2_scr[key2_rows(ch, rt, h), :] = rank2_ref[h, keys2, lanes]
                    key2_scr[key2_rows(ch, rt, HEADS + h), :] = e2_ref[h, keys2, lanes]

    def gate_group(hid_cur, i1, il, ch):
        lanes = slice(ch * LANES, (ch + 1) * LANES)
        bcast = lambda ref, h: pltpu.bitcast(
            jnp.broadcast_to(ref[h, i1:i1 + 1, lanes], (8, LANES)), BF16)
        c1 = [bcast(c1_ref, h) for h in range(HEADS)]
        e1 = [bcast(e1_ref, h) for h in range(HEADS)]
        for rt in range(HEAD_DIM // ROW_TILE):
            r0 = rt * ROW_TILE
            rows = slice(il * HEAD_DIM + r0, il * HEAD_DIM + r0 + ROW_TILE)
            w = jnp.zeros((ROW_TILE, LANES), BF16)
            for h in range(HEADS):
                w = w + jnp.where(key2_scr[key2_rows(ch, rt, h), :] < c1[h],
                                  key2_scr[key2_rows(ch, rt, HEADS + h), :] * e1[h],
                                  jnp.zeros_like(w))
            hid = hid_cur[rows, lanes]
            gelu2 = hid * (1.0 + lax.erf(hid * 0.7071067811865476))
            g_scr[rows, lanes] = w * gelu2.astype(BF16)

    def stage(hid_new, hid_cur, i1_base):
        width = min(MXU_TILE, tn)
        depth = min(MXU_TILE, te)
        for k0 in range(0, te, depth):
            for n0 in range(0, tn, width):
                hid_new[k0:k0 + depth, n0:n0 + width] = _dot(u_ref[k0:k0 + depth, :],
                                                             xnt_ref[:, n0:n0 + width])
                for il in range(k0 // HEAD_DIM, (k0 + depth) // HEAD_DIM):
                    for ch in range(n0 // LANES, (n0 + width) // LANES):
                        gate_group(hid_cur, i1_base + il, il, ch)
                acc_scr[:, n0:n0 + width] += _dot(vt_ref[0, :, k0:k0 + depth],
                                                  g_scr[k0:k0 + depth, n0:n0 + width])

    group = c1_ref.shape[1]

    @pl.when(t % 2 == 0)
    def _():
        stage(hid0, hid1, nb % group)

    @pl.when(t % 2 == 1)
    def _():
        stage(hid1, hid0, 0)

    @pl.when(t == pl.num_programs(1) - 1)
    def _():
        y = x_ref[...] + acc_scr[...].T
        if final_norm:
            y = _rms(y, gfin_ref[...])
        y_ref[...] = y


def _peer_dense(xnt, u, vt, rank2, e2, c1, e1, x, gfin, tn, final_norm):
    N, D = x.shape
    nblk, _, te = vt.shape
    assert N % tn == 0 and u.shape[0] == nblk * te and te % HEAD_DIM == 0
    nb = te // HEAD_DIM
    group = max(8, nb)
    assert group // nb <= 2 and HEAD_DIM % group == 0
    key2_blk = pl.BlockSpec((HEADS, HEAD_DIM, tn), lambda i, t: (0, 0, i))
    key1_blk = pl.BlockSpec(
        (HEADS, group, tn),
        lambda i, t: (0, jnp.clip(((t - 1) * nb) // group, 0, HEAD_DIM // group - 1), i))
    return pl.pallas_call(
        functools.partial(_peer_dense_kernel, final_norm=final_norm),
        grid=(N // tn, nblk + 1),
        in_specs=[pl.BlockSpec((D, tn), lambda i, t: (0, i)),
                  pl.BlockSpec((te, D), lambda i, t: (jnp.minimum(t, nblk - 1), 0)),
                  pl.BlockSpec((1, D, te), lambda i, t: (jnp.maximum(t - 1, 0), 0, 0)),
                  key2_blk, key2_blk, key1_blk, key1_blk,
                  pl.BlockSpec((tn, D), lambda i, t: (i, 0)),
                  pl.BlockSpec((1, D), lambda i, t: (0, 0))],
        out_specs=pl.BlockSpec((tn, D), lambda i, t: (i, 0)),
        out_shape=jax.ShapeDtypeStruct((N, D), F32),
        scratch_shapes=[pltpu.VMEM((te, tn), F32), pltpu.VMEM((te, tn), F32),
                        pltpu.VMEM((te, tn), BF16),
                        pltpu.VMEM((D, tn), F32),
                        pltpu.VMEM((2 * HEADS * HEAD_DIM * (tn // LANES), LANES), BF16)],
        compiler_params=pltpu.CompilerParams(
            dimension_semantics=("arbitrary", "arbitrary"), vmem_limit_bytes=VMEM_LIMIT),
        name="peer_dense",
    )(xnt, u, vt, rank2, e2, c1, e1, x, gfin)


def _peer(x, g, wq_t, keys, u, vt, gfin, tn_router, tn_dense, final_norm):
    xnt, rank2, e2, c1, e1 = _router(x, g, wq_t, keys, tn_router)
    return _peer_dense(xnt, u, vt, rank2, e2, c1, e1, x, gfin, tn_dense, final_norm)


HGRN_CHUNK = 64
HGRN_SEQS = 2
CONV_CHUNK = 256
CONV_SAMPLE_BLOCK = 32
PEER_ROUTER_TOKENS = 512
PEER_DENSE_TOKENS = 512
PEER_EXPERTS = 512


def kernel(x_prompt, x_sample, state_hgrn, state_conv, norm_mix, norm_ffn, norm_final, hg_w_in, hg_w_out, hg_gnorm, hg_lb_logits, cv_w_pw1, cv_b_pw1, cv_w_dw, cv_b_dw, cv_ln_g, cv_ln_b, cv_w_pw2, cv_b_pw2, peer_w_q, peer_sub_keys, peer_u, peer_v):
    B, T, D = x_prompt.shape
    NS = x_sample.shape[0]
    depth = norm_mix.shape[0]
    row = lambda a: a.reshape(1, -1)

    lb_all = jnp.cumsum(jax.nn.softmax(hg_lb_logits.astype(F32), axis=0), axis=0)

    xp = x_prompt
    xs = x_sample.reshape(NS, D)
    hg_p, hg_s, cv_p, cv_s = [], [], [], []
    for i in range(depth):
        j = i // 2
        gmix = row(norm_mix[i])
        if i % 2 == 0:
            w_in = hg_w_in[j].astype(BF16)
            w_out = hg_w_out[j].astype(BF16)
            gn = row(hg_gnorm[j])
            lb = row(lb_all[i])
            xp, sp = _hgrn_prompt(xp, gmix, w_in, w_out, gn, lb, HGRN_CHUNK, HGRN_SEQS)
            xs, ss = _hgrn_sample(xs, gmix, w_in, w_out, gn, lb, state_hgrn[j])
            hg_p.append(sp)
            hg_s.append(ss)
        else:
            cw = (cv_w_pw1[j].astype(BF16), row(cv_b_pw1[j]), cv_w_dw[j], row(cv_b_dw[j]),
                  row(cv_ln_g[j]), row(cv_ln_b[j]), cv_w_pw2[j].astype(BF16), row(cv_b_pw2[j]))
            xp, bp = _conv_prompt(xp, gmix, *cw, CONV_CHUNK)
            xs, bs = _conv_sample(xs, gmix, *cw, state_conv[j], CONV_SAMPLE_BLOCK)
            cv_p.append(bp)
            cv_s.append(bs)
        wq_t = peer_w_q[i].T.astype(BF16)
        keys = peer_sub_keys[i].reshape(2 * HEADS, HEAD_DIM, HEAD_DIM).astype(BF16)
        u = peer_u[i].astype(BF16)
        vt = peer_v[i].reshape(-1, PEER_EXPERTS, D).transpose(0, 2, 1).astype(BF16)
        gffn = row(norm_ffn[i])
        gfin = row(norm_final)
        last = i == depth - 1
        xp = _peer(xp.reshape(B * T, D), gffn, wq_t, keys, u, vt, gfin,
                   min(PEER_ROUTER_TOKENS, B * T), min(PEER_DENSE_TOKENS, B * T),
                   last).reshape(B, T, D)
        xs = _peer(xs, gffn, wq_t, keys, u, vt, gfin, min(LANES, NS), min(LANES, NS), last)
    return (xp, xs.reshape(NS, 1, D), jnp.stack(hg_p), jnp.stack(hg_s),
            jnp.stack(cv_p), jnp.stack(cv_s))
```

```python
import functools

import jax
import jax.numpy as jnp
from jax import lax
from jax.experimental import pallas as pl
from jax.experimental.pallas import tpu as pltpu

F32 = jnp.float32
BF16 = jnp.bfloat16
EPS = 1e-6

HEADS = 8
HEAD_DIM = 128
TOPK = 16
CONV_WIDTH = 31
LANES = 128
SUB = 16
VMEM_LIMIT = 56 * 1024 * 1024

NEG_INF = float("-inf")


def _rms(x, g):
    return x * lax.rsqrt(jnp.mean(x * x, axis=-1, keepdims=True) + EPS) * g


def _sigmoid(x):
    return 1.0 / (1.0 + jnp.exp(-x))


def _dot(a, b):
    return jnp.dot(a, b, preferred_element_type=F32)


def _dot_nt(a, b):
    return lax.dot_general(a, b, (((1,), (1,)), ((), ())), preferred_element_type=F32)


def _hgrn_prompt_kernel(x_ref, gmix_ref, win_ref, wout_ref, gn_ref, lb_ref, tril_ref, ones_ref,
                        y_ref, s_ref,
                        st_scr, q_scr, k_scr, v_scr, b_scr, p_scr, a_scr, o_scr):
    R, C, D = x_ref.shape
    M = R * C
    nsub = C // SUB
    ci = pl.program_id(1)

    @pl.when(ci == 0)
    def _():
        st_scr[...] = jnp.zeros_like(st_scr)

    xc = x_ref[...].reshape(M, D)
    hn = _rms(xc, gmix_ref[...])
    proj = _dot(hn.astype(BF16), win_ref[...])
    qz = proj[:, 0:D]
    fz = proj[:, D:2 * D]
    gz = proj[:, 3 * D:4 * D]
    lb = lb_ref[...]
    f = lb + (1.0 - lb) * _sigmoid(fz)
    q_scr[...] = qz * _sigmoid(qz)
    k_scr[...] = 1.0 - f
    v_scr[...] = proj[:, 2 * D:3 * D]
    lf = jnp.log(f)
    hi = lf.astype(BF16)
    rem = lf - hi.astype(F32)
    mid = rem.astype(BF16)
    lo = (rem - mid.astype(F32)).astype(BF16)
    tril = tril_ref[...]
    b_scr[...] = _dot(tril, hi) + _dot(tril, mid) + _dot(tril, lo)

    qe = (q_scr[...] * jnp.exp(b_scr[...])).astype(BF16)
    for r in range(R):
        for h in range(HEADS):
            L = slice(h * HEAD_DIM, (h + 1) * HEAD_DIM)
            o_scr[r * C:(r + 1) * C, L] = _dot_nt(qe[r * C:(r + 1) * C, L],
                                                  st_scr[r * HEADS + h].astype(BF16))

    t_iota = lax.broadcasted_iota(jnp.int32, (SUB, D), 0)
    for r in range(R):
        base = r * C
        for I in range(nsub):
            rows = slice(base + SUB * I, base + SUB * (I + 1))
            prev = slice(base, base + SUB * I)
            bI = b_scr[rows, :]
            qI = q_scr[rows, :]
            if I > 0:
                beta = b_scr[base + SUB * I - 1:base + SUB * I, :]
                qs = (qI * jnp.exp(bI - beta)).astype(BF16)
                ks = (k_scr[prev, :] * jnp.exp(beta - b_scr[prev, :])).astype(BF16)
                vs = v_scr[prev, :].astype(BF16)
                for h in range(HEADS):
                    L = slice(h * HEAD_DIM, (h + 1) * HEAD_DIM)
                    att = _dot_nt(qs[:, L], ks[:, L])
                    o_scr[rows, L] += _dot(att.astype(BF16), vs[:, L])
            for s in range(SUB):
                src = base + SUB * I + s
                keep = t_iota >= s
                dec = jnp.exp(jnp.where(keep, bI - b_scr[src:src + 1, :], 0.0))
                p = jnp.where(keep, qI * dec * k_scr[src:src + 1, :], 0.0)
                p_scr[src * SUB:(src + 1) * SUB, :] = p.astype(BF16)

    for h in range(HEADS):
        L = slice(h * HEAD_DIM, (h + 1) * HEAD_DIM)
        a_scr[:, L] = _dot(p_scr[:, L], ones_ref[...])
    for J in range(M // SUB):
        rows = slice(SUB * J, SUB * (J + 1))
        acc = o_scr[rows, :]
        for s in range(SUB):
            src = SUB * J + s
            acc = acc + a_scr[src * SUB:(src + 1) * SUB, :] * v_scr[src:src + 1, :]
        o_scr[rows, :] = acc

    for r in range(R):
        rows = slice(r * C, (r + 1) * C)
        b_last = b_scr[(r + 1) * C - 1:(r + 1) * C, :]
        kd = (k_scr[rows, :] * jnp.exp(b_last - b_scr[rows, :])).astype(BF16)
        eb_last = jnp.exp(b_last)
        for h in range(HEADS):
            L = slice(h * HEAD_DIM, (h + 1) * HEAD_DIM)
            vt = v_scr[rows, L].T.astype(BF16)
            st_scr[r * HEADS + h] = st_scr[r * HEADS + h] * eb_last[:, L] + _dot(vt, kd[:, L])

    gn = gn_ref[...]
    for h in range(HEADS):
        L = slice(h * HEAD_DIM, (h + 1) * HEAD_DIM)
        oh = o_scr[:, L]
        gzh = gz[:, L]
        o_scr[:, L] = _rms(oh, gn) * (gzh * _sigmoid(gzh))
    y_ref[...] = (xc + _dot(o_scr[...].astype(BF16), wout_ref[...])).reshape(R, C, D)

    @pl.when(ci == pl.num_programs(1) - 1)
    def _():
        for r in range(R):
            for h in range(HEADS):
                s_ref[r, h] = st_scr[r * HEADS + h].T


def _hgrn_prompt(x, gmix, w_in, w_out, gnorm, lb, chunk, seqs):
    B, T, D = x.shape
    C = min(chunk, T)
    R = min(seqs, B)
    assert T % C == 0 and C % SUB == 0 and B % R == 0
    M = R * C
    tril = jnp.kron(jnp.eye(R, dtype=F32), jnp.tril(jnp.ones((C, C), F32))).astype(BF16)
    ones = jnp.ones((HEAD_DIM, HEAD_DIM), BF16)
    const = lambda shape: pl.BlockSpec(shape, lambda b, c: (0,) * len(shape))
    return pl.pallas_call(
        _hgrn_prompt_kernel,
        grid=(B // R, T // C),
        in_specs=[
            pl.BlockSpec((R, C, D), lambda b, c: (b, c, 0)),
            const((1, D)), const((D, 4 * D)), const((D, D)), const((1, HEAD_DIM)), const((1, D)),
            const((M, M)), const((HEAD_DIM, HEAD_DIM)),
        ],
        out_specs=[
            pl.BlockSpec((R, C, D), lambda b, c: (b, c, 0)),
            pl.BlockSpec((R, HEADS, HEAD_DIM, HEAD_DIM), lambda b, c: (b, 0, 0, 0)),
        ],
        out_shape=[
            jax.ShapeDtypeStruct((B, T, D), F32),
            jax.ShapeDtypeStruct((B, HEADS, HEAD_DIM, HEAD_DIM), F32),
        ],
        scratch_shapes=[
            pltpu.VMEM((R * HEADS, HEAD_DIM, HEAD_DIM), F32),
            pltpu.VMEM((M, D), F32),
            pltpu.VMEM((M, D), F32),
            pltpu.VMEM((M, D), F32),
            pltpu.VMEM((M, D), F32),
            pltpu.VMEM((M * SUB, D), BF16),
            pltpu.VMEM((M * SUB, D), F32),
            pltpu.VMEM((M, D), F32),
        ],
        compiler_params=pltpu.CompilerParams(
            dimension_semantics=("arbitrary", "arbitrary"), vmem_limit_bytes=VMEM_LIMIT),
        name="hgrn_prompt",
    )(x, gmix, w_in, w_out, gnorm, lb, tril, ones)


def _hgrn_sample_kernel(x_ref, gmix_ref, win_ref, wout_ref, gn_ref, lb_ref, s_ref,
                        y_ref, snew_ref, proj_scr, o_scr):
    n = pl.program_id(0)
    D = x_ref.shape[1]

    @pl.when(n == 0)
    def _():
        hn = _rms(x_ref[...], gmix_ref[...])
        proj_scr[...] = _dot(hn.astype(BF16), win_ref[...])

    row = proj_scr[pl.ds(n, 1), :]
    qz = row[:, 0:D]
    fz = row[:, D:2 * D]
    v = row[:, 2 * D:3 * D]
    lb = lb_ref[...]
    f = lb + (1.0 - lb) * _sigmoid(fz)
    q = qz * _sigmoid(qz)
    k = 1.0 - f

    def cols(r):
        stacked = jnp.concatenate(
            [r[:, h * HEAD_DIM:(h + 1) * HEAD_DIM] for h in range(HEADS)]
            + [jnp.zeros((HEAD_DIM - HEADS, HEAD_DIM), F32)], axis=0)
        return stacked.T

    fc = cols(f)
    kc = cols(k)
    qc = cols(q)
    outs = []
    for h in range(HEADS):
        L = slice(h * HEAD_DIM, (h + 1) * HEAD_DIM)
        s_new = s_ref[0, h] * fc[:, h:h + 1] + kc[:, h:h + 1] * v[:, L]
        snew_ref[0, h] = s_new
        outs.append(jnp.sum(qc[:, h:h + 1] * s_new, axis=0, keepdims=True))
    o_scr[pl.ds(n, 1), :] = jnp.concatenate(outs, axis=1)

    @pl.when(n == pl.num_programs(0) - 1)
    def _():
        gn = gn_ref[...]
        gz = proj_scr[:, 3 * D:4 * D]
        for h in range(HEADS):
            L = slice(h * HEAD_DIM, (h + 1) * HEAD_DIM)
            gzh = gz[:, L]
            o_scr[:, L] = _rms(o_scr[:, L], gn) * (gzh * _sigmoid(gzh))
        y_ref[...] = x_ref[...] + _dot(o_scr[...].astype(BF16), wout_ref[...])


def _hgrn_sample(x, gmix, w_in, w_out, gnorm, lb, state):
    N, D = x.shape
    const = lambda shape: pl.BlockSpec(shape, lambda n: (0,) * len(shape))
    sblk = pl.BlockSpec((1, HEADS, HEAD_DIM, HEAD_DIM), lambda n: (n, 0, 0, 0))
    return pl.pallas_call(
        _hgrn_sample_kernel,
        grid=(N,),
        in_specs=[const((N, D)), const((1, D)), const((D, 4 * D)), const((D, D)),
                  const((1, HEAD_DIM)), const((1, D)), sblk],
        out_specs=[const((N, D)), sblk],
        out_shape=[jax.ShapeDtypeStruct((N, D), F32),
                   jax.ShapeDtypeStruct((N, HEADS, HEAD_DIM, HEAD_DIM), F32)],
        scratch_shapes=[pltpu.VMEM((N, 4 * D), F32), pltpu.VMEM((N, D), F32)],
        compiler_params=pltpu.CompilerParams(
            dimension_semantics=("arbitrary",), vmem_limit_bytes=VMEM_LIMIT),
        name="hgrn_sample",
    )(x, gmix, w_in, w_out, gnorm, lb, state)


CONV_PAD = 32


def _conv_tail(y, xc, bdw_ref, lng_ref, lnb_ref, pw2_ref, bpw2_ref):
    y = y + bdw_ref[...]
    mu = jnp.mean(y, axis=-1, keepdims=True)
    yc = y - mu
    var = jnp.mean(yc * yc, axis=-1, keepdims=True)
    yn = yc * lax.rsqrt(var + EPS) * lng_ref[...] + lnb_ref[...]
    z = yn * _sigmoid(yn)
    return xc + _dot(z.astype(BF16), pw2_ref[...]) + bpw2_ref[...]


def _glu(xc, gmix_ref, pw1_ref, bpw1_ref):
    D = xc.shape[1]
    hn = _rms(xc, gmix_ref[...])
    h2 = _dot(hn.astype(BF16), pw1_ref[...]) + bpw1_ref[...]
    return h2[:, 0:D] * _sigmoid(h2[:, D:2 * D])


def _conv_prompt_kernel(x_ref, gmix_ref, pw1_ref, bpw1_ref, wdw_ref, bdw_ref, lng_ref, lnb_ref,
                        pw2_ref, bpw2_ref, y_ref, buf_ref, full_scr):
    C = x_ref.shape[1]
    ci = pl.program_id(1)
    hist = CONV_WIDTH - 1
    off = CONV_PAD - hist

    @pl.when(ci == 0)
    def _():
        full_scr[0:CONV_PAD, :] = jnp.zeros((CONV_PAD, full_scr.shape[1]), F32)

    xc = x_ref[0]
    full_scr[CONV_PAD:CONV_PAD + C, :] = _glu(xc, gmix_ref, pw1_ref, bpw1_ref)
    y = full_scr[off:off + C, :] * wdw_ref[0:1, :]
    for j in range(1, CONV_WIDTH):
        y = y + full_scr[off + j:off + j + C, :] * wdw_ref[j:j + 1, :]
    y_ref[0] = _conv_tail(y, xc, bdw_ref, lng_ref, lnb_ref, pw2_ref, bpw2_ref)

    @pl.when(ci == pl.num_programs(1) - 1)
    def _():
        buf_ref[0] = full_scr[C + off:C + CONV_PAD, :]

    full_scr[0:CONV_PAD, :] = full_scr[C:C + CONV_PAD, :]


def _conv_prompt(x, gmix, pw1, bpw1, wdw, bdw, lng, lnb, pw2, bpw2, chunk):
    B, T, D = x.shape
    C = min(chunk, T)
    assert T % C == 0 and C >= CONV_PAD
    hist = CONV_WIDTH - 1
    const = lambda shape: pl.BlockSpec(shape, lambda b, c: (0,) * len(shape))
    return pl.pallas_call(
        _conv_prompt_kernel,
        grid=(B, T // C),
        in_specs=[pl.BlockSpec((1, C, D), lambda b, c: (b, c, 0)),
                  const((1, D)), const((D, 2 * D)), const((1, 2 * D)), const((CONV_WIDTH, D)),
                  const((1, D)), const((1, D)), const((1, D)), const((D, D)), const((1, D))],
        out_specs=[pl.BlockSpec((1, C, D), lambda b, c: (b, c, 0)),
                   pl.BlockSpec((1, hist, D), lambda b, c: (b, 0, 0))],
        out_shape=[jax.ShapeDtypeStruct((B, T, D), F32),
                   jax.ShapeDtypeStruct((B, hist, D), F32)],
        scratch_shapes=[pltpu.VMEM((CONV_PAD + C, D), F32)],
        compiler_params=pltpu.CompilerParams(
            dimension_semantics=("arbitrary", "arbitrary"), vmem_limit_bytes=VMEM_LIMIT),
        name="conv_prompt",
    )(x, gmix, pw1, bpw1, wdw, bdw, lng, lnb, pw2, bpw2)


def _conv_sample_kernel(x_ref, gmix_ref, pw1_ref, bpw1_ref, wdw_ref, bdw_ref, lng_ref, lnb_ref,
                        pw2_ref, bpw2_ref, buf_ref, y_ref, nbuf_ref, u_scr, y_scr):
    nt = x_ref.shape[0]
    hist = CONV_WIDTH - 1
    xc = x_ref[...]
    u_scr[...] = _glu(xc, gmix_ref, pw1_ref, bpw1_ref)
    w_hist = wdw_ref[0:hist, :]
    w_last = wdw_ref[hist:hist + 1, :]

    def body(n, carry):
        buf = buf_ref[n]
        u = u_scr[pl.ds(n, 1), :]
        y_scr[pl.ds(n, 1), :] = jnp.sum(buf * w_hist, axis=0, keepdims=True) + u * w_last
        nbuf_ref[n, 0:hist - 1, :] = buf[1:hist, :]
        nbuf_ref[n, hist - 1:hist, :] = u
        return carry

    lax.fori_loop(0, nt, body, 0)
    y_ref[...] = _conv_tail(y_scr[...], xc, bdw_ref, lng_ref, lnb_ref, pw2_ref, bpw2_ref)


def _conv_sample(x, gmix, pw1, bpw1, wdw, bdw, lng, lnb, pw2, bpw2, buf, block):
    N, D = x.shape
    nt = min(block, N)
    assert N % nt == 0
    hist = CONV_WIDTH - 1
    const = lambda shape: pl.BlockSpec(shape, lambda i: (0,) * len(shape))
    return pl.pallas_call(
        _conv_sample_kernel,
        grid=(N // nt,),
        in_specs=[pl.BlockSpec((nt, D), lambda i: (i, 0)),
                  const((1, D)), const((D, 2 * D)), const((1, 2 * D)), const((CONV_WIDTH, D)),
                  const((1, D)), const((1, D)), const((1, D)), const((D, D)), const((1, D)),
                  pl.BlockSpec((nt, hist, D), lambda i: (i, 0, 0))],
        out_specs=[pl.BlockSpec((nt, D), lambda i: (i, 0)),
                   pl.BlockSpec((nt, hist, D), lambda i: (i, 0, 0))],
        out_shape=[jax.ShapeDtypeStruct((N, D), F32),
                   jax.ShapeDtypeStruct((N, hist, D), F32)],
        scratch_shapes=[pltpu.VMEM((nt, D), F32), pltpu.VMEM((nt, D), F32)],
        compiler_params=pltpu.CompilerParams(
            dimension_semantics=("arbitrary",), vmem_limit_bytes=VMEM_LIMIT),
        name="conv_sample",
    )(x, gmix, pw1, bpw1, wdw, bdw, lng, lnb, pw2, bpw2, buf)


def _topk_ranks(vals, ids, k):
    rank = jnp.full(vals.shape, float(k), F32)
    tops = []
    for r in range(k):
        m = jnp.max(vals, axis=0, keepdims=True)
        first = jnp.min(jnp.where(vals == m, ids, 1e9), axis=0, keepdims=True)
        sel = ids == first
        vals = jnp.where(sel, NEG_INF, vals)
        rank = jnp.where(sel, float(r), rank)
        tops.append(m)
    return rank, tops


INT_MIN = -2 ** 31


def _sort_key(s):
    b = lax.bitcast_convert_type(jnp.where(s == 0.0, 0.0, s), jnp.int32)
    return b ^ (lax.shift_right_arithmetic(b, 31) & jnp.int32(0x7FFFFFFF))


def _key_value(k):
    b = k ^ (lax.shift_right_arithmetic(k, 31) & jnp.int32(0x7FFFFFFF))
    return lax.bitcast_convert_type(b, F32)


def _topk_keys(keys, k):
    tops = []
    for r in range(k):
        m = jnp.max(keys, axis=0, keepdims=True)
        keys = jnp.where(keys == m, jnp.int32(INT_MIN + r), keys)
        tops.append(m)
    return keys, tops


def _selected(keys, k):
    sel = keys < jnp.int32(INT_MIN + k)
    n = jnp.sum(jnp.where(sel, 1.0, 0.0), axis=0, keepdims=True)
    return sel, jnp.where(n != float(k), 1.0, 0.0)


def _dup_bf16(x):
    u = lax.bitcast_convert_type(x.astype(BF16).astype(F32), jnp.int32)
    return u | lax.shift_right_logical(u, 16)


def _router_kernel(x_ref, g_ref, wq_ref, keys_ref,
                   xnt_ref, rank2_ref, e2_ref, c1_ref, e1_ref,
                   s_scr, rank_scr, top_scr, miss_scr):
    tn = x_ref.shape[0]
    xn = _rms(x_ref[...], g_ref[...])
    xnt = xn.T.astype(BF16)
    xnt_ref[...] = xnt
    qt = _dot(wq_ref[...], xnt).astype(BF16)
    for hp in range(2 * HEADS):
        s_scr[hp] = _dot(keys_ref[hp], qt[hp * HEAD_DIM:(hp + 1) * HEAD_DIM, :])

    outs = (rank2_ref, e2_ref, c1_ref, e1_ref)
    scr = (s_scr, rank_scr, top_scr)
    _select_experts(tn, scr, outs, miss_scr)


def _run_with_tie_fallback(n, per_group, problem, miss_scr):
    def fast(g, carry):
        miss = [problem(g * per_group + c, False) for c in range(per_group)]
        miss_scr[pl.ds(g, 1), :] = functools.reduce(jnp.maximum, miss)
        return carry

    lax.fori_loop(0, n // per_group, fast, 0, unroll=2 if per_group == 1 else 1)

    def fix(g, carry):
        @pl.when(jnp.max(miss_scr[pl.ds(g, 1), :]) > 0.0)
        def _():
            for c in range(per_group):
                problem(g * per_group + c, True)
        return carry

    lax.fori_loop(0, n // per_group, fix, 0)


def _select_experts(tn, scr, outs, miss_scr):
    s_scr, rank_scr, top_scr = scr
    rank2_ref, e2_ref, c1_ref, e1_ref = outs
    nch = tn // LANES
    key_ids = lax.broadcasted_iota(jnp.int32, (HEAD_DIM, LANES), 0).astype(F32)

    def half_problem(it, exact):
        hp = it // nch
        lanes = pl.ds(pl.multiple_of((it % nch) * LANES, LANES), LANES)
        s = s_scr[hp, :, lanes]
        miss = None
        if exact:
            rank, tops = _topk_ranks(s, key_ids, TOPK)
            top = jnp.concatenate(tops, axis=0)
        else:
            keys, tops = _topk_keys(_sort_key(s), TOPK)
            sel, miss = _selected(keys, TOPK)
            rank = jnp.where(sel, (keys ^ jnp.int32(INT_MIN)).astype(F32), float(TOPK))
            top = _key_value(jnp.concatenate(tops, axis=0))
        rank_scr[hp, :, lanes] = rank
        top_scr[hp, :, lanes] = top
        return miss

    _run_with_tie_fallback(2 * HEADS * nch, nch, half_problem, miss_scr)

    row16 = lax.broadcasted_iota(jnp.int32, (16, LANES), 0)
    row8 = lax.broadcasted_iota(jnp.int32, (8, LANES), 0)
    r16 = row16.astype(F32)
    r8 = row8.astype(F32)
    groups_valid = [row16 >= 0, row8 >= 0, row16 >= 2, row8 >= 2,
                    (row8 >= 2) & (row8 <= 4), (row8 >= 2) & (row8 <= 3), row8 == 2]
    groups_ids = [r16, 16.0 + r8, r16 * 16.0, r8 * 16.0 + 1.0, 32.0 + r8, 48.0 + r8, 64.0 + r8]
    cand_ids = jnp.concatenate(
        [jnp.where(v, i, 1e9) for v, i in zip(groups_valid, groups_ids)], axis=0)

    def head_problem(it, exact):
        h = it // nch
        lanes = pl.ds(pl.multiple_of((it % nch) * LANES, LANES), LANES)
        v1 = top_scr[2 * h, :, lanes]
        v2 = top_scr[2 * h + 1, :, lanes]
        groups = [v1[0:1] + v2, v1[1:2] + v2[0:8], v1 + v2[0:1], v1[0:8] + v2[1:2],
                  v1[2:3] + v2[0:8], v1[3:4] + v2[0:8], v1[4:5] + v2[0:8]]
        cand = jnp.concatenate(
            [jnp.where(v, g, NEG_INF) for v, g in zip(groups_valid, groups)], axis=0)
        miss = None
        if exact:
            crank, _ = _topk_ranks(cand, cand_ids, TOPK)
            sel = crank < float(TOPK)
        else:
            ckeys, _ = _topk_keys(_sort_key(cand), TOPK)
            sel, miss = _selected(ckeys, TOPK)
        z = jnp.sum(jnp.where(sel, jnp.exp(jnp.where(sel, cand - cand[0:1], 0.0)), 0.0),
                    axis=0, keepdims=True)
        self32 = jnp.where(sel, 1.0, 0.0)
        cnt = lambda a, b: jnp.sum(self32[a:b], axis=0, keepdims=True)
        per_r1 = (self32[24:40]
                  + jnp.concatenate([self32[40:48], jnp.zeros((8, LANES), F32)], axis=0)
                  + jnp.where(row16 == 0, cnt(0, 16), 0.0)
                  + jnp.where(row16 == 1, cnt(16, 24), 0.0)
                  + jnp.where(row16 == 2, cnt(48, 56), 0.0)
                  + jnp.where(row16 == 3, cnt(56, 64), 0.0)
                  + jnp.where(row16 == 4, cnt(64, 72), 0.0))
        rank1 = rank_scr[2 * h, :, lanes]
        rank2 = rank_scr[2 * h + 1, :, lanes]
        c1 = jnp.zeros((HEAD_DIM, LANES), F32)
        for j in range(1, TOPK // 3 + 1):
            reach = jnp.sum(jnp.where(per_r1 >= float(j), 1.0, 0.0), axis=0, keepdims=True)
            c1 = c1 + jnp.where(rank1 < reach, 1.0, 0.0)
        c1 = jnp.where(rank1 == 0.0, per_r1[0:1], jnp.where(rank1 == 1.0, per_r1[1:2], c1))
        s1 = s_scr[2 * h, :, lanes]
        s2 = s_scr[2 * h + 1, :, lanes]
        in1 = rank1 < float(TOPK)
        in2 = rank2 < float(TOPK)
        e1 = jnp.where(in1, jnp.exp(jnp.where(in1, s1 - v1[0:1], 0.0)), 0.0)
        e2 = jnp.where(in2, jnp.exp(jnp.where(in2, s2 - v2[0:1], 0.0)), 0.0) / z
        rank2_ref[h, :, lanes] = rank2.astype(BF16)
        e2_ref[h, :, lanes] = (0.5 * e2).astype(BF16)
        c1_ref[h, :, lanes] = _dup_bf16(c1)
        e1_ref[h, :, lanes] = _dup_bf16(e1)
        return miss

    _run_with_tie_fallback(HEADS * nch, nch, head_problem, miss_scr)


def _router(x, g, wq_t, keys, tn):
    N, D = x.shape
    assert N % tn == 0 and tn % LANES == 0
    const = lambda shape: pl.BlockSpec(shape, lambda i: (0,) * len(shape))
    gate_blk = pl.BlockSpec((HEADS, HEAD_DIM, tn), lambda i: (0, 0, i))
    gate_shape = lambda dt: jax.ShapeDtypeStruct((HEADS, HEAD_DIM, N), dt)
    return pl.pallas_call(
        _router_kernel,
        grid=(N // tn,),
        in_specs=[pl.BlockSpec((tn, D), lambda i: (i, 0)), const((1, D)),
                  const(wq_t.shape), const(keys.shape)],
        out_specs=[pl.BlockSpec((D, tn), lambda i: (0, i)),
                   gate_blk, gate_blk, gate_blk, gate_blk],
        out_shape=[jax.ShapeDtypeStruct((D, N), BF16),
                   gate_shape(BF16), gate_shape(BF16), gate_shape(jnp.int32),
                   gate_shape(jnp.int32)],
        scratch_shapes=[pltpu.VMEM((2 * HEADS, HEAD_DIM, tn), F32),
                        pltpu.VMEM((2 * HEADS, HEAD_DIM, tn), F32),
                        pltpu.VMEM((2 * HEADS, TOPK, tn), F32),
                        pltpu.VMEM((2 * HEADS * (tn // LANES), LANES), F32)],
        compiler_params=pltpu.CompilerParams(
            dimension_semantics=("arbitrary",), vmem_limit_bytes=VMEM_LIMIT),
        name="peer_router",
    )(x, g, wq_t, keys)


ROW_TILE = 16
MXU_TILE = 256


def _peer_dense_kernel(xnt_ref, u_ref, vt_ref, rank2_ref, e2_ref, c1_ref, e1_ref, x_ref, gfin_ref,
                       y_ref, hid0, hid1, g_scr, acc_scr, key2_scr, *, final_norm):
    t = pl.program_id(1)
    te, tn = hid0.shape
    nb = te // HEAD_DIM

    def key2_rows(ch, rt, slot):
        tile = (ch * (HEAD_DIM // ROW_TILE) + rt) * 2 * HEADS + slot
        return slice(tile * ROW_TILE, (tile + 1) * ROW_TILE)

    @pl.when(t == 0)
    def _():
        for ref in (hid0, hid1, acc_scr):
            ref[...] = jnp.zeros_like(ref)
        for ch in range(tn // LANES):
            lanes = slice(ch * LANES, (ch + 1) * LANES)
            for rt in range(HEAD_DIM // ROW_TILE):
                keys2 = slice(rt * ROW_TILE, (rt + 1) * ROW_TILE)
                for h in range(HEADS):
                    key2_scr[key2_rows(ch, rt, h), :] = rank2_ref[h, keys2, lanes]
                    key2_scr[key2_rows(ch, rt, HEADS + h), :] = e2_ref[h, keys2, lanes]

    def gate_tile(hid_cur, i1_base, ils, ch):
        lanes = slice(ch * LANES, (ch + 1) * LANES)
        bcast = lambda ref, h, il: pltpu.bitcast(
            jnp.broadcast_to(ref[h, i1_base + il:i1_base + il + 1, lanes], (8, LANES)), BF16)
        c1 = {il: [bcast(c1_ref, h, il) for h in range(HEADS)] for il in ils}
        e1 = {il: [bcast(e1_ref, h, il) for h in range(HEADS)] for il in ils}
        for rt in range(HEAD_DIM // ROW_TILE):
            r0 = rt * ROW_TILE
            rank2 = [key2_scr[key2_rows(ch, rt, h), :] for h in range(HEADS)]
            e2 = [key2_scr[key2_rows(ch, rt, HEADS + h), :] for h in range(HEADS)]
            for il in ils:
                rows = slice(il * HEAD_DIM + r0, il * HEAD_DIM + r0 + ROW_TILE)
                w = jnp.zeros((ROW_TILE, LANES), BF16)
                for h in range(HEADS):
                    w = w + jnp.where(rank2[h] < c1[il][h], e2[h] * e1[il][h],
                                      jnp.zeros_like(w))
                hid = hid_cur[rows, lanes]
                gelu2 = hid * (1.0 + lax.erf(hid * 0.7071067811865476))
                g_scr[rows, lanes] = w * gelu2.astype(BF16)

    def stage(hid_new, hid_cur, i1_base):
        hid_new[...] = _dot(u_ref[...], xnt_ref[...])
        depth = min(MXU_TILE, te)
        for k0 in range(0, te, depth):
            ils = list(range(k0 // HEAD_DIM, (k0 + depth) // HEAD_DIM))
            for ch in range(tn // LANES):
                gate_tile(hid_cur, i1_base, ils, ch)
        acc_scr[...] += _dot(vt_ref[0], g_scr[...])

    group = c1_ref.shape[1]

    @pl.when(t % 2 == 0)
    def _():
        stage(hid0, hid1, nb % group)

    @pl.when(t % 2 == 1)
    def _():
        stage(hid1, hid0, 0)

    @pl.when(t == pl.num_programs(1) - 1)
    def _():
        y = x_ref[...] + acc_scr[...].T
        if final_norm:
            y = _rms(y, gfin_ref[...])
        y_ref[...] = y


def _peer_dense(xnt, u, vt, rank2, e2, c1, e1, x, gfin, tn, final_norm):
    N, D = x.shape
    nblk, _, te = vt.shape
    assert N % tn == 0 and u.shape[0] == nblk * te and te % HEAD_DIM == 0
    nb = te // HEAD_DIM
    group = max(8, nb)
    assert group // nb <= 2 and HEAD_DIM % group == 0
    key2_blk = pl.BlockSpec((HEADS, HEAD_DIM, tn), lambda i, t: (0, 0, i))
    key1_blk = pl.BlockSpec(
        (HEADS, group, tn),
        lambda i, t: (0, jnp.clip(((t - 1) * nb) // group, 0, HEAD_DIM // group - 1), i))
    return pl.pallas_call(
        functools.partial(_peer_dense_kernel, final_norm=final_norm),
        grid=(N // tn, nblk + 1),
        in_specs=[pl.BlockSpec((D, tn), lambda i, t: (0, i)),
                  pl.BlockSpec((te, D), lambda i, t: (jnp.minimum(t, nblk - 1), 0)),
                  pl.BlockSpec((1, D, te), lambda i, t: (jnp.maximum(t - 1, 0), 0, 0)),
                  key2_blk, key2_blk, key1_blk, key1_blk,
                  pl.BlockSpec((tn, D), lambda i, t: (i, 0)),
                  pl.BlockSpec((1, D), lambda i, t: (0, 0))],
        out_specs=pl.BlockSpec((tn, D), lambda i, t: (i, 0)),
        out_shape=jax.ShapeDtypeStruct((N, D), F32),
        scratch_shapes=[pltpu.VMEM((te, tn), F32), pltpu.VMEM((te, tn), F32),
                        pltpu.VMEM((te, tn), BF16),
                        pltpu.VMEM((D, tn), F32),
                        pltpu.VMEM((2 * HEADS * HEAD_DIM * (tn // LANES), LANES), BF16)],
        compiler_params=pltpu.CompilerParams(
            dimension_semantics=("arbitrary", "arbitrary"), vmem_limit_bytes=VMEM_LIMIT),
        name="peer_dense",
    )(xnt, u, vt, rank2, e2, c1, e1, x, gfin)


def _peer(x, g, wq_t, keys, u, vt, gfin, tn_router, tn_dense, final_norm):
    xnt, rank2, e2, c1, e1 = _router(x, g, wq_t, keys, tn_router)
    return _peer_dense(xnt, u, vt, rank2, e2, c1, e1, x, gfin, tn_dense, final_norm)


HGRN_CHUNK = 64
HGRN_SEQS = 2
CONV_CHUNK = 256
CONV_SAMPLE_BLOCK = 32
PEER_ROUTER_TOKENS = 512
PEER_DENSE_TOKENS = 512
PEER_EXPERTS = 512


def kernel(x_prompt, x_sample, state_hgrn, state_conv, norm_mix, norm_ffn, norm_final, hg_w_in, hg_w_out, hg_gnorm, hg_lb_logits, cv_w_pw1, cv_b_pw1, cv_w_dw, cv_b_dw, cv_ln_g, cv_ln_b, cv_w_pw2, cv_b_pw2, peer_w_q, peer_sub_keys, peer_u, peer_v):
    B, T, D = x_prompt.shape
    NS = x_sample.shape[0]
    depth = norm_mix.shape[0]
    row = lambda a: a.reshape(1, -1)

    lb_all = jnp.cumsum(jax.nn.softmax(hg_lb_logits.astype(F32), axis=0), axis=0)

    xp = x_prompt
    xs = x_sample.reshape(NS, D)
    hg_p, hg_s, cv_p, cv_s = [], [], [], []
    for i in range(depth):
        j = i // 2
        gmix = row(norm_mix[i])
        if i % 2 == 0:
            w_in = hg_w_in[j].astype(BF16)
            w_out = hg_w_out[j].astype(BF16)
            gn = row(hg_gnorm[j])
            lb = row(lb_all[i])
            xp, sp = _hgrn_prompt(xp, gmix, w_in, w_out, gn, lb, HGRN_CHUNK, HGRN_SEQS)
            xs, ss = _hgrn_sample(xs, gmix, w_in, w_out, gn, lb, state_hgrn[j])
            hg_p.append(sp)
            hg_s.append(ss)
        else:
            cw = (cv_w_pw1[j].astype(BF16), row(cv_b_pw1[j]), cv_w_dw[j], row(cv_b_dw[j]),
                  row(cv_ln_g[j]), row(cv_ln_b[j]), cv_w_pw2[j].astype(BF16), row(cv_b_pw2[j]))
            xp, bp = _conv_prompt(xp, gmix, *cw, CONV_CHUNK)
            xs, bs = _conv_sample(xs, gmix, *cw, state_conv[j], CONV_SAMPLE_BLOCK)
            cv_p.append(bp)
            cv_s.append(bs)
        wq_t = peer_w_q[i].T.astype(BF16)
        keys = peer_sub_keys[i].reshape(2 * HEADS, HEAD_DIM, HEAD_DIM).astype(BF16)
        u = peer_u[i].astype(BF16)
        vt = peer_v[i].reshape(-1, PEER_EXPERTS, D).transpose(0, 2, 1).astype(BF16)
        gffn = row(norm_ffn[i])
        gfin = row(norm_final)
        last = i == depth - 1
        xp = _peer(xp.reshape(B * T, D), gffn, wq_t, keys, u, vt, gfin,
                   min(PEER_ROUTER_TOKENS, B * T), min(PEER_DENSE_TOKENS, B * T),
                   last).reshape(B, T, D)
        xs = _peer(xs, gffn, wq_t, keys, u, vt, gfin, min(LANES, NS), min(LANES, NS), last)
    return (xp, xs.reshape(NS, 1, D), jnp.stack(hg_p), jnp.stack(hg_s),
            jnp.stack(cv_p), jnp.stack(cv_s))
```

```python
import functools

import jax
import jax.numpy as jnp
from jax import lax
from jax.experimental import pallas as pl
from jax.experimental.pallas import tpu as pltpu

F32 = jnp.float32
BF16 = jnp.bfloat16
EPS = 1e-6

HEADS = 8
HEAD_DIM = 128
TOPK = 16
CONV_WIDTH = 31
LANES = 128
SUB = 16
VMEM_LIMIT = 56 * 1024 * 1024

NEG_INF = float("-inf")


def _rms(x, g):
    return x * lax.rsqrt(jnp.mean(x * x, axis=-1, keepdims=True) + EPS) * g


def _sigmoid(x):
    return 1.0 / (1.0 + jnp.exp(-x))


def _dot(a, b):
    return jnp.dot(a, b, preferred_element_type=F32)


def _dot_nt(a, b):
    return lax.dot_general(a, b, (((1,), (1,)), ((), ())), preferred_element_type=F32)


def _hgrn_prompt_kernel(x_ref, gmix_ref, win_ref, wout_ref, gn_ref, lb_ref, tril_ref, ones_ref,
                        y_ref, s_ref,
                        st_scr, q_scr, k_scr, v_scr, b_scr, p_scr, a_scr, o_scr):
    R, C, D = x_ref.shape
    M = R * C
    nsub = C // SUB
    ci = pl.program_id(1)

    @pl.when(ci == 0)
    def _():
        st_scr[...] = jnp.zeros_like(st_scr)

    xc = x_ref[...].reshape(M, D)
    hn = _rms(xc, gmix_ref[...])
    proj = _dot(hn.astype(BF16), win_ref[...])
    qz = proj[:, 0:D]
    fz = proj[:, D:2 * D]
    gz = proj[:, 3 * D:4 * D]
    lb = lb_ref[...]
    f = lb + (1.0 - lb) * _sigmoid(fz)
    q_scr[...] = qz * _sigmoid(qz)
    k_scr[...] = 1.0 - f
    v_scr[...] = proj[:, 2 * D:3 * D]
    lf = jnp.log(f)
    hi = lf.astype(BF16)
    rem = lf - hi.astype(F32)
    mid = rem.astype(BF16)
    lo = (rem - mid.astype(F32)).astype(BF16)
    tril = tril_ref[...]
    b_scr[...] = _dot(tril, hi) + _dot(tril, mid) + _dot(tril, lo)

    qe = (q_scr[...] * jnp.exp(b_scr[...])).astype(BF16)
    for r in range(R):
        for h in range(HEADS):
            L = slice(h * HEAD_DIM, (h + 1) * HEAD_DIM)
            o_scr[r * C:(r + 1) * C, L] = _dot_nt(qe[r * C:(r + 1) * C, L],
                                                  st_scr[r * HEADS + h].astype(BF16))

    t_iota = lax.broadcasted_iota(jnp.int32, (SUB, D), 0)
    for r in range(R):
        base = r * C
        for I in range(nsub):
            rows = slice(base + SUB * I, base + SUB * (I + 1))
            prev = slice(base, base + SUB * I)
            bI = b_scr[rows, :]
            qI = q_scr[rows, :]
            if I > 0:
                beta = b_scr[base + SUB * I - 1:base + SUB * I, :]
                qs = (qI * jnp.exp(bI - beta)).astype(BF16)
                ks = (k_scr[prev, :] * jnp.exp(beta - b_scr[prev, :])).astype(BF16)
                vs = v_scr[prev, :].astype(BF16)
                for h in range(HEADS):
                    L = slice(h * HEAD_DIM, (h + 1) * HEAD_DIM)
                    att = _dot_nt(qs[:, L], ks[:, L])
                    o_scr[rows, L] += _dot(att.astype(BF16), vs[:, L])
            for s in range(SUB):
                src = base + SUB * I + s
                keep = t_iota >= s
                dec = jnp.exp(jnp.where(keep, bI - b_scr[src:src + 1, :], 0.0))
                p = jnp.where(keep, qI * dec * k_scr[src:src + 1, :], 0.0)
                p_scr[src * SUB:(src + 1) * SUB, :] = p.astype(BF16)

    for h in range(HEADS):
        L = slice(h * HEAD_DIM, (h + 1) * HEAD_DIM)
        a_scr[:, L] = _dot(p_scr[:, L], ones_ref[...])
    for J in range(M // SUB):
        rows = slice(SUB * J, SUB * (J + 1))
        acc = o_scr[rows, :]
        for s in range(SUB):
            src = SUB * J + s
            acc = acc + a_scr[src * SUB:(src + 1) * SUB, :] * v_scr[src:src + 1, :]
        o_scr[rows, :] = acc

    for r in range(R):
        rows = slice(r * C, (r + 1) * C)
        b_last = b_scr[(r + 1) * C - 1:(r + 1) * C, :]
        kd = (k_scr[rows, :] * jnp.exp(b_last - b_scr[rows, :])).astype(BF16)
        eb_last = jnp.exp(b_last)
        for h in range(HEADS):
            L = slice(h * HEAD_DIM, (h + 1) * HEAD_DIM)
            vt = v_scr[rows, L].T.astype(BF16)
            st_scr[r * HEADS + h] = st_scr[r * HEADS + h] * eb_last[:, L] + _dot(vt, kd[:, L])

    gn = gn_ref[...]
    for h in range(HEADS):
        L = slice(h * HEAD_DIM, (h + 1) * HEAD_DIM)
        oh = o_scr[:, L]
        gzh = gz[:, L]
        o_scr[:, L] = _rms(oh, gn) * (gzh * _sigmoid(gzh))
    y_ref[...] = (xc + _dot(o_scr[...].astype(BF16), wout_ref[...])).reshape(R, C, D)

    @pl.when(ci == pl.num_programs(1) - 1)
    def _():
        for r in range(R):
            for h in range(HEADS):
                s_ref[r, h] = st_scr[r * HEADS + h].T


def _hgrn_prompt(x, gmix, w_in, w_out, gnorm, lb, chunk, seqs):
    B, T, D = x.shape
    C = min(chunk, T)
    R = min(seqs, B)
    assert T % C == 0 and C % SUB == 0 and B % R == 0
    M = R * C
    tril = jnp.kron(jnp.eye(R, dtype=F32), jnp.tril(jnp.ones((C, C), F32))).astype(BF16)
    ones = jnp.ones((HEAD_DIM, HEAD_DIM), BF16)
    const = lambda shape: pl.BlockSpec(shape, lambda b, c: (0,) * len(shape))
    return pl.pallas_call(
        _hgrn_prompt_kernel,
        grid=(B // R, T // C),
        in_specs=[
            pl.BlockSpec((R, C, D), lambda b, c: (b, c, 0)),
            const((1, D)), const((D, 4 * D)), const((D, D)), const((1, HEAD_DIM)), const((1, D)),
            const((M, M)), const((HEAD_DIM, HEAD_DIM)),
        ],
        out_specs=[
            pl.BlockSpec((R, C, D), lambda b, c: (b, c, 0)),
            pl.BlockSpec((R, HEADS, HEAD_DIM, HEAD_DIM), lambda b, c: (b, 0, 0, 0)),
        ],
        out_shape=[
            jax.ShapeDtypeStruct((B, T, D), F32),
            jax.ShapeDtypeStruct((B, HEADS, HEAD_DIM, HEAD_DIM), F32),
        ],
        scratch_shapes=[
            pltpu.VMEM((R * HEADS, HEAD_DIM, HEAD_DIM), F32),
            pltpu.VMEM((M, D), F32),
            pltpu.VMEM((M, D), F32),
            pltpu.VMEM((M, D), F32),
            pltpu.VMEM((M, D), F32),
            pltpu.VMEM((M * SUB, D), BF16),
            pltpu.VMEM((M * SUB, D), F32),
            pltpu.VMEM((M, D), F32),
        ],
        compiler_params=pltpu.CompilerParams(
            dimension_semantics=("arbitrary", "arbitrary"), vmem_limit_bytes=VMEM_LIMIT),
        name="hgrn_prompt",
    )(x, gmix, w_in, w_out, gnorm, lb, tril, ones)


def _hgrn_sample_kernel(x_ref, gmix_ref, win_ref, wout_ref, gn_ref, lb_ref, s_ref,
                        y_ref, snew_ref, proj_scr, o_scr):
    n = pl.program_id(0)
    D = x_ref.shape[1]

    @pl.when(n == 0)
    def _():
        hn = _rms(x_ref[...], gmix_ref[...])
        proj_scr[...] = _dot(hn.astype(BF16), win_ref[...])

    row = proj_scr[pl.ds(n, 1), :]
    qz = row[:, 0:D]
    fz = row[:, D:2 * D]
    v = row[:, 2 * D:3 * D]
    lb = lb_ref[...]
    f = lb + (1.0 - lb) * _sigmoid(fz)
    q = qz * _sigmoid(qz)
    k = 1.0 - f

    def cols(r):
        stacked = jnp.concatenate(
            [r[:, h * HEAD_DIM:(h + 1) * HEAD_DIM] for h in range(HEADS)]
            + [jnp.zeros((HEAD_DIM - HEADS, HEAD_DIM), F32)], axis=0)
        return stacked.T

    fc = cols(f)
    kc = cols(k)
    qc = cols(q)
    outs = []
    for h in range(HEADS):
        L = slice(h * HEAD_DIM, (h + 1) * HEAD_DIM)
        s_new = s_ref[0, h] * fc[:, h:h + 1] + kc[:, h:h + 1] * v[:, L]
        snew_ref[0, h] = s_new
        outs.append(jnp.sum(qc[:, h:h + 1] * s_new, axis=0, keepdims=True))
    o_scr[pl.ds(n, 1), :] = jnp.concatenate(outs, axis=1)

    @pl.when(n == pl.num_programs(0) - 1)
    def _():
        gn = gn_ref[...]
        gz = proj_scr[:, 3 * D:4 * D]
        for h in range(HEADS):
            L = slice(h * HEAD_DIM, (h + 1) * HEAD_DIM)
            gzh = gz[:, L]
            o_scr[:, L] = _rms(o_scr[:, L], gn) * (gzh * _sigmoid(gzh))
        y_ref[...] = x_ref[...] + _dot(o_scr[...].astype(BF16), wout_ref[...])


def _hgrn_sample(x, gmix, w_in, w_out, gnorm, lb, state):
    N, D = x.shape
    const = lambda shape: pl.BlockSpec(shape, lambda n: (0,) * len(shape))
    sblk = pl.BlockSpec((1, HEADS, HEAD_DIM, HEAD_DIM), lambda n: (n, 0, 0, 0))
    return pl.pallas_call(
        _hgrn_sample_kernel,
        grid=(N,),
        in_specs=[const((N, D)), const((1, D)), const((D, 4 * D)), const((D, D)),
                  const((1, HEAD_DIM)), const((1, D)), sblk],
        out_specs=[const((N, D)), sblk],
        out_shape=[jax.ShapeDtypeStruct((N, D), F32),
                   jax.ShapeDtypeStruct((N, HEADS, HEAD_DIM, HEAD_DIM), F32)],
        scratch_shapes=[pltpu.VMEM((N, 4 * D), F32), pltpu.VMEM((N, D), F32)],
        compiler_params=pltpu.CompilerParams(
            dimension_semantics=("arbitrary",), vmem_limit_bytes=VMEM_LIMIT),
        name="hgrn_sample",
    )(x, gmix, w_in, w_out, gnorm, lb, state)


CONV_PAD = 32


def _conv_tail(y, xc, bdw_ref, lng_ref, lnb_ref, pw2_ref, bpw2_ref):
    y = y + bdw_ref[...]
    mu = jnp.mean(y, axis=-1, keepdims=True)
    yc = y - mu
    var = jnp.mean(yc * yc, axis=-1, keepdims=True)
    yn = yc * lax.rsqrt(var + EPS) * lng_ref[...] + lnb_ref[...]
    z = yn * _sigmoid(yn)
    return xc + _dot(z.astype(BF16), pw2_ref[...]) + bpw2_ref[...]


def _glu(xc, gmix_ref, pw1_ref, bpw1_ref):
    D = xc.shape[1]
    hn = _rms(xc, gmix_ref[...])
    h2 = _dot(hn.astype(BF16), pw1_ref[...]) + bpw1_ref[...]
    return h2[:, 0:D] * _sigmoid(h2[:, D:2 * D])


def _conv_prompt_kernel(x_ref, gmix_ref, pw1_ref, bpw1_ref, wdw_ref, bdw_ref, lng_ref, lnb_ref,
                        pw2_ref, bpw2_ref, y_ref, buf_ref, full_scr, sh_scr, y_scr):
    C = x_ref.shape[1]
    D = x_ref.shape[2]
    ci = pl.program_id(1)
    hist = CONV_WIDTH - 1
    off = CONV_PAD - hist
    sh_rows = sh_scr.shape[1]

    @pl.when(ci == 0)
    def _():
        full_scr[0:CONV_PAD, :] = jnp.zeros((CONV_PAD, full_scr.shape[1]), F32)

    xc = x_ref[0]
    full_scr[CONV_PAD:CONV_PAD + C, :] = _glu(xc, gmix_ref, pw1_ref, bpw1_ref)

    def lane_chunk(c, carry):
        lanes = pl.ds(pl.multiple_of(c * LANES, LANES), LANES)
        for b in range(1, 8):
            sh_scr[b - 1, :, lanes] = full_scr[b:b + sh_rows, lanes]
        w = [jnp.broadcast_to(wdw_ref[j:j + 1, lanes], (8, LANES)) for j in range(CONV_WIDTH)]
        for r0 in range(0, C, 8):
            acc = None
            for j in range(CONV_WIDTH):
                a, b = divmod(off + j, 8)
                rows = slice(8 * a + r0, 8 * a + r0 + 8)
                src = full_scr[rows, lanes] if b == 0 else sh_scr[b - 1, rows, lanes]
                acc = src * w[j] if acc is None else acc + src * w[j]
            y_scr[r0:r0 + 8, lanes] = acc
        return carry

    lax.fori_loop(0, D // LANES, lane_chunk, 0)
    y_ref[0] = _conv_tail(y_scr[...], xc, bdw_ref, lng_ref, lnb_ref, pw2_ref, bpw2_ref)

    @pl.when(ci == pl.num_programs(1) - 1)
    def _():
        buf_ref[0] = full_scr[C + off:C + CONV_PAD, :]

    full_scr[0:CONV_PAD, :] = full_scr[C:C + CONV_PAD, :]


def _conv_prompt(x, gmix, pw1, bpw1, wdw, bdw, lng, lnb, pw2, bpw2, chunk):
    B, T, D = x.shape
    C = min(chunk, T)
    assert T % C == 0 and C >= CONV_PAD
    hist = CONV_WIDTH - 1
    const = lambda shape: pl.BlockSpec(shape, lambda b, c: (0,) * len(shape))
    return pl.pallas_call(
        _conv_prompt_kernel,
        grid=(B, T // C),
        in_specs=[pl.BlockSpec((1, C, D), lambda b, c: (b, c, 0)),
                  const((1, D)), const((D, 2 * D)), const((1, 2 * D)), const((CONV_WIDTH, D)),
                  const((1, D)), const((1, D)), const((1, D)), const((D, D)), const((1, D))],
        out_specs=[pl.BlockSpec((1, C, D), lambda b, c: (b, c, 0)),
                   pl.BlockSpec((1, hist, D), lambda b, c: (b, 0, 0))],
        out_shape=[jax.ShapeDtypeStruct((B, T, D), F32),
                   jax.ShapeDtypeStruct((B, hist, D), F32)],
        scratch_shapes=[pltpu.VMEM((CONV_PAD + C, D), F32),
                        pltpu.VMEM((7, C + CONV_PAD - 8, D), F32),
                        pltpu.VMEM((C, D), F32)],
        compiler_params=pltpu.CompilerParams(
            dimension_semantics=("arbitrary", "arbitrary"), vmem_limit_bytes=VMEM_LIMIT),
        name="conv_prompt",
    )(x, gmix, pw1, bpw1, wdw, bdw, lng, lnb, pw2, bpw2)


def _conv_sample_kernel(x_ref, gmix_ref, pw1_ref, bpw1_ref, wdw_ref, bdw_ref, lng_ref, lnb_ref,
                        pw2_ref, bpw2_ref, buf_ref, y_ref, nbuf_ref, u_scr, y_scr):
    nt = x_ref.shape[0]
    hist = CONV_WIDTH - 1
    xc = x_ref[...]
    u_scr[...] = _glu(xc, gmix_ref, pw1_ref, bpw1_ref)
    w_hist = wdw_ref[0:hist, :]
    w_last = wdw_ref[hist:hist + 1, :]

    def body(n, carry):
        buf = buf_ref[n]
        u = u_scr[pl.ds(n, 1), :]
        y_scr[pl.ds(n, 1), :] = jnp.sum(buf * w_hist, axis=0, keepdims=True) + u * w_last
        nbuf_ref[n, 0:hist - 1, :] = buf[1:hist, :]
        nbuf_ref[n, hist - 1:hist, :] = u
        return carry

    lax.fori_loop(0, nt, body, 0)
    y_ref[...] = _conv_tail(y_scr[...], xc, bdw_ref, lng_ref, lnb_ref, pw2_ref, bpw2_ref)


def _conv_sample(x, gmix, pw1, bpw1, wdw, bdw, lng, lnb, pw2, bpw2, buf, block):
    N, D = x.shape
    nt = min(block, N)
    assert N % nt == 0
    hist = CONV_WIDTH - 1
    const = lambda shape: pl.BlockSpec(shape, lambda i: (0,) * len(shape))
    return pl.pallas_call(
        _conv_sample_kernel,
        grid=(N // nt,),
        in_specs=[pl.BlockSpec((nt, D), lambda i: (i, 0)),
                  const((1, D)), const((D, 2 * D)), const((1, 2 * D)), const((CONV_WIDTH, D)),
                  const((1, D)), const((1, D)), const((1, D)), const((D, D)), const((1, D)),
                  pl.BlockSpec((nt, hist, D), lambda i: (i, 0, 0))],
        out_specs=[pl.BlockSpec((nt, D), lambda i: (i, 0)),
                   pl.BlockSpec((nt, hist, D), lambda i: (i, 0, 0))],
        out_shape=[jax.ShapeDtypeStruct((N, D), F32),
                   jax.ShapeDtypeStruct((N, hist, D), F32)],
        scratch_shapes=[pltpu.VMEM((nt, D), F32), pltpu.VMEM((nt, D), F32)],
        compiler_params=pltpu.CompilerParams(
            dimension_semantics=("arbitrary",), vmem_limit_bytes=VMEM_LIMIT),
        name="conv_sample",
    )(x, gmix, pw1, bpw1, wdw, bdw, lng, lnb, pw2, bpw2, buf)


def _topk_ranks(vals, ids, k):
    rank = jnp.full(vals.shape, float(k), F32)
    tops = []
    for r in range(k):
        m = jnp.max(vals, axis=0, keepdims=True)
        first = jnp.min(jnp.where(vals == m, ids, 1e9), axis=0, keepdims=True)
        sel = ids == first
        vals = jnp.where(sel, NEG_INF, vals)
        rank = jnp.where(sel, float(r), rank)
        tops.append(m)
    return rank, tops


INT_MIN = -2 ** 31


def _sort_key(s):
    b = lax.bitcast_convert_type(jnp.where(s == 0.0, 0.0, s), jnp.int32)
    return b ^ (lax.shift_right_arithmetic(b, 31) & jnp.int32(0x7FFFFFFF))


def _key_value(k):
    b = k ^ (lax.shift_right_arithmetic(k, 31) & jnp.int32(0x7FFFFFFF))
    return lax.bitcast_convert_type(b, F32)


def _topk_keys(keys, k):
    tops = []
    for r in range(k):
        m = jnp.max(keys, axis=0, keepdims=True)
        keys = jnp.where(keys == m, jnp.int32(INT_MIN + r), keys)
        tops.append(m)
    return keys, tops


def _selected(keys, k):
    sel = keys < jnp.int32(INT_MIN + k)
    n = jnp.sum(jnp.where(sel, 1.0, 0.0), axis=0, keepdims=True)
    return sel, jnp.where(n != float(k), 1.0, 0.0)


def _dup_bf16(x):
    u = lax.bitcast_convert_type(x.astype(BF16).astype(F32), jnp.int32)
    return u | lax.shift_right_logical(u, 16)


def _router_kernel(x_ref, g_ref, wq_ref, keys_ref,
                   xnt_ref, rank2_ref, e2_ref, c1_ref, e1_ref,
                   s_scr, rank_scr, top_scr, miss_scr):
    tn = x_ref.shape[0]
    xn = _rms(x_ref[...], g_ref[...])
    xnt = xn.T.astype(BF16)
    xnt_ref[...] = xnt
    qt = _dot(wq_ref[...], xnt).astype(BF16)
    for hp in range(2 * HEADS):
        s_scr[hp] = _dot(keys_ref[hp], qt[hp * HEAD_DIM:(hp + 1) * HEAD_DIM, :])

    outs = (rank2_ref, e2_ref, c1_ref, e1_ref)
    scr = (s_scr, rank_scr, top_scr)
    _select_experts(tn, scr, outs, miss_scr)


def _run_with_tie_fallback(n, per_group, problem, miss_scr):
    def fast(g, carry):
        miss = [problem(g * per_group + c, False) for c in range(per_group)]
        miss_scr[pl.ds(g, 1), :] = functools.reduce(jnp.maximum, miss)
        return carry

    lax.fori_loop(0, n // per_group, fast, 0, unroll=2 if per_group == 1 else 1)

    def fix(g, carry):
        @pl.when(jnp.max(miss_scr[pl.ds(g, 1), :]) > 0.0)
        def _():
            for c in range(per_group):
                problem(g * per_group + c, True)
        return carry

    lax.fori_loop(0, n // per_group, fix, 0)


def _select_experts(tn, scr, outs, miss_scr):
    s_scr, rank_scr, top_scr = scr
    rank2_ref, e2_ref, c1_ref, e1_ref = outs
    nch = tn // LANES
    key_ids = lax.broadcasted_iota(jnp.int32, (HEAD_DIM, LANES), 0).astype(F32)

    def half_problem(it, exact):
        hp = it // nch
        lanes = pl.ds(pl.multiple_of((it % nch) * LANES, LANES), LANES)
        s = s_scr[hp, :, lanes]
        miss = None
        if exact:
            rank, tops = _topk_ranks(s, key_ids, TOPK)
            top = jnp.concatenate(tops, axis=0)
        else:
            keys, tops = _topk_keys(_sort_key(s), TOPK)
            sel, miss = _selected(keys, TOPK)
            rank = jnp.where(sel, (keys ^ jnp.int32(INT_MIN)).astype(F32), float(TOPK))
            top = _key_value(jnp.concatenate(tops, axis=0))
        rank_scr[hp, :, lanes] = rank
        top_scr[hp, :, lanes] = top
        return miss

    _run_with_tie_fallback(2 * HEADS * nch, nch, half_problem, miss_scr)

    row16 = lax.broadcasted_iota(jnp.int32, (16, LANES), 0)
    row8 = lax.broadcasted_iota(jnp.int32, (8, LANES), 0)
    r16 = row16.astype(F32)
    r8 = row8.astype(F32)
    groups_valid = [row16 >= 0, row8 >= 0, row16 >= 2, row8 >= 2,
                    (row8 >= 2) & (row8 <= 4), (row8 >= 2) & (row8 <= 3), row8 == 2]
    groups_ids = [r16, 16.0 + r8, r16 * 16.0, r8 * 16.0 + 1.0, 32.0 + r8, 48.0 + r8, 64.0 + r8]
    cand_ids = jnp.concatenate(
        [jnp.where(v, i, 1e9) for v, i in zip(groups_valid, groups_ids)], axis=0)

    def head_problem(it, exact):
        h = it // nch
        lanes = pl.ds(pl.multiple_of((it % nch) * LANES, LANES), LANES)
        v1 = top_scr[2 * h, :, lanes]
        v2 = top_scr[2 * h + 1, :, lanes]
        groups = [v1[0:1] + v2, v1[1:2] + v2[0:8], v1 + v2[0:1], v1[0:8] + v2[1:2],
                  v1[2:3] + v2[0:8], v1[3:4] + v2[0:8], v1[4:5] + v2[0:8]]
        cand = jnp.concatenate(
            [jnp.where(v, g, NEG_INF) for v, g in zip(groups_valid, groups)], axis=0)
        miss = None
        if exact:
            crank, _ = _topk_ranks(cand, cand_ids, TOPK)
            sel = crank < float(TOPK)
        else:
            ckeys, _ = _topk_keys(_sort_key(cand), TOPK)
            sel, miss = _selected(ckeys, TOPK)
        z = jnp.sum(jnp.where(sel, jnp.exp(jnp.where(sel, cand - cand[0:1], 0.0)), 0.0),
                    axis=0, keepdims=True)
        self32 = jnp.where(sel, 1.0, 0.0)
        cnt = lambda a, b: jnp.sum(self32[a:b], axis=0, keepdims=True)
        per_r1 = (self32[24:40]
                  + jnp.concatenate([self32[40:48], jnp.zeros((8, LANES), F32)], axis=0)
                  + jnp.where(row16 == 0, cnt(0, 16), 0.0)
                  + jnp.where(row16 == 1, cnt(16, 24), 0.0)
                  + jnp.where(row16 == 2, cnt(48, 56), 0.0)
                  + jnp.where(row16 == 3, cnt(56, 64), 0.0)
                  + jnp.where(row16 == 4, cnt(64, 72), 0.0))
        rank1 = rank_scr[2 * h, :, lanes]
        rank2 = rank_scr[2 * h + 1, :, lanes]
        c1 = jnp.zeros((HEAD_DIM, LANES), F32)
        for j in range(1, TOPK // 3 + 1):
            reach = jnp.sum(jnp.where(per_r1 >= float(j), 1.0, 0.0), axis=0, keepdims=True)
            c1 = c1 + jnp.where(rank1 < reach, 1.0, 0.0)
        c1 = jnp.where(rank1 == 0.0, per_r1[0:1], jnp.where(rank1 == 1.0, per_r1[1:2], c1))
        s1 = s_scr[2 * h, :, lanes]
        s2 = s_scr[2 * h + 1, :, lanes]
        in1 = rank1 < float(TOPK)
        in2 = rank2 < float(TOPK)
        e1 = jnp.where(in1, jnp.exp(jnp.where(in1, s1 - v1[0:1], 0.0)), 0.0)
        e2 = jnp.where(in2, jnp.exp(jnp.where(in2, s2 - v2[0:1], 0.0)), 0.0) / z
        rank2_ref[h, :, lanes] = rank2.astype(BF16)
        e2_ref[h, :, lanes] = (0.5 * e2).astype(BF16)
        c1_ref[h, :, lanes] = _dup_bf16(c1)
        e1_ref[h, :, lanes] = _dup_bf16(e1)
        return miss

    _run_with_tie_fallback(HEADS * nch, nch, head_problem, miss_scr)


def _router(x, g, wq_t, keys, tn):
    N, D = x.shape
    assert N % tn == 0 and tn % LANES == 0
    const = lambda shape: pl.BlockSpec(shape, lambda i: (0,) * len(shape))
    gate_blk = pl.BlockSpec((HEADS, HEAD_DIM, tn), lambda i: (0, 0, i))
    gate_shape = lambda dt: jax.ShapeDtypeStruct((HEADS, HEAD_DIM, N), dt)
    return pl.pallas_call(
        _router_kernel,
        grid=(N // tn,),
        in_specs=[pl.BlockSpec((tn, D), lambda i: (i, 0)), const((1, D)),
                  const(wq_t.shape), const(keys.shape)],
        out_specs=[pl.BlockSpec((D, tn), lambda i: (0, i)),
                   gate_blk, gate_blk, gate_blk, gate_blk],
        out_shape=[jax.ShapeDtypeStruct((D, N), BF16),
                   gate_shape(BF16), gate_shape(BF16), gate_shape(jnp.int32),
                   gate_shape(jnp.int32)],
        scratch_shapes=[pltpu.VMEM((2 * HEADS, HEAD_DIM, tn), F32),
                        pltpu.VMEM((2 * HEADS, HEAD_DIM, tn), F32),
                        pltpu.VMEM((2 * HEADS, TOPK, tn), F32),
                        pltpu.VMEM((2 * HEADS * (tn // LANES), LANES), F32)],
        compiler_params=pltpu.CompilerParams(
            dimension_semantics=("arbitrary",), vmem_limit_bytes=VMEM_LIMIT),
        name="peer_router",
    )(x, g, wq_t, keys)


ROW_TILE = 16
MXU_TILE = 256


def _peer_dense_kernel(xnt_ref, u_ref, vt_ref, rank2_ref, e2_ref, c1_ref, e1_ref, x_ref, gfin_ref,
                       y_ref, hid_scr, g_scr, acc_scr, key2_scr, *, final_norm):
    t = pl.program_id(1)
    te, tn = hid_scr.shape
    nb = te // HEAD_DIM

    def key2_rows(ch, rt, slot):
        tile = (ch * (HEAD_DIM // ROW_TILE) + rt) * 2 * HEADS + slot
        if isinstance(ch, int):
            return slice(tile * ROW_TILE, (tile + 1) * ROW_TILE)
        return pl.ds(pl.multiple_of(tile * ROW_TILE, ROW_TILE), ROW_TILE)

    @pl.when(t == 0)
    def _():
        acc_scr[...] = jnp.zeros_like(acc_scr)
        for ch in range(tn // LANES):
            lanes = slice(ch * LANES, (ch + 1) * LANES)
            for rt in range(HEAD_DIM // ROW_TILE):
                keys2 = slice(rt * ROW_TILE, (rt + 1) * ROW_TILE)
                for h in range(HEADS):
                    key2_scr[key2_rows(ch, rt, h), :] = rank2_ref[h, keys2, lanes]
                    key2_scr[key2_rows(ch, rt, HEADS + h), :] = e2_ref[h, keys2, lanes]

    hid_scr[...] = _dot(u_ref[...], xnt_ref[...])

    def lane_chunk(ch, carry):
        lanes = pl.ds(pl.multiple_of(ch * LANES, LANES), LANES)
        for il in range(nb):
            bcast = lambda ref, h: pltpu.bitcast(
                jnp.broadcast_to(ref[h, il:il + 1, lanes], (8, LANES)), BF16)
            c1 = [bcast(c1_ref, h) for h in range(HEADS)]
            e1 = [bcast(e1_ref, h) for h in range(HEADS)]
            for rt in range(HEAD_DIM // ROW_TILE):
                r0 = rt * ROW_TILE
                rows = slice(il * HEAD_DIM + r0, il * HEAD_DIM + r0 + ROW_TILE)
                w = jnp.zeros((ROW_TILE, LANES), BF16)
                for h in range(HEADS):
                    w = w + jnp.where(key2_scr[key2_rows(ch, rt, h), :] < c1[h],
                                      key2_scr[key2_rows(ch, rt, HEADS + h), :] * e1[h],
                                      jnp.zeros_like(w))
                hid = hid_scr[rows, lanes]
                gelu2 = hid * (1.0 + lax.erf(hid * 0.7071067811865476))
                g_scr[rows, lanes] = w * gelu2.astype(BF16)
        return carry

    lax.fori_loop(0, tn // LANES, lane_chunk, 0)
    acc_scr[...] += _dot(vt_ref[0], g_scr[...])

    @pl.when(t == pl.num_programs(1) - 1)
    def _():
        y = x_ref[...] + acc_scr[...].T
        if final_norm:
            y = _rms(y, gfin_ref[...])
        y_ref[...] = y


def _peer_dense(xnt, u, vt, rank2, e2, c1, e1, x, gfin, tn, final_norm):
    N, D = x.shape
    nblk, _, te = vt.shape
    assert N % tn == 0 and u.shape[0] == nblk * te and te % HEAD_DIM == 0
    nb = te // HEAD_DIM
    assert nb % 8 == 0
    key2_blk = pl.BlockSpec((HEADS, HEAD_DIM, tn), lambda i, t: (0, 0, i))
    key1_blk = pl.BlockSpec((HEADS, nb, tn), lambda i, t: (0, t, i))
    return pl.pallas_call(
        functools.partial(_peer_dense_kernel, final_norm=final_norm),
        grid=(N // tn, nblk),
        in_specs=[pl.BlockSpec((D, tn), lambda i, t: (0, i)),
                  pl.BlockSpec((te, D), lambda i, t: (t, 0)),
                  pl.BlockSpec((1, D, te), lambda i, t: (t, 0, 0)),
                  key2_blk, key2_blk, key1_blk, key1_blk,
                  pl.BlockSpec((tn, D), lambda i, t: (i, 0)),
                  pl.BlockSpec((1, D), lambda i, t: (0, 0))],
        out_specs=pl.BlockSpec((tn, D), lambda i, t: (i, 0)),
        out_shape=jax.ShapeDtypeStruct((N, D), F32),
        scratch_shapes=[pltpu.VMEM((te, tn), F32),
                        pltpu.VMEM((te, tn), BF16),
                        pltpu.VMEM((D, tn), F32),
                        pltpu.VMEM((2 * HEADS * HEAD_DIM * (tn // LANES), LANES), BF16)],
        compiler_params=pltpu.CompilerParams(
            dimension_semantics=("arbitrary", "arbitrary"), vmem_limit_bytes=VMEM_LIMIT),
        name="peer_dense",
    )(xnt, u, vt, rank2, e2, c1, e1, x, gfin)


def _peer(x, g, wq_t, keys, u, vt, gfin, tn_router, tn_dense, final_norm):
    xnt, rank2, e2, c1, e1 = _router(x, g, wq_t, keys, tn_router)
    return _peer_dense(xnt, u, vt, rank2, e2, c1, e1, x, gfin, tn_dense, final_norm)


HGRN_CHUNK = 64
HGRN_SEQS = 2
CONV_CHUNK = 256
CONV_SAMPLE_BLOCK = 32
PEER_ROUTER_TOKENS = 512
PEER_DENSE_TOKENS = 512
PEER_EXPERTS = 1024


def kernel(x_prompt, x_sample, state_hgrn, state_conv, norm_mix, norm_ffn, norm_final, hg_w_in, hg_w_out, hg_gnorm, hg_lb_logits, cv_w_pw1, cv_b_pw1, cv_w_dw, cv_b_dw, cv_ln_g, cv_ln_b, cv_w_pw2, cv_b_pw2, peer_w_q, peer_sub_keys, peer_u, peer_v):
    B, T, D = x_prompt.shape
    NS = x_sample.shape[0]
    depth = norm_mix.shape[0]
    row = lambda a: a.reshape(1, -1)

    lb_all = jnp.cumsum(jax.nn.softmax(hg_lb_logits.astype(F32), axis=0), axis=0)

    xp = x_prompt
    xs = x_sample.reshape(NS, D)
    hg_p, hg_s, cv_p, cv_s = [], [], [], []
    for i in range(depth):
        j = i // 2
        gmix = row(norm_mix[i])
        if i % 2 == 0:
            w_in = hg_w_in[j].astype(BF16)
            w_out = hg_w_out[j].astype(BF16)
            gn = row(hg_gnorm[j])
            lb = row(lb_all[i])
            xp, sp = _hgrn_prompt(xp, gmix, w_in, w_out, gn, lb, HGRN_CHUNK, HGRN_SEQS)
            xs, ss = _hgrn_sample(xs, gmix, w_in, w_out, gn, lb, state_hgrn[j])
            hg_p.append(sp)
            hg_s.append(ss)
        else:
            cw = (cv_w_pw1[j].astype(BF16), row(cv_b_pw1[j]), cv_w_dw[j], row(cv_b_dw[j]),
                  row(cv_ln_g[j]), row(cv_ln_b[j]), cv_w_pw2[j].astype(BF16), row(cv_b_pw2[j]))
            xp, bp = _conv_prompt(xp, gmix, *cw, CONV_CHUNK)
            xs, bs = _conv_sample(xs, gmix, *cw, state_conv[j], CONV_SAMPLE_BLOCK)
            cv_p.append(bp)
            cv_s.append(bs)
        wq_t = peer_w_q[i].T.astype(BF16)
        keys = peer_sub_keys[i].reshape(2 * HEADS, HEAD_DIM, HEAD_DIM).astype(BF16)
        u = peer_u[i].astype(BF16)
        vt = peer_v[i].reshape(-1, PEER_EXPERTS, D).transpose(0, 2, 1).astype(BF16)
        gffn = row(norm_ffn[i])
        gfin = row(norm_final)
        last = i == depth - 1
        xp = _peer(xp.reshape(B * T, D), gffn, wq_t, keys, u, vt, gfin,
                   min(PEER_ROUTER_TOKENS, B * T), min(PEER_DENSE_TOKENS, B * T),
                   last).reshape(B, T, D)
        xs = _peer(xs, gffn, wq_t, keys, u, vt, gfin, min(LANES, NS), min(LANES, NS), last)
    return (xp, xs.reshape(NS, 1, D), jnp.stack(hg_p), jnp.stack(hg_s),
            jnp.stack(cv_p), jnp.stack(cv_s))
```

```python
import functools

import jax
import jax.numpy as jnp
from jax import lax
from jax.experimental import pallas as pl
from jax.experimental.pallas import tpu as pltpu

F32 = jnp.float32
BF16 = jnp.bfloat16
EPS = 1e-6

HEADS = 8
HEAD_DIM = 128
TOPK = 16
CONV_WIDTH = 31
LANES = 128
SUB = 16
VMEM_LIMIT = 56 * 1024 * 1024

NEG_INF = float("-inf")


def _rms(x, g):
    return x * lax.rsqrt(jnp.mean(x * x, axis=-1, keepdims=True) + EPS) * g


def _sigmoid(x):
    return 1.0 / (1.0 + jnp.exp(-x))


def _dot(a, b):
    return jnp.dot(a, b, preferred_element_type=F32)


def _dot_nt(a, b):
    return lax.dot_general(a, b, (((1,), (1,)), ((), ())), preferred_element_type=F32)


def _hgrn_prompt_kernel(x_ref, gmix_ref, win_ref, wout_ref, gn_ref, lb_ref, tril_ref, ones_ref,
                        y_ref, s_ref,
                        st_scr, q_scr, k_scr, v_scr, b_scr, p_scr, a_scr, o_scr):
    R, C, D = x_ref.shape
    M = R * C
    nsub = C // SUB
    ci = pl.program_id(1)

    @pl.when(ci == 0)
    def _():
        st_scr[...] = jnp.zeros_like(st_scr)

    xc = x_ref[...].reshape(M, D)
    hn = _rms(xc, gmix_ref[...])
    proj = _dot(hn.astype(BF16), win_ref[...])
    qz = proj[:, 0:D]
    fz = proj[:, D:2 * D]
    gz = proj[:, 3 * D:4 * D]
    lb = lb_ref[...]
    f = lb + (1.0 - lb) * _sigmoid(fz)
    q_scr[...] = qz * _sigmoid(qz)
    k_scr[...] = 1.0 - f
    v_scr[...] = proj[:, 2 * D:3 * D]
    lf = jnp.log(f)
    hi = lf.astype(BF16)
    rem = lf - hi.astype(F32)
    mid = rem.astype(BF16)
    lo = (rem - mid.astype(F32)).astype(BF16)
    tril = tril_ref[...]
    b_scr[...] = _dot(tril, hi) + _dot(tril, mid) + _dot(tril, lo)

    qe = (q_scr[...] * jnp.exp(b_scr[...])).astype(BF16)
    for r in range(R):
        for h in range(HEADS):
            L = slice(h * HEAD_DIM, (h + 1) * HEAD_DIM)
            o_scr[r * C:(r + 1) * C, L] = _dot_nt(qe[r * C:(r + 1) * C, L],
                                                  st_scr[r * HEADS + h].astype(BF16))

    t_iota = lax.broadcasted_iota(jnp.int32, (SUB, D), 0)
    for r in range(R):
        base = r * C
        for I in range(nsub):
            rows = slice(base + SUB * I, base + SUB * (I + 1))
            prev = slice(base, base + SUB * I)
            bI = b_scr[rows, :]
            qI = q_scr[rows, :]
            if I > 0:
                beta = b_scr[base + SUB * I - 1:base + SUB * I, :]
                qs = (qI * jnp.exp(bI - beta)).astype(BF16)
                ks = (k_scr[prev, :] * jnp.exp(beta - b_scr[prev, :])).astype(BF16)
                vs = v_scr[prev, :].astype(BF16)
                for h in range(HEADS):
                    L = slice(h * HEAD_DIM, (h + 1) * HEAD_DIM)
                    att = _dot_nt(qs[:, L], ks[:, L])
                    o_scr[rows, L] += _dot(att.astype(BF16), vs[:, L])
            for s in range(SUB):
                src = base + SUB * I + s
                keep = t_iota >= s
                dec = jnp.exp(jnp.where(keep, bI - b_scr[src:src + 1, :], 0.0))
                p = jnp.where(keep, qI * dec * k_scr[src:src + 1, :], 0.0)
                p_scr[src * SUB:(src + 1) * SUB, :] = p.astype(BF16)

    for h in range(HEADS):
        L = slice(h * HEAD_DIM, (h + 1) * HEAD_DIM)
        a_scr[:, L] = _dot(p_scr[:, L], ones_ref[...])
    for J in range(M // SUB):
        rows = slice(SUB * J, SUB * (J + 1))
        acc = o_scr[rows, :]
        for s in range(SUB):
            src = SUB * J + s
            acc = acc + a_scr[src * SUB:(src + 1) * SUB, :] * v_scr[src:src + 1, :]
        o_scr[rows, :] = acc

    for r in range(R):
        rows = slice(r * C, (r + 1) * C)
        b_last = b_scr[(r + 1) * C - 1:(r + 1) * C, :]
        kd = (k_scr[rows, :] * jnp.exp(b_last - b_scr[rows, :])).astype(BF16)
        eb_last = jnp.exp(b_last)
        for h in range(HEADS):
            L = slice(h * HEAD_DIM, (h + 1) * HEAD_DIM)
            vt = v_scr[rows, L].T.astype(BF16)
            st_scr[r * HEADS + h] = st_scr[r * HEADS + h] * eb_last[:, L] + _dot(vt, kd[:, L])

    gn = gn_ref[...]
    for h in range(HEADS):
        L = slice(h * HEAD_DIM, (h + 1) * HEAD_DIM)
        oh = o_scr[:, L]
        gzh = gz[:, L]
        o_scr[:, L] = _rms(oh, gn) * (gzh * _sigmoid(gzh))
    y_ref[...] = (xc + _dot(o_scr[...].astype(BF16), wout_ref[...])).reshape(R, C, D)

    @pl.when(ci == pl.num_programs(1) - 1)
    def _():
        for r in range(R):
            for h in range(HEADS):
                s_ref[r, h] = st_scr[r * HEADS + h].T


def _hgrn_prompt(x, gmix, w_in, w_out, gnorm, lb, chunk, seqs):
    B, T, D = x.shape
    C = min(chunk, T)
    R = min(seqs, B)
    assert T % C == 0 and C % SUB == 0 and B % R == 0
    M = R * C
    tril = jnp.kron(jnp.eye(R, dtype=F32), jnp.tril(jnp.ones((C, C), F32))).astype(BF16)
    ones = jnp.ones((HEAD_DIM, HEAD_DIM), BF16)
    const = lambda shape: pl.BlockSpec(shape, lambda b, c: (0,) * len(shape))
    return pl.pallas_call(
        _hgrn_prompt_kernel,
        grid=(B // R, T // C),
        in_specs=[
            pl.BlockSpec((R, C, D), lambda b, c: (b, c, 0)),
            const((1, D)), const((D, 4 * D)), const((D, D)), const((1, HEAD_DIM)), const((1, D)),
            const((M, M)), const((HEAD_DIM, HEAD_DIM)),
        ],
        out_specs=[
            pl.BlockSpec((R, C, D), lambda b, c: (b, c, 0)),
            pl.BlockSpec((R, HEADS, HEAD_DIM, HEAD_DIM), lambda b, c: (b, 0, 0, 0)),
        ],
        out_shape=[
            jax.ShapeDtypeStruct((B, T, D), F32),
            jax.ShapeDtypeStruct((B, HEADS, HEAD_DIM, HEAD_DIM), F32),
        ],
        scratch_shapes=[
            pltpu.VMEM((R * HEADS, HEAD_DIM, HEAD_DIM), F32),
            pltpu.VMEM((M, D), F32),
            pltpu.VMEM((M, D), F32),
            pltpu.VMEM((M, D), F32),
            pltpu.VMEM((M, D), F32),
            pltpu.VMEM((M * SUB, D), BF16),
            pltpu.VMEM((M * SUB, D), F32),
            pltpu.VMEM((M, D), F32),
        ],
        compiler_params=pltpu.CompilerParams(
            dimension_semantics=("arbitrary", "arbitrary"), vmem_limit_bytes=VMEM_LIMIT),
        name="hgrn_prompt",
    )(x, gmix, w_in, w_out, gnorm, lb, tril, ones)


def _hgrn_sample_kernel(x_ref, gmix_ref, win_ref, wout_ref, gn_ref, lb_ref, s_ref,
                        y_ref, snew_ref, proj_scr, o_scr):
    n = pl.program_id(0)
    D = x_ref.shape[1]

    @pl.when(n == 0)
    def _():
        hn = _rms(x_ref[...], gmix_ref[...])
        proj_scr[...] = _dot(hn.astype(BF16), win_ref[...])

    row = proj_scr[pl.ds(n, 1), :]
    qz = row[:, 0:D]
    fz = row[:, D:2 * D]
    v = row[:, 2 * D:3 * D]
    lb = lb_ref[...]
    f = lb + (1.0 - lb) * _sigmoid(fz)
    q = qz * _sigmoid(qz)
    k = 1.0 - f

    def cols(r):
        stacked = jnp.concatenate(
            [r[:, h * HEAD_DIM:(h + 1) * HEAD_DIM] for h in range(HEADS)]
            + [jnp.zeros((HEAD_DIM - HEADS, HEAD_DIM), F32)], axis=0)
        return stacked.T

    fc = cols(f)
    kc = cols(k)
    qc = cols(q)
    outs = []
    for h in range(HEADS):
        L = slice(h * HEAD_DIM, (h + 1) * HEAD_DIM)
        s_new = s_ref[0, h] * fc[:, h:h + 1] + kc[:, h:h + 1] * v[:, L]
        snew_ref[0, h] = s_new
        outs.append(jnp.sum(qc[:, h:h + 1] * s_new, axis=0, keepdims=True))
    o_scr[pl.ds(n, 1), :] = jnp.concatenate(outs, axis=1)

    @pl.when(n == pl.num_programs(0) - 1)
    def _():
        gn = gn_ref[...]
        gz = proj_scr[:, 3 * D:4 * D]
        for h in range(HEADS):
            L = slice(h * HEAD_DIM, (h + 1) * HEAD_DIM)
            gzh = gz[:, L]
            o_scr[:, L] = _rms(o_scr[:, L], gn) * (gzh * _sigmoid(gzh))
        y_ref[...] = x_ref[...] + _dot(o_scr[...].astype(BF16), wout_ref[...])


def _hgrn_sample(x, gmix, w_in, w_out, gnorm, lb, state):
    N, D = x.shape
    const = lambda shape: pl.BlockSpec(shape, lambda n: (0,) * len(shape))
    sblk = pl.BlockSpec((1, HEADS, HEAD_DIM, HEAD_DIM), lambda n: (n, 0, 0, 0))
    return pl.pallas_call(
        _hgrn_sample_kernel,
        grid=(N,),
        in_specs=[const((N, D)), const((1, D)), const((D, 4 * D)), const((D, D)),
                  const((1, HEAD_DIM)), const((1, D)), sblk],
        out_specs=[const((N, D)), sblk],
        out_shape=[jax.ShapeDtypeStruct((N, D), F32),
                   jax.ShapeDtypeStruct((N, HEADS, HEAD_DIM, HEAD_DIM), F32)],
        scratch_shapes=[pltpu.VMEM((N, 4 * D), F32), pltpu.VMEM((N, D), F32)],
        compiler_params=pltpu.CompilerParams(
            dimension_semantics=("arbitrary",), vmem_limit_bytes=VMEM_LIMIT),
        name="hgrn_sample",
    )(x, gmix, w_in, w_out, gnorm, lb, state)


CONV_PAD = 32


def _conv_tail(y, xc, bdw_ref, lng_ref, lnb_ref, pw2_ref, bpw2_ref):
    y = y + bdw_ref[...]
    mu = jnp.mean(y, axis=-1, keepdims=True)
    yc = y - mu
    var = jnp.mean(yc * yc, axis=-1, keepdims=True)
    yn = yc * lax.rsqrt(var + EPS) * lng_ref[...] + lnb_ref[...]
    z = yn * _sigmoid(yn)
    return xc + _dot(z.astype(BF16), pw2_ref[...]) + bpw2_ref[...]


def _glu(xc, gmix_ref, pw1_ref, bpw1_ref):
    D = xc.shape[1]
    hn = _rms(xc, gmix_ref[...])
    h2 = _dot(hn.astype(BF16), pw1_ref[...]) + bpw1_ref[...]
    return h2[:, 0:D] * _sigmoid(h2[:, D:2 * D])


def _conv_prompt_kernel(x_ref, gmix_ref, pw1_ref, bpw1_ref, wdw_ref, bdw_ref, lng_ref, lnb_ref,
                        pw2_ref, bpw2_ref, y_ref, buf_ref, full_scr, sh_scr, y_scr):
    C = x_ref.shape[1]
    D = x_ref.shape[2]
    ci = pl.program_id(1)
    hist = CONV_WIDTH - 1
    off = CONV_PAD - hist
    sh_rows = sh_scr.shape[1]

    @pl.when(ci == 0)
    def _():
        full_scr[0:CONV_PAD, :] = jnp.zeros((CONV_PAD, full_scr.shape[1]), F32)

    xc = x_ref[0]
    full_scr[CONV_PAD:CONV_PAD + C, :] = _glu(xc, gmix_ref, pw1_ref, bpw1_ref)

    def lane_chunk(c, carry):
        lanes = pl.ds(pl.multiple_of(c * LANES, LANES), LANES)
        for b in range(1, 8):
            sh_scr[b - 1, :, lanes] = full_scr[b:b + sh_rows, lanes]
        w = [jnp.broadcast_to(wdw_ref[j:j + 1, lanes], (8, LANES)) for j in range(CONV_WIDTH)]
        for r0 in range(0, C, 8):
            acc = None
            for j in range(CONV_WIDTH):
                a, b = divmod(off + j, 8)
                rows = slice(8 * a + r0, 8 * a + r0 + 8)
                src = full_scr[rows, lanes] if b == 0 else sh_scr[b - 1, rows, lanes]
                acc = src * w[j] if acc is None else acc + src * w[j]
            y_scr[r0:r0 + 8, lanes] = acc
        return carry

    lax.fori_loop(0, D // LANES, lane_chunk, 0)
    y_ref[0] = _conv_tail(y_scr[...], xc, bdw_ref, lng_ref, lnb_ref, pw2_ref, bpw2_ref)

    @pl.when(ci == pl.num_programs(1) - 1)
    def _():
        buf_ref[0] = full_scr[C + off:C + CONV_PAD, :]

    full_scr[0:CONV_PAD, :] = full_scr[C:C + CONV_PAD, :]


def _conv_prompt(x, gmix, pw1, bpw1, wdw, bdw, lng, lnb, pw2, bpw2, chunk):
    B, T, D = x.shape
    C = min(chunk, T)
    assert T % C == 0 and C >= CONV_PAD
    hist = CONV_WIDTH - 1
    const = lambda shape: pl.BlockSpec(shape, lambda b, c: (0,) * len(shape))
    return pl.pallas_call(
        _conv_prompt_kernel,
        grid=(B, T // C),
        in_specs=[pl.BlockSpec((1, C, D), lambda b, c: (b, c, 0)),
                  const((1, D)), const((D, 2 * D)), const((1, 2 * D)), const((CONV_WIDTH, D)),
                  const((1, D)), const((1, D)), const((1, D)), const((D, D)), const((1, D))],
        out_specs=[pl.BlockSpec((1, C, D), lambda b, c: (b, c, 0)),
                   pl.BlockSpec((1, hist, D), lambda b, c: (b, 0, 0))],
        out_shape=[jax.ShapeDtypeStruct((B, T, D), F32),
                   jax.ShapeDtypeStruct((B, hist, D), F32)],
        scratch_shapes=[pltpu.VMEM((CONV_PAD + C, D), F32),
                        pltpu.VMEM((7, C + CONV_PAD - 8, D), F32),
                        pltpu.VMEM((C, D), F32)],
        compiler_params=pltpu.CompilerParams(
            dimension_semantics=("arbitrary", "arbitrary"), vmem_limit_bytes=VMEM_LIMIT),
        name="conv_prompt",
    )(x, gmix, pw1, bpw1, wdw, bdw, lng, lnb, pw2, bpw2)


def _conv_sample_kernel(x_ref, gmix_ref, pw1_ref, bpw1_ref, wdw_ref, bdw_ref, lng_ref, lnb_ref,
                        pw2_ref, bpw2_ref, buf_ref, y_ref, nbuf_ref, u_scr, y_scr):
    nt = x_ref.shape[0]
    hist = CONV_WIDTH - 1
    xc = x_ref[...]
    u_scr[...] = _glu(xc, gmix_ref, pw1_ref, bpw1_ref)
    w_hist = wdw_ref[0:hist, :]
    w_last = wdw_ref[hist:hist + 1, :]

    def body(n, carry):
        buf = buf_ref[n]
        u = u_scr[pl.ds(n, 1), :]
        y_scr[pl.ds(n, 1), :] = jnp.sum(buf * w_hist, axis=0, keepdims=True) + u * w_last
        nbuf_ref[n, 0:hist - 1, :] = buf[1:hist, :]
        nbuf_ref[n, hist - 1:hist, :] = u
        return carry

    lax.fori_loop(0, nt, body, 0)
    y_ref[...] = _conv_tail(y_scr[...], xc, bdw_ref, lng_ref, lnb_ref, pw2_ref, bpw2_ref)


def _conv_sample(x, gmix, pw1, bpw1, wdw, bdw, lng, lnb, pw2, bpw2, buf, block):
    N, D = x.shape
    nt = min(block, N)
    assert N % nt == 0
    hist = CONV_WIDTH - 1
    const = lambda shape: pl.BlockSpec(shape, lambda i: (0,) * len(shape))
    return pl.pallas_call(
        _conv_sample_kernel,
        grid=(N // nt,),
        in_specs=[pl.BlockSpec((nt, D), lambda i: (i, 0)),
                  const((1, D)), const((D, 2 * D)), const((1, 2 * D)), const((CONV_WIDTH, D)),
                  const((1, D)), const((1, D)), const((1, D)), const((D, D)), const((1, D)),
                  pl.BlockSpec((nt, hist, D), lambda i: (i, 0, 0))],
        out_specs=[pl.BlockSpec((nt, D), lambda i: (i, 0)),
                   pl.BlockSpec((nt, hist, D), lambda i: (i, 0, 0))],
        out_shape=[jax.ShapeDtypeStruct((N, D), F32),
                   jax.ShapeDtypeStruct((N, hist, D), F32)],
        scratch_shapes=[pltpu.VMEM((nt, D), F32), pltpu.VMEM((nt, D), F32)],
        compiler_params=pltpu.CompilerParams(
            dimension_semantics=("arbitrary",), vmem_limit_bytes=VMEM_LIMIT),
        name="conv_sample",
    )(x, gmix, pw1, bpw1, wdw, bdw, lng, lnb, pw2, bpw2, buf)


def _topk_ranks(vals, ids, k):
    rank = jnp.full(vals.shape, float(k), F32)
    tops = []
    for r in range(k):
        m = jnp.max(vals, axis=0, keepdims=True)
        first = jnp.min(jnp.where(vals == m, ids, 1e9), axis=0, keepdims=True)
        sel = ids == first
        vals = jnp.where(sel, NEG_INF, vals)
        rank = jnp.where(sel, float(r), rank)
        tops.append(m)
    return rank, tops


INT_MIN = -2 ** 31


def _sort_key(s):
    b = lax.bitcast_convert_type(jnp.where(s == 0.0, 0.0, s), jnp.int32)
    return b ^ (lax.shift_right_arithmetic(b, 31) & jnp.int32(0x7FFFFFFF))


def _key_value(k):
    b = k ^ (lax.shift_right_arithmetic(k, 31) & jnp.int32(0x7FFFFFFF))
    return lax.bitcast_convert_type(b, F32)


def _topk_keys(keys, k):
    tops = []
    for r in range(k):
        m = jnp.max(keys, axis=0, keepdims=True)
        keys = jnp.where(keys == m, jnp.int32(INT_MIN + r), keys)
        tops.append(m)
    return keys, tops


def _selected(keys, k):
    sel = keys < jnp.int32(INT_MIN + k)
    n = jnp.sum(jnp.where(sel, 1.0, 0.0), axis=0, keepdims=True)
    return sel, jnp.where(n != float(k), 1.0, 0.0)


def _dup_bf16(x):
    u = lax.bitcast_convert_type(x.astype(BF16).astype(F32), jnp.int32)
    return u | lax.shift_right_logical(u, 16)


def _router_kernel(x_ref, g_ref, wq_ref, keys_ref,
                   xnt_ref, rank2_ref, e2_ref, c1_ref, e1_ref,
                   s_scr, rank_scr, top_scr, miss_scr):
    tn = x_ref.shape[0]
    xn = _rms(x_ref[...], g_ref[...])
    xnt = xn.T.astype(BF16)
    xnt_ref[...] = xnt
    qt = _dot(wq_ref[...], xnt).astype(BF16)
    for hp in range(2 * HEADS):
        s_scr[hp] = _dot(keys_ref[hp], qt[hp * HEAD_DIM:(hp + 1) * HEAD_DIM, :])

    outs = (rank2_ref, e2_ref, c1_ref, e1_ref)
    scr = (s_scr, rank_scr, top_scr)
    _select_experts(tn, scr, outs, miss_scr)


def _run_with_tie_fallback(n, per_group, problem, miss_scr):
    def fast(g, carry):
        miss = [problem(g * per_group + c, False) for c in range(per_group)]
        miss_scr[pl.ds(g, 1), :] = functools.reduce(jnp.maximum, miss)
        return carry

    lax.fori_loop(0, n // per_group, fast, 0, unroll=2 if per_group == 1 else 1)

    def fix(g, carry):
        @pl.when(jnp.max(miss_scr[pl.ds(g, 1), :]) > 0.0)
        def _():
            for c in range(per_group):
                problem(g * per_group + c, True)
        return carry

    lax.fori_loop(0, n // per_group, fix, 0)


def _select_experts(tn, scr, outs, miss_scr):
    s_scr, rank_scr, top_scr = scr
    rank2_ref, e2_ref, c1_ref, e1_ref = outs
    nch = tn // LANES
    key_ids = lax.broadcasted_iota(jnp.int32, (HEAD_DIM, LANES), 0).astype(F32)

    def half_problem(it, exact):
        hp = it // nch
        lanes = pl.ds(pl.multiple_of((it % nch) * LANES, LANES), LANES)
        s = s_scr[hp, :, lanes]
        miss = None
        if exact:
            rank, tops = _topk_ranks(s, key_ids, TOPK)
            top = jnp.concatenate(tops, axis=0)
        else:
            keys, tops = _topk_keys(_sort_key(s), TOPK)
            sel, miss = _selected(keys, TOPK)
            rank = jnp.where(sel, (keys ^ jnp.int32(INT_MIN)).astype(F32), float(TOPK))
            top = _key_value(jnp.concatenate(tops, axis=0))
        rank_scr[hp, :, lanes] = rank
        top_scr[hp, :, lanes] = top
        return miss

    _run_with_tie_fallback(2 * HEADS * nch, nch, half_problem, miss_scr)

    row16 = lax.broadcasted_iota(jnp.int32, (16, LANES), 0)
    row8 = lax.broadcasted_iota(jnp.int32, (8, LANES), 0)
    r16 = row16.astype(F32)
    r8 = row8.astype(F32)
    groups_valid = [row16 >= 0, row8 >= 0, row16 >= 2, row8 >= 2,
                    (row8 >= 2) & (row8 <= 4), (row8 >= 2) & (row8 <= 3), row8 == 2]
    groups_ids = [r16, 16.0 + r8, r16 * 16.0, r8 * 16.0 + 1.0, 32.0 + r8, 48.0 + r8, 64.0 + r8]
    cand_ids = jnp.concatenate(
        [jnp.where(v, i, 1e9) for v, i in zip(groups_valid, groups_ids)], axis=0)

    def head_problem(it, exact):
        h = it // nch
        lanes = pl.ds(pl.multiple_of((it % nch) * LANES, LANES), LANES)
        v1 = top_scr[2 * h, :, lanes]
        v2 = top_scr[2 * h + 1, :, lanes]
        groups = [v1[0:1] + v2, v1[1:2] + v2[0:8], v1 + v2[0:1], v1[0:8] + v2[1:2],
                  v1[2:3] + v2[0:8], v1[3:4] + v2[0:8], v1[4:5] + v2[0:8]]
        cand = jnp.concatenate(
            [jnp.where(v, g, NEG_INF) for v, g in zip(groups_valid, groups)], axis=0)
        miss = None
        if exact:
            crank, _ = _topk_ranks(cand, cand_ids, TOPK)
            sel = crank < float(TOPK)
        else:
            ckeys, _ = _topk_keys(_sort_key(cand), TOPK)
            sel, miss = _selected(ckeys, TOPK)
        z = jnp.sum(jnp.where(sel, jnp.exp(jnp.where(sel, cand - cand[0:1], 0.0)), 0.0),
                    axis=0, keepdims=True)
        self32 = jnp.where(sel, 1.0, 0.0)
        cnt = lambda a, b: jnp.sum(self32[a:b], axis=0, keepdims=True)
        per_r1 = (self32[24:40]
                  + jnp.concatenate([self32[40:48], jnp.zeros((8, LANES), F32)], axis=0)
                  + jnp.where(row16 == 0, cnt(0, 16), 0.0)
                  + jnp.where(row16 == 1, cnt(16, 24), 0.0)
                  + jnp.where(row16 == 2, cnt(48, 56), 0.0)
                  + jnp.where(row16 == 3, cnt(56, 64), 0.0)
                  + jnp.where(row16 == 4, cnt(64, 72), 0.0))
        rank1 = rank_scr[2 * h, :, lanes]
        rank2 = rank_scr[2 * h + 1, :, lanes]
        c1 = jnp.zeros((HEAD_DIM, LANES), F32)
        for j in range(1, TOPK // 3 + 1):
            reach = jnp.sum(jnp.where(per_r1 >= float(j), 1.0, 0.0), axis=0, keepdims=True)
            c1 = c1 + jnp.where(rank1 < reach, 1.0, 0.0)
        c1 = jnp.where(rank1 == 0.0, per_r1[0:1], jnp.where(rank1 == 1.0, per_r1[1:2], c1))
        s1 = s_scr[2 * h, :, lanes]
        s2 = s_scr[2 * h + 1, :, lanes]
        in1 = rank1 < float(TOPK)
        in2 = rank2 < float(TOPK)
        e1 = jnp.where(in1, jnp.exp(jnp.where(in1, s1 - v1[0:1], 0.0)), 0.0)
        e2 = jnp.where(in2, jnp.exp(jnp.where(in2, s2 - v2[0:1], 0.0)), 0.0) / z
        rank2_ref[h, :, lanes] = rank2.astype(BF16)
        e2_ref[h, :, lanes] = (0.5 * e2).astype(BF16)
        c1_ref[h, :, lanes] = _dup_bf16(c1)
        e1_ref[h, :, lanes] = _dup_bf16(e1)
        return miss

    _run_with_tie_fallback(HEADS * nch, nch, head_problem, miss_scr)


def _router(x, g, wq_t, keys, tn):
    N, D = x.shape
    assert N % tn == 0 and tn % LANES == 0
    const = lambda shape: pl.BlockSpec(shape, lambda i: (0,) * len(shape))
    gate_blk = pl.BlockSpec((HEADS, HEAD_DIM, tn), lambda i: (0, 0, i))
    gate_shape = lambda dt: jax.ShapeDtypeStruct((HEADS, HEAD_DIM, N), dt)
    return pl.pallas_call(
        _router_kernel,
        grid=(N // tn,),
        in_specs=[pl.BlockSpec((tn, D), lambda i: (i, 0)), const((1, D)),
                  const(wq_t.shape), const(keys.shape)],
        out_specs=[pl.BlockSpec((D, tn), lambda i: (0, i)),
                   gate_blk, gate_blk, gate_blk, gate_blk],
        out_shape=[jax.ShapeDtypeStruct((D, N), BF16),
                   gate_shape(BF16), gate_shape(BF16), gate_shape(jnp.int32),
                   gate_shape(jnp.int32)],
        scratch_shapes=[pltpu.VMEM((2 * HEADS, HEAD_DIM, tn), F32),
                        pltpu.VMEM((2 * HEADS, HEAD_DIM, tn), F32),
                        pltpu.VMEM((2 * HEADS, TOPK, tn), F32),
                        pltpu.VMEM((2 * HEADS * (tn // LANES), LANES), F32)],
        compiler_params=pltpu.CompilerParams(
            dimension_semantics=("arbitrary",), vmem_limit_bytes=VMEM_LIMIT),
        name="peer_router",
    )(x, g, wq_t, keys)


ROW_TILE = 16
MXU_TILE = 256


def _peer_dense_kernel(xnt_ref, u_ref, vt_ref, rank2_ref, e2_ref, c1_ref, e1_ref, x_ref, gfin_ref,
                       y_ref, hid0, hid1, g_scr, acc_scr, key2_scr, *, final_norm):
    t = pl.program_id(1)
    te, tn = hid0.shape
    nb = te // HEAD_DIM

    def key2_rows(ch, rt, slot):
        tile = (ch * (HEAD_DIM // ROW_TILE) + rt) * 2 * HEADS + slot
        return slice(tile * ROW_TILE, (tile + 1) * ROW_TILE)

    @pl.when(t == 0)
    def _():
        for ref in (hid0, hid1, acc_scr):
            ref[...] = jnp.zeros_like(ref)
        for ch in range(tn // LANES):
            lanes = slice(ch * LANES, (ch + 1) * LANES)
            for rt in range(HEAD_DIM // ROW_TILE):
                keys2 = slice(rt * ROW_TILE, (rt + 1) * ROW_TILE)
                for h in range(HEADS):
                    key2_scr[key2_rows(ch, rt, h), :] = rank2_ref[h, keys2, lanes]
                    key2_scr[key2_rows(ch, rt, HEADS + h), :] = e2_ref[h, keys2, lanes]

    def gate_group(hid_cur, i1, il, ch):
        lanes = slice(ch * LANES, (ch + 1) * LANES)
        bcast = lambda ref, h: pltpu.bitcast(
            jnp.broadcast_to(ref[h, i1:i1 + 1, lanes], (8, LANES)), BF16)
        c1 = [bcast(c1_ref, h) for h in range(HEADS)]
        e1 = [bcast(e1_ref, h) for h in range(HEADS)]
        for rt in range(HEAD_DIM // ROW_TILE):
            r0 = rt * ROW_TILE
            rows = slice(il * HEAD_DIM + r0, il * HEAD_DIM + r0 + ROW_TILE)
            w = jnp.zeros((ROW_TILE, LANES), BF16)
            for h in range(HEADS):
                w = w + jnp.where(key2_scr[key2_rows(ch, rt, h), :] < c1[h],
                                  key2_scr[key2_rows(ch, rt, HEADS + h), :] * e1[h],
                                  jnp.zeros_like(w))
            hid = hid_cur[rows, lanes]
            gelu2 = hid * (1.0 + lax.erf(hid * 0.7071067811865476))
            g_scr[rows, lanes] = w * gelu2.astype(BF16)

    def stage(hid_new, hid_cur, i1_base):
        width = min(MXU_TILE, tn)
        depth = min(MXU_TILE, te)
        for k0 in range(0, te, depth):
            for n0 in range(0, tn, width):
                hid_new[k0:k0 + depth, n0:n0 + width] = _dot(u_ref[k0:k0 + depth, :],
                                                             xnt_ref[:, n0:n0 + width])
                for il in range(k0 // HEAD_DIM, (k0 + depth) // HEAD_DIM):
                    for ch in range(n0 // LANES, (n0 + width) // LANES):
                        gate_group(hid_cur, i1_base + il, il, ch)
                acc_scr[:, n0:n0 + width] += _dot(vt_ref[0, :, k0:k0 + depth],
                                                  g_scr[k0:k0 + depth, n0:n0 + width])

    group = c1_ref.shape[1]

    @pl.when(t % 2 == 0)
    def _():
        stage(hid0, hid1, nb % group)

    @pl.when(t % 2 == 1)
    def _():
        stage(hid1, hid0, 0)

    @pl.when(t == pl.num_programs(1) - 1)
    def _():
        y = x_ref[...] + acc_scr[...].T
        if final_norm:
            y = _rms(y, gfin_ref[...])
        y_ref[...] = y


def _peer_dense(xnt, u, vt, rank2, e2, c1, e1, x, gfin, tn, final_norm):
    N, D = x.shape
    nblk, _, te = vt.shape
    assert N % tn == 0 and u.shape[0] == nblk * te and te % HEAD_DIM == 0
    nb = te // HEAD_DIM
    group = max(8, nb)
    assert group // nb <= 2 and HEAD_DIM % group == 0
    key2_blk = pl.BlockSpec((HEADS, HEAD_DIM, tn), lambda i, t: (0, 0, i))
    key1_blk = pl.BlockSpec(
        (HEADS, group, tn),
        lambda i, t: (0, jnp.clip(((t - 1) * nb) // group, 0, HEAD_DIM // group - 1), i))
    return pl.pallas_call(
        functools.partial(_peer_dense_kernel, final_norm=final_norm),
        grid=(N // tn, nblk + 1),
        in_specs=[pl.BlockSpec((D, tn), lambda i, t: (0, i)),
                  pl.BlockSpec((te, D), lambda i, t: (jnp.minimum(t, nblk - 1), 0)),
                  pl.BlockSpec((1, D, te), lambda i, t: (jnp.maximum(t - 1, 0), 0, 0)),
                  key2_blk, key2_blk, key1_blk, key1_blk,
                  pl.BlockSpec((tn, D), lambda i, t: (i, 0)),
                  pl.BlockSpec((1, D), lambda i, t: (0, 0))],
        out_specs=pl.BlockSpec((tn, D), lambda i, t: (i, 0)),
        out_shape=jax.ShapeDtypeStruct((N, D), F32),
        scratch_shapes=[pltpu.VMEM((te, tn), F32), pltpu.VMEM((te, tn), F32),
                        pltpu.VMEM((te, tn), BF16),
                        pltpu.VMEM((D, tn), F32),
                        pltpu.VMEM((2 * HEADS * HEAD_DIM * (tn // LANES), LANES), BF16)],
        compiler_params=pltpu.CompilerParams(
            dimension_semantics=("arbitrary", "arbitrary"), vmem_limit_bytes=VMEM_LIMIT),
        name="peer_dense",
    )(xnt, u, vt, rank2, e2, c1, e1, x, gfin)


def _peer(x, g, wq_t, keys, u, vt, gfin, tn_router, tn_dense, final_norm):
    xnt, rank2, e2, c1, e1 = _router(x, g, wq_t, keys, tn_router)
    return _peer_dense(xnt, u, vt, rank2, e2, c1, e1, x, gfin, tn_dense, final_norm)


HGRN_CHUNK = 64
HGRN_SEQS = 2
CONV_CHUNK = 256
CONV_SAMPLE_BLOCK = 32
PEER_ROUTER_TOKENS = 512
PEER_DENSE_TOKENS = 512
PEER_EXPERTS = 512


def kernel(x_prompt, x_sample, state_hgrn, state_conv, norm_mix, norm_ffn, norm_final, hg_w_in, hg_w_out, hg_gnorm, hg_lb_logits, cv_w_pw1, cv_b_pw1, cv_w_dw, cv_b_dw, cv_ln_g, cv_ln_b, cv_w_pw2, cv_b_pw2, peer_w_q, peer_sub_keys, peer_u, peer_v):
    B, T, D = x_prompt.shape
    NS = x_sample.shape[0]
    depth = norm_mix.shape[0]
    row = lambda a: a.reshape(1, -1)

    lb_all = jnp.cumsum(jax.nn.softmax(hg_lb_logits.astype(F32), axis=0), axis=0)

    xp = x_prompt
    xs = x_sample.reshape(NS, D)
    hg_p, hg_s, cv_p, cv_s = [], [], [], []
    for i in range(depth):
        j = i // 2
        gmix = row(norm_mix[i])
        if i % 2 == 0:
            w_in = hg_w_in[j].astype(BF16)
            w_out = hg_w_out[j].astype(BF16)
            gn = row(hg_gnorm[j])
            lb = row(lb_all[i])
            xp, sp = _hgrn_prompt(xp, gmix, w_in, w_out, gn, lb, HGRN_CHUNK, HGRN_SEQS)
            xs, ss = _hgrn_sample(xs, gmix, w_in, w_out, gn, lb, state_hgrn[j])
            hg_p.append(sp)
            hg_s.append(ss)
        else:
            cw = (cv_w_pw1[j].astype(BF16), row(cv_b_pw1[j]), cv_w_dw[j], row(cv_b_dw[j]),
                  row(cv_ln_g[j]), row(cv_ln_b[j]), cv_w_pw2[j].astype(BF16), row(cv_b_pw2[j]))
            xp, bp = _conv_prompt(xp, gmix, *cw, CONV_CHUNK)
            xs, bs = _conv_sample(xs, gmix, *cw, state_conv[j], CONV_SAMPLE_BLOCK)
            cv_p.append(bp)
            cv_s.append(bs)
        wq_t = peer_w_q[i].T.astype(BF16)
        keys = peer_sub_keys[i].reshape(2 * HEADS, HEAD_DIM, HEAD_DIM).astype(BF16)
        u = peer_u[i].astype(BF16)
        vt = peer_v[i].reshape(-1, PEER_EXPERTS, D).transpose(0, 2, 1).astype(BF16)
        gffn = row(norm_ffn[i])
        gfin = row(norm_final)
        last = i == depth - 1
        xp = _peer(xp.reshape(B * T, D), gffn, wq_t, keys, u, vt, gfin,
                   min(PEER_ROUTER_TOKENS, B * T), min(PEER_DENSE_TOKENS, B * T),
                   last).reshape(B, T, D)
        xs = _peer(xs, gffn, wq_t, keys, u, vt, gfin, min(LANES, NS), min(LANES, NS), last)
    return (xp, xs.reshape(NS, 1, D), jnp.stack(hg_p), jnp.stack(hg_s),
            jnp.stack(cv_p), jnp.stack(cv_s))
```

```python
import functools

import jax
import jax.numpy as jnp
from jax import lax
from jax.experimental import pallas as pl
from jax.experimental.pallas import tpu as pltpu

F32 = jnp.float32
BF16 = jnp.bfloat16
EPS = 1e-6

HEADS = 8
HEAD_DIM = 128
TOPK = 16
CONV_WIDTH = 31
LANES = 128
SUB = 16
VMEM_LIMIT = 56 * 1024 * 1024

NEG_INF = float("-inf")


def _rms(x, g):
    return x * lax.rsqrt(jnp.mean(x * x, axis=-1, keepdims=True) + EPS) * g


def _sigmoid(x):
    return 1.0 / (1.0 + jnp.exp(-x))


def _dot(a, b):
    return jnp.dot(a, b, preferred_element_type=F32)


def _dot_nt(a, b):
    return lax.dot_general(a, b, (((1,), (1,)), ((), ())), preferred_element_type=F32)


def _hgrn_prompt_kernel(x_ref, gmix_ref, win_ref, wout_ref, gn_ref, lb_ref, tril_ref, ones_ref,
                        y_ref, s_ref,
                        st_scr, q_scr, k_scr, v_scr, b_scr, p_scr, a_scr, o_scr):
    R, C, D = x_ref.shape
    M = R * C
    nsub = C // SUB
    ci = pl.program_id(1)

    @pl.when(ci == 0)
    def _():
        st_scr[...] = jnp.zeros_like(st_scr)

    xc = x_ref[...].reshape(M, D)
    hn = _rms(xc, gmix_ref[...])
    proj = _dot(hn.astype(BF16), win_ref[...])
    qz = proj[:, 0:D]
    fz = proj[:, D:2 * D]
    gz = proj[:, 3 * D:4 * D]
    lb = lb_ref[...]
    f = lb + (1.0 - lb) * _sigmoid(fz)
    q_scr[...] = qz * _sigmoid(qz)
    k_scr[...] = 1.0 - f
    v_scr[...] = proj[:, 2 * D:3 * D]
    lf = jnp.log(f)
    hi = lf.astype(BF16)
    rem = lf - hi.astype(F32)
    mid = rem.astype(BF16)
    lo = (rem - mid.astype(F32)).astype(BF16)
    tril = tril_ref[...]
    b_scr[...] = _dot(tril, hi) + _dot(tril, mid) + _dot(tril, lo)

    qe = (q_scr[...] * jnp.exp(b_scr[...])).astype(BF16)
    for r in range(R):
        for h in range(HEADS):
            L = slice(h * HEAD_DIM, (h + 1) * HEAD_DIM)
            o_scr[r * C:(r + 1) * C, L] = _dot_nt(qe[r * C:(r + 1) * C, L],
                                                  st_scr[r * HEADS + h].astype(BF16))

    t_iota = lax.broadcasted_iota(jnp.int32, (SUB, D), 0)
    for r in range(R):
        base = r * C
        for I in range(nsub):
            rows = slice(base + SUB * I, base + SUB * (I + 1))
            prev = slice(base, base + SUB * I)
            bI = b_scr[rows, :]
            qI = q_scr[rows, :]
            if I > 0:
                beta = b_scr[base + SUB * I - 1:base + SUB * I, :]
                qs = (qI * jnp.exp(bI - beta)).astype(BF16)
                ks = (k_scr[prev, :] * jnp.exp(beta - b_scr[prev, :])).astype(BF16)
                vs = v_scr[prev, :].astype(BF16)
                for h in range(HEADS):
                    L = slice(h * HEAD_DIM, (h + 1) * HEAD_DIM)
                    att = _dot_nt(qs[:, L], ks[:, L])
                    o_scr[rows, L] += _dot(att.astype(BF16), vs[:, L])
            for s in range(SUB):
                src = base + SUB * I + s
                keep = t_iota >= s
                dec = jnp.exp(jnp.where(keep, bI - b_scr[src:src + 1, :], 0.0))
                p = jnp.where(keep, qI * dec * k_scr[src:src + 1, :], 0.0)
                p_scr[src * SUB:(src + 1) * SUB, :] = p.astype(BF16)

    for h in range(HEADS):
        L = slice(h * HEAD_DIM, (h + 1) * HEAD_DIM)
        a_scr[:, L] = _dot(p_scr[:, L], ones_ref[...])
    for J in range(M // SUB):
        rows = slice(SUB * J, SUB * (J + 1))
        acc = o_scr[rows, :]
        for s in range(SUB):
            src = SUB * J + s
            acc = acc + a_scr[src * SUB:(src + 1) * SUB, :] * v_scr[src:src + 1, :]
        o_scr[rows, :] = acc

    for r in range(R):
        rows = slice(r * C, (r + 1) * C)
        b_last = b_scr[(r + 1) * C - 1:(r + 1) * C, :]
        kd = (k_scr[rows, :] * jnp.exp(b_last - b_scr[rows, :])).astype(BF16)
        eb_last = jnp.exp(b_last)
        for h in range(HEADS):
            L = slice(h * HEAD_DIM, (h + 1) * HEAD_DIM)
            vt = v_scr[rows, L].T.astype(BF16)
            st_scr[r * HEADS + h] = st_scr[r * HEADS + h] * eb_last[:, L] + _dot(vt, kd[:, L])

    gn = gn_ref[...]
    for h in range(HEADS):
        L = slice(h * HEAD_DIM, (h + 1) * HEAD_DIM)
        oh = o_scr[:, L]
        gzh = gz[:, L]
        o_scr[:, L] = _rms(oh, gn) * (gzh * _sigmoid(gzh))
    y_ref[...] = (xc + _dot(o_scr[...].astype(BF16), wout_ref[...])).reshape(R, C, D)

    @pl.when(ci == pl.num_programs(1) - 1)
    def _():
        for r in range(R):
            for h in range(HEADS):
                s_ref[r, h] = st_scr[r * HEADS + h].T


def _hgrn_prompt(x, gmix, w_in, w_out, gnorm, lb, chunk, seqs):
    B, T, D = x.shape
    C = min(chunk, T)
    R = min(seqs, B)
    assert T % C == 0 and C % SUB == 0 and B % R == 0
    M = R * C
    tril = jnp.kron(jnp.eye(R, dtype=F32), jnp.tril(jnp.ones((C, C), F32))).astype(BF16)
    ones = jnp.ones((HEAD_DIM, HEAD_DIM), BF16)
    const = lambda shape: pl.BlockSpec(shape, lambda b, c: (0,) * len(shape))
    return pl.pallas_call(
        _hgrn_prompt_kernel,
        grid=(B // R, T // C),
        in_specs=[
            pl.BlockSpec((R, C, D), lambda b, c: (b, c, 0)),
            const((1, D)), const((D, 4 * D)), const((D, D)), const((1, HEAD_DIM)), const((1, D)),
            const((M, M)), const((HEAD_DIM, HEAD_DIM)),
        ],
        out_specs=[
            pl.BlockSpec((R, C, D), lambda b, c: (b, c, 0)),
            pl.BlockSpec((R, HEADS, HEAD_DIM, HEAD_DIM), lambda b, c: (b, 0, 0, 0)),
        ],
        out_shape=[
            jax.ShapeDtypeStruct((B, T, D), F32),
            jax.ShapeDtypeStruct((B, HEADS, HEAD_DIM, HEAD_DIM), F32),
        ],
        scratch_shapes=[
            pltpu.VMEM((R * HEADS, HEAD_DIM, HEAD_DIM), F32),
            pltpu.VMEM((M, D), F32),
            pltpu.VMEM((M, D), F32),
            pltpu.VMEM((M, D), F32),
            pltpu.VMEM((M, D), F32),
            pltpu.VMEM((M * SUB, D), BF16),
            pltpu.VMEM((M * SUB, D), F32),
            pltpu.VMEM((M, D), F32),
        ],
        compiler_params=pltpu.CompilerParams(
            dimension_semantics=("arbitrary", "arbitrary"), vmem_limit_bytes=VMEM_LIMIT),
        name="hgrn_prompt",
    )(x, gmix, w_in, w_out, gnorm, lb, tril, ones)


def _hgrn_sample_kernel(x_ref, gmix_ref, win_ref, wout_ref, gn_ref, lb_ref, s_ref,
                        y_ref, snew_ref, proj_scr, o_scr):
    n = pl.program_id(0)
    D = x_ref.shape[1]

    @pl.when(n == 0)
    def _():
        hn = _rms(x_ref[...], gmix_ref[...])
        proj_scr[...] = _dot(hn.astype(BF16), win_ref[...])

    row = proj_scr[pl.ds(n, 1), :]
    qz = row[:, 0:D]
    fz = row[:, D:2 * D]
    v = row[:, 2 * D:3 * D]
    lb = lb_ref[...]
    f = lb + (1.0 - lb) * _sigmoid(fz)
    q = qz * _sigmoid(qz)
    k = 1.0 - f

    def cols(r):
        stacked = jnp.concatenate(
            [r[:, h * HEAD_DIM:(h + 1) * HEAD_DIM] for h in range(HEADS)]
            + [jnp.zeros((HEAD_DIM - HEADS, HEAD_DIM), F32)], axis=0)
        return stacked.T

    fc = cols(f)
    kc = cols(k)
    qc = cols(q)
    outs = []
    for h in range(HEADS):
        L = slice(h * HEAD_DIM, (h + 1) * HEAD_DIM)
        s_new = s_ref[0, h] * fc[:, h:h + 1] + kc[:, h:h + 1] * v[:, L]
        snew_ref[0, h] = s_new
        outs.append(jnp.sum(qc[:, h:h + 1] * s_new, axis=0, keepdims=True))
    o_scr[pl.ds(n, 1), :] = jnp.concatenate(outs, axis=1)

    @pl.when(n == pl.num_programs(0) - 1)
    def _():
        gn = gn_ref[...]
        gz = proj_scr[:, 3 * D:4 * D]
        for h in range(HEADS):
            L = slice(h * HEAD_DIM, (h + 1) * HEAD_DIM)
            gzh = gz[:, L]
            o_scr[:, L] = _rms(o_scr[:, L], gn) * (gzh * _sigmoid(gzh))
        y_ref[...] = x_ref[...] + _dot(o_scr[...].astype(BF16), wout_ref[...])


def _hgrn_sample(x, gmix, w_in, w_out, gnorm, lb, state):
    N, D = x.shape
    const = lambda shape: pl.BlockSpec(shape, lambda n: (0,) * len(shape))
    sblk = pl.BlockSpec((1, HEADS, HEAD_DIM, HEAD_DIM), lambda n: (n, 0, 0, 0))
    return pl.pallas_call(
        _hgrn_sample_kernel,
        grid=(N,),
        in_specs=[const((N, D)), const((1, D)), const((D, 4 * D)), const((D, D)),
                  const((1, HEAD_DIM)), const((1, D)), sblk],
        out_specs=[const((N, D)), sblk],
        out_shape=[jax.ShapeDtypeStruct((N, D), F32),
                   jax.ShapeDtypeStruct((N, HEADS, HEAD_DIM, HEAD_DIM), F32)],
        scratch_shapes=[pltpu.VMEM((N, 4 * D), F32), pltpu.VMEM((N, D), F32)],
        compiler_params=pltpu.CompilerParams(
            dimension_semantics=("arbitrary",), vmem_limit_bytes=VMEM_LIMIT),
        name="hgrn_sample",
    )(x, gmix, w_in, w_out, gnorm, lb, state)


CONV_PAD = 32


def _conv_tail(y, xc, bdw_ref, lng_ref, lnb_ref, pw2_ref, bpw2_ref):
    y = y + bdw_ref[...]
    mu = jnp.mean(y, axis=-1, keepdims=True)
    yc = y - mu
    var = jnp.mean(yc * yc, axis=-1, keepdims=True)
    yn = yc * lax.rsqrt(var + EPS) * lng_ref[...] + lnb_ref[...]
    z = yn * _sigmoid(yn)
    return xc + _dot(z.astype(BF16), pw2_ref[...]) + bpw2_ref[...]


def _glu(xc, gmix_ref, pw1_ref, bpw1_ref):
    D = xc.shape[1]
    hn = _rms(xc, gmix_ref[...])
    h2 = _dot(hn.astype(BF16), pw1_ref[...]) + bpw1_ref[...]
    return h2[:, 0:D] * _sigmoid(h2[:, D:2 * D])


def _conv_prompt_kernel(x_ref, gmix_ref, pw1_ref, bpw1_ref, wdw_ref, bdw_ref, lng_ref, lnb_ref,
                        pw2_ref, bpw2_ref, y_ref, buf_ref, full_scr, sh_scr, y_scr):
    C = x_ref.shape[1]
    D = x_ref.shape[2]
    ci = pl.program_id(1)
    hist = CONV_WIDTH - 1
    off = CONV_PAD - hist
    sh_rows = sh_scr.shape[1]

    @pl.when(ci == 0)
    def _():
        full_scr[0:CONV_PAD, :] = jnp.zeros((CONV_PAD, full_scr.shape[1]), F32)

    xc = x_ref[0]
    full_scr[CONV_PAD:CONV_PAD + C, :] = _glu(xc, gmix_ref, pw1_ref, bpw1_ref)

    def lane_chunk(c, carry):
        lanes = pl.ds(pl.multiple_of(c * LANES, LANES), LANES)
        for b in range(1, 8):
            sh_scr[b - 1, :, lanes] = full_scr[b:b + sh_rows, lanes]
        w = [jnp.broadcast_to(wdw_ref[j:j + 1, lanes], (8, LANES)) for j in range(CONV_WIDTH)]
        for r0 in range(0, C, 8):
            acc = None
            for j in range(CONV_WIDTH):
                a, b = divmod(off + j, 8)
                rows = slice(8 * a + r0, 8 * a + r0 + 8)
                src = full_scr[rows, lanes] if b == 0 else sh_scr[b - 1, rows, lanes]
                acc = src * w[j] if acc is None else acc + src * w[j]
            y_scr[r0:r0 + 8, lanes] = acc
        return carry

    lax.fori_loop(0, D // LANES, lane_chunk, 0)
    y_ref[0] = _conv_tail(y_scr[...], xc, bdw_ref, lng_ref, lnb_ref, pw2_ref, bpw2_ref)

    @pl.when(ci == pl.num_programs(1) - 1)
    def _():
        buf_ref[0] = full_scr[C + off:C + CONV_PAD, :]

    full_scr[0:CONV_PAD, :] = full_scr[C:C + CONV_PAD, :]


def _conv_prompt(x, gmix, pw1, bpw1, wdw, bdw, lng, lnb, pw2, bpw2, chunk):
    B, T, D = x.shape
    C = min(chunk, T)
    assert T % C == 0 and C >= CONV_PAD
    hist = CONV_WIDTH - 1
    const = lambda shape: pl.BlockSpec(shape, lambda b, c: (0,) * len(shape))
    return pl.pallas_call(
        _conv_prompt_kernel,
        grid=(B, T // C),
        in_specs=[pl.BlockSpec((1, C, D), lambda b, c: (b, c, 0)),
                  const((1, D)), const((D, 2 * D)), const((1, 2 * D)), const((CONV_WIDTH, D)),
                  const((1, D)), const((1, D)), const((1, D)), const((D, D)), const((1, D))],
        out_specs=[pl.BlockSpec((1, C, D), lambda b, c: (b, c, 0)),
                   pl.BlockSpec((1, hist, D), lambda b, c: (b, 0, 0))],
        out_shape=[jax.ShapeDtypeStruct((B, T, D), F32),
                   jax.ShapeDtypeStruct((B, hist, D), F32)],
        scratch_shapes=[pltpu.VMEM((CONV_PAD + C, D), F32),
                        pltpu.VMEM((7, C + CONV_PAD - 8, D), F32),
                        pltpu.VMEM((C, D), F32)],
        compiler_params=pltpu.CompilerParams(
            dimension_semantics=("arbitrary", "arbitrary"), vmem_limit_bytes=VMEM_LIMIT),
        name="conv_prompt",
    )(x, gmix, pw1, bpw1, wdw, bdw, lng, lnb, pw2, bpw2)


def _conv_sample_kernel(x_ref, gmix_ref, pw1_ref, bpw1_ref, wdw_ref, bdw_ref, lng_ref, lnb_ref,
                        pw2_ref, bpw2_ref, buf_ref, y_ref, nbuf_ref, u_scr, y_scr):
    nt = x_ref.shape[0]
    hist = CONV_WIDTH - 1
    xc = x_ref[...]
    u_scr[...] = _glu(xc, gmix_ref, pw1_ref, bpw1_ref)
    w_hist = wdw_ref[0:hist, :]
    w_last = wdw_ref[hist:hist + 1, :]

    def body(n, carry):
        buf = buf_ref[n]
        u = u_scr[pl.ds(n, 1), :]
        y_scr[pl.ds(n, 1), :] = jnp.sum(buf * w_hist, axis=0, keepdims=True) + u * w_last
        nbuf_ref[n, 0:hist - 1, :] = buf[1:hist, :]
        nbuf_ref[n, hist - 1:hist, :] = u
        return carry

    lax.fori_loop(0, nt, body, 0)
    y_ref[...] = _conv_tail(y_scr[...], xc, bdw_ref, lng_ref, lnb_ref, pw2_ref, bpw2_ref)


def _conv_sample(x, gmix, pw1, bpw1, wdw, bdw, lng, lnb, pw2, bpw2, buf, block):
    N, D = x.shape
    nt = min(block, N)
    assert N % nt == 0
    hist = CONV_WIDTH - 1
    const = lambda shape: pl.BlockSpec(shape, lambda i: (0,) * len(shape))
    return pl.pallas_call(
        _conv_sample_kernel,
        grid=(N // nt,),
        in_specs=[pl.BlockSpec((nt, D), lambda i: (i, 0)),
                  const((1, D)), const((D, 2 * D)), const((1, 2 * D)), const((CONV_WIDTH, D)),
                  const((1, D)), const((1, D)), const((1, D)), const((D, D)), const((1, D)),
                  pl.BlockSpec((nt, hist, D), lambda i: (i, 0, 0))],
        out_specs=[pl.BlockSpec((nt, D), lambda i: (i, 0)),
                   pl.BlockSpec((nt, hist, D), lambda i: (i, 0, 0))],
        out_shape=[jax.ShapeDtypeStruct((N, D), F32),
                   jax.ShapeDtypeStruct((N, hist, D), F32)],
        scratch_shapes=[pltpu.VMEM((nt, D), F32), pltpu.VMEM((nt, D), F32)],
        compiler_params=pltpu.CompilerParams(
            dimension_semantics=("arbitrary",), vmem_limit_bytes=VMEM_LIMIT),
        name="conv_sample",
    )(x, gmix, pw1, bpw1, wdw, bdw, lng, lnb, pw2, bpw2, buf)


def _topk_ranks(vals, ids, k):
    rank = jnp.full(vals.shape, float(k), F32)
    tops = []
    for r in range(k):
        m = jnp.max(vals, axis=0, keepdims=True)
        first = jnp.min(jnp.where(vals == m, ids, 1e9), axis=0, keepdims=True)
        sel = ids == first
        vals = jnp.where(sel, NEG_INF, vals)
        rank = jnp.where(sel, float(r), rank)
        tops.append(m)
    return rank, tops


INT_MIN = -2 ** 31


def _sort_key(s):
    b = lax.bitcast_convert_type(jnp.where(s == 0.0, 0.0, s), jnp.int32)
    return b ^ (lax.shift_right_arithmetic(b, 31) & jnp.int32(0x7FFFFFFF))


def _key_value(k):
    b = k ^ (lax.shift_right_arithmetic(k, 31) & jnp.int32(0x7FFFFFFF))
    return lax.bitcast_convert_type(b, F32)


def _topk_keys(keys, k):
    tops = []
    for r in range(k):
        m = jnp.max(keys, axis=0, keepdims=True)
        keys = jnp.where(keys == m, jnp.int32(INT_MIN + r), keys)
        tops.append(m)
    return keys, tops


def _selected(keys, k):
    sel = keys < jnp.int32(INT_MIN + k)
    n = jnp.sum(jnp.where(sel, 1.0, 0.0), axis=0, keepdims=True)
    return sel, jnp.where(n != float(k), 1.0, 0.0)


def _dup_bf16(x):
    u = lax.bitcast_convert_type(x.astype(BF16).astype(F32), jnp.int32)
    return u | lax.shift_right_logical(u, 16)


def _router_kernel(x_ref, g_ref, wq_ref, keys_ref,
                   xnt_ref, rank2_ref, e2_ref, c1_ref, e1_ref,
                   s_scr, rank_scr, top_scr, miss_scr):
    tn = x_ref.shape[0]
    xn = _rms(x_ref[...], g_ref[...])
    xnt = xn.T.astype(BF16)
    xnt_ref[...] = xnt
    qt = _dot(wq_ref[...], xnt).astype(BF16)
    for hp in range(2 * HEADS):
        s_scr[hp] = _dot(keys_ref[hp], qt[hp * HEAD_DIM:(hp + 1) * HEAD_DIM, :])

    outs = (rank2_ref, e2_ref, c1_ref, e1_ref)
    scr = (s_scr, rank_scr, top_scr)
    _select_experts(tn, scr, outs, miss_scr)


def _run_with_tie_fallback(n, per_group, problem, miss_scr):
    def fast(g, carry):
        miss = [problem(g * per_group + c, False) for c in range(per_group)]
        miss_scr[pl.ds(g, 1), :] = functools.reduce(jnp.maximum, miss)
        return carry

    lax.fori_loop(0, n // per_group, fast, 0, unroll=2 if per_group == 1 else 1)

    def fix(g, carry):
        @pl.when(jnp.max(miss_scr[pl.ds(g, 1), :]) > 0.0)
        def _():
            for c in range(per_group):
                problem(g * per_group + c, True)
        return carry

    lax.fori_loop(0, n // per_group, fix, 0)


def _select_experts(tn, scr, outs, miss_scr):
    s_scr, rank_scr, top_scr = scr
    rank2_ref, e2_ref, c1_ref, e1_ref = outs
    nch = tn // LANES
    key_ids = lax.broadcasted_iota(jnp.int32, (HEAD_DIM, LANES), 0).astype(F32)

    def half_problem(it, exact):
        hp = it // nch
        lanes = pl.ds(pl.multiple_of((it % nch) * LANES, LANES), LANES)
        s = s_scr[hp, :, lanes]
        miss = None
        if exact:
            rank, tops = _topk_ranks(s, key_ids, TOPK)
            top = jnp.concatenate(tops, axis=0)
        else:
            keys, tops = _topk_keys(_sort_key(s), TOPK)
            sel, miss = _selected(keys, TOPK)
            rank = jnp.where(sel, (keys ^ jnp.int32(INT_MIN)).astype(F32), float(TOPK))
            top = _key_value(jnp.concatenate(tops, axis=0))
        rank_scr[hp, :, lanes] = rank
        top_scr[hp, :, lanes] = top
        return miss

    _run_with_tie_fallback(2 * HEADS * nch, nch, half_problem, miss_scr)

    row16 = lax.broadcasted_iota(jnp.int32, (16, LANES), 0)
    row8 = lax.broadcasted_iota(jnp.int32, (8, LANES), 0)
    r16 = row16.astype(F32)
    r8 = row8.astype(F32)
    groups_valid = [row16 >= 0, row8 >= 0, row16 >= 2, row8 >= 2,
                    (row8 >= 2) & (row8 <= 4), (row8 >= 2) & (row8 <= 3), row8 == 2]
    groups_ids = [r16, 16.0 + r8, r16 * 16.0, r8 * 16.0 + 1.0, 32.0 + r8, 48.0 + r8, 64.0 + r8]
    cand_ids = jnp.concatenate(
        [jnp.where(v, i, 1e9) for v, i in zip(groups_valid, groups_ids)], axis=0)

    def head_problem(it, exact):
        h = it // nch
        lanes = pl.ds(pl.multiple_of((it % nch) * LANES, LANES), LANES)
        v1 = top_scr[2 * h, :, lanes]
        v2 = top_scr[2 * h + 1, :, lanes]
        groups = [v1[0:1] + v2, v1[1:2] + v2[0:8], v1 + v2[0:1], v1[0:8] + v2[1:2],
                  v1[2:3] + v2[0:8], v1[3:4] + v2[0:8], v1[4:5] + v2[0:8]]
        cand = jnp.concatenate(
            [jnp.where(v, g, NEG_INF) for v, g in zip(groups_valid, groups)], axis=0)
        miss = None
        if exact:
            crank, _ = _topk_ranks(cand, cand_ids, TOPK)
            sel = crank < float(TOPK)
        else:
            ckeys, _ = _topk_keys(_sort_key(cand), TOPK)
            sel, miss = _selected(ckeys, TOPK)
        z = jnp.sum(jnp.where(sel, jnp.exp(jnp.where(sel, cand - cand[0:1], 0.0)), 0.0),
                    axis=0, keepdims=True)
        self32 = jnp.where(sel, 1.0, 0.0)
        cnt = lambda a, b: jnp.sum(self32[a:b], axis=0, keepdims=True)
        per_r1 = (self32[24:40]
                  + jnp.concatenate([self32[40:48], jnp.zeros((8, LANES), F32)], axis=0)
                  + jnp.where(row16 == 0, cnt(0, 16), 0.0)
                  + jnp.where(row16 == 1, cnt(16, 24), 0.0)
                  + jnp.where(row16 == 2, cnt(48, 56), 0.0)
                  + jnp.where(row16 == 3, cnt(56, 64), 0.0)
                  + jnp.where(row16 == 4, cnt(64, 72), 0.0))
        rank1 = rank_scr[2 * h, :, lanes]
        rank2 = rank_scr[2 * h + 1, :, lanes]
        c1 = jnp.zeros((HEAD_DIM, LANES), F32)
        for j in range(1, TOPK // 3 + 1):
            reach = jnp.sum(jnp.where(per_r1 >= float(j), 1.0, 0.0), axis=0, keepdims=True)
            c1 = c1 + jnp.where(rank1 < reach, 1.0, 0.0)
        c1 = jnp.where(rank1 == 0.0, per_r1[0:1], jnp.where(rank1 == 1.0, per_r1[1:2], c1))
        s1 = s_scr[2 * h, :, lanes]
        s2 = s_scr[2 * h + 1, :, lanes]
        in1 = rank1 < float(TOPK)
        in2 = rank2 < float(TOPK)
        e1 = jnp.where(in1, jnp.exp(jnp.where(in1, s1 - v1[0:1], 0.0)), 0.0)
        e2 = jnp.where(in2, jnp.exp(jnp.where(in2, s2 - v2[0:1], 0.0)), 0.0) / z
        rank2_ref[h, :, lanes] = rank2.astype(BF16)
        e2_ref[h, :, lanes] = (0.5 * e2).astype(BF16)
        c1_ref[h, :, lanes] = _dup_bf16(c1)
        e1_ref[h, :, lanes] = _dup_bf16(e1)
        return miss

    _run_with_tie_fallback(HEADS * nch, nch, head_problem, miss_scr)


def _router(x, g, wq_t, keys, tn):
    N, D = x.shape
    assert N % tn == 0 and tn % LANES == 0
    const = lambda shape: pl.BlockSpec(shape, lambda i: (0,) * len(shape))
    gate_blk = pl.BlockSpec((HEADS, HEAD_DIM, tn), lambda i: (0, 0, i))
    gate_shape = lambda dt: jax.ShapeDtypeStruct((HEADS, HEAD_DIM, N), dt)
    return pl.pallas_call(
        _router_kernel,
        grid=(N // tn,),
        in_specs=[pl.BlockSpec((tn, D), lambda i: (i, 0)), const((1, D)),
                  const(wq_t.shape), const(keys.shape)],
        out_specs=[pl.BlockSpec((D, tn), lambda i: (0, i)),
                   gate_blk, gate_blk, gate_blk, gate_blk],
        out_shape=[jax.ShapeDtypeStruct((D, N), BF16),
                   gate_shape(BF16), gate_shape(BF16), gate_shape(jnp.int32),
                   gate_shape(jnp.int32)],
        scratch_shapes=[pltpu.VMEM((2 * HEADS, HEAD_DIM, tn), F32),
                        pltpu.VMEM((2 * HEADS, HEAD_DIM, tn), F32),
                        pltpu.VMEM((2 * HEADS, TOPK, tn), F32),
                        pltpu.VMEM((2 * HEADS * (tn // LANES), LANES), F32)],
        compiler_params=pltpu.CompilerParams(
            dimension_semantics=("arbitrary",), vmem_limit_bytes=VMEM_LIMIT),
        name="peer_router",
    )(x, g, wq_t, keys)


ROW_TILE = 16
MXU_TILE = 256


def _peer_dense_kernel(xnt_ref, u_ref, vt_ref, rank2_ref, e2_ref, c1_ref, e1_ref, x_ref, gfin_ref,
                       y_ref, hid0, hid1, g_scr, acc_scr, key2_scr, *, final_norm):
    t = pl.program_id(1)
    te, tn = hid0.shape
    nb = te // HEAD_DIM

    def key2_rows(ch, rt, slot):
        tile = (ch * (HEAD_DIM // ROW_TILE) + rt) * 2 * HEADS + slot
        return slice(tile * ROW_TILE, (tile + 1) * ROW_TILE)

    @pl.when(t == 0)
    def _():
        for ref in (hid0, hid1, acc_scr):
            ref[...] = jnp.zeros_like(ref)
        for ch in range(tn // LANES):
            lanes = slice(ch * LANES, (ch + 1) * LANES)
            for rt in range(HEAD_DIM // ROW_TILE):
                keys2 = slice(rt * ROW_TILE, (rt + 1) * ROW_TILE)
                for h in range(HEADS):
                    key2_scr[key2_rows(ch, rt, h), :] = rank2_ref[h, keys2, lanes]
                    key2_scr[key2_rows(ch, rt, HEADS + h), :] = e2_ref[h, keys2, lanes]

    def gate_group(hid_cur, i1, il, ch):
        lanes = slice(ch * LANES, (ch + 1) * LANES)
        bcast = lambda ref, h: pltpu.bitcast(
            jnp.broadcast_to(ref[h, i1:i1 + 1, lanes], (8, LANES)), BF16)
        c1 = [bcast(c1_ref, h) for h in range(HEADS)]
        e1 = [bcast(e1_ref, h) for h in range(HEADS)]
        for rt in range(HEAD_DIM // ROW_TILE):
            r0 = rt * ROW_TILE
            rows = slice(il * HEAD_DIM + r0, il * HEAD_DIM + r0 + ROW_TILE)
            w = jnp.zeros((ROW_TILE, LANES), BF16)
            for h in range(HEADS):
                w = w + jnp.where(key2_scr[key2_rows(ch, rt, h), :] < c1[h],
                                  key2_scr[key2_rows(ch, rt, HEADS + h), :] * e1[h],
                                  jnp.zeros_like(w))
            hid = hid_cur[rows, lanes]
            gelu2 = hid * (1.0 + lax.erf(hid * 0.7071067811865476))
            g_scr[rows, lanes] = w * gelu2.astype(BF16)

    def stage(hid_new, hid_cur, i1_base):
        width = min(MXU_TILE, tn)
        depth = min(MXU_TILE, te)
        for k0 in range(0, te, depth):
            for n0 in range(0, tn, width):
                hid_new[k0:k0 + depth, n0:n0 + width] = _dot(u_ref[0, k0:k0 + depth, :],
                                                             xnt_ref[:, n0:n0 + width])
                for il in range(k0 // HEAD_DIM, (k0 + depth) // HEAD_DIM):
                    for ch in range(n0 // LANES, (n0 + width) // LANES):
                        gate_group(hid_cur, i1_base + il, il, ch)
                acc_scr[:, n0:n0 + width] += _dot(vt_ref[0, 0, :, k0:k0 + depth],
                                                  g_scr[k0:k0 + depth, n0:n0 + width])

    group = c1_ref.shape[1]

    @pl.when(t % 2 == 0)
    def _():
        stage(hid0, hid1, nb % group)

    @pl.when(t % 2 == 1)
    def _():
        stage(hid1, hid0, 0)

    @pl.when(t == pl.num_programs(1) - 1)
    def _():
        y = x_ref[...] + acc_scr[...].T
        if final_norm:
            y = _rms(y, gfin_ref[...])
        y_ref[...] = y


def _peer_dense(xnt, u, vt, layer, rank2, e2, c1, e1, x, gfin, tn, final_norm):
    N, D = x.shape
    _, nblk, _, te = vt.shape
    assert N % tn == 0 and u.shape[1] == nblk * te and te % HEAD_DIM == 0
    nb = te // HEAD_DIM
    group = max(8, nb)
    assert group // nb <= 2 and HEAD_DIM % group == 0
    key2_blk = pl.BlockSpec((HEADS, HEAD_DIM, tn), lambda i, t: (0, 0, i))
    key1_blk = pl.BlockSpec(
        (HEADS, group, tn),
        lambda i, t: (0, jnp.clip(((t - 1) * nb) // group, 0, HEAD_DIM // group - 1), i))
    return pl.pallas_call(
        functools.partial(_peer_dense_kernel, final_norm=final_norm),
        grid=(N // tn, nblk + 1),
        in_specs=[pl.BlockSpec((D, tn), lambda i, t: (0, i)),
                  pl.BlockSpec((1, te, D), lambda i, t: (layer, jnp.minimum(t, nblk - 1), 0)),
                  pl.BlockSpec((1, 1, D, te),
                               lambda i, t: (layer, jnp.maximum(t - 1, 0), 0, 0)),
                  key2_blk, key2_blk, key1_blk, key1_blk,
                  pl.BlockSpec((tn, D), lambda i, t: (i, 0)),
                  pl.BlockSpec((1, D), lambda i, t: (0, 0))],
        out_specs=pl.BlockSpec((tn, D), lambda i, t: (i, 0)),
        out_shape=jax.ShapeDtypeStruct((N, D), F32),
        scratch_shapes=[pltpu.VMEM((te, tn), F32), pltpu.VMEM((te, tn), F32),
                        pltpu.VMEM((te, tn), BF16),
                        pltpu.VMEM((D, tn), F32),
                        pltpu.VMEM((2 * HEADS * HEAD_DIM * (tn // LANES), LANES), BF16)],
        compiler_params=pltpu.CompilerParams(
            dimension_semantics=("arbitrary", "arbitrary"), vmem_limit_bytes=VMEM_LIMIT),
        name="peer_dense",
    )(xnt, u, vt, rank2, e2, c1, e1, x, gfin)


def _peer(x, g, wq_t, keys, u, vt, layer, gfin, tn_router, tn_dense, final_norm):
    xnt, rank2, e2, c1, e1 = _router(x, g, wq_t, keys, tn_router)
    return _peer_dense(xnt, u, vt, layer, rank2, e2, c1, e1, x, gfin, tn_dense, final_norm)


HGRN_CHUNK = 64
HGRN_SEQS = 2
CONV_CHUNK = 256
CONV_SAMPLE_BLOCK = 32
PEER_ROUTER_TOKENS = 512
PEER_DENSE_TOKENS = 512
PEER_EXPERTS = 512


def kernel(x_prompt, x_sample, state_hgrn, state_conv, norm_mix, norm_ffn, norm_final, hg_w_in, hg_w_out, hg_gnorm, hg_lb_logits, cv_w_pw1, cv_b_pw1, cv_w_dw, cv_b_dw, cv_ln_g, cv_ln_b, cv_w_pw2, cv_b_pw2, peer_w_q, peer_sub_keys, peer_u, peer_v):
    B, T, D = x_prompt.shape
    NS = x_sample.shape[0]
    depth = norm_mix.shape[0]
    row = lambda a: a.reshape(1, -1)

    lb_all = jnp.cumsum(jax.nn.softmax(hg_lb_logits.astype(F32), axis=0), axis=0)

    u_all = peer_u.astype(BF16)
    vt_all = peer_v.reshape(depth, -1, PEER_EXPERTS, D).transpose(0, 1, 3, 2).astype(BF16)

    xp = x_prompt
    xs = x_sample.reshape(NS, D)
    hg_p, hg_s, cv_p, cv_s = [], [], [], []
    for i in range(depth):
        j = i // 2
        gmix = row(norm_mix[i])
        if i % 2 == 0:
            w_in = hg_w_in[j].astype(BF16)
            w_out = hg_w_out[j].astype(BF16)
            gn = row(hg_gnorm[j])
            lb = row(lb_all[i])
            xp, sp = _hgrn_prompt(xp, gmix, w_in, w_out, gn, lb, HGRN_CHUNK, HGRN_SEQS)
            xs, ss = _hgrn_sample(xs, gmix, w_in, w_out, gn, lb, state_hgrn[j])
            hg_p.append(sp)
            hg_s.append(ss)
        else:
            cw = (cv_w_pw1[j].astype(BF16), row(cv_b_pw1[j]), cv_w_dw[j], row(cv_b_dw[j]),
                  row(cv_ln_g[j]), row(cv_ln_b[j]), cv_w_pw2[j].astype(BF16), row(cv_b_pw2[j]))
            xp, bp = _conv_prompt(xp, gmix, *cw, CONV_CHUNK)
            xs, bs = _conv_sample(xs, gmix, *cw, state_conv[j], CONV_SAMPLE_BLOCK)
            cv_p.append(bp)
            cv_s.append(bs)
        wq_t = peer_w_q[i].T.astype(BF16)
        keys = peer_sub_keys[i].reshape(2 * HEADS, HEAD_DIM, HEAD_DIM).astype(BF16)
        gffn = row(norm_ffn[i])
        gfin = row(norm_final)
        last = i == depth - 1
        xp = _peer(xp.reshape(B * T, D), gffn, wq_t, keys, u_all, vt_all, i, gfin,
                   min(PEER_ROUTER_TOKENS, B * T), min(PEER_DENSE_TOKENS, B * T),
                   last).reshape(B, T, D)
        xs = _peer(xs, gffn, wq_t, keys, u_all, vt_all, i, gfin, min(LANES, NS), min(LANES, NS),
                   last)
    return (xp, xs.reshape(NS, 1, D), jnp.stack(hg_p), jnp.stack(hg_s),
            jnp.stack(cv_p), jnp.stack(cv_s))
```

```python
import functools

import jax
import jax.numpy as jnp
from jax import lax
from jax.experimental import pallas as pl
from jax.experimental.pallas import tpu as pltpu

F32 = jnp.float32
BF16 = jnp.bfloat16
EPS = 1e-6

HEADS = 8
HEAD_DIM = 128
TOPK = 16
CONV_WIDTH = 31
LANES = 128
SUB = 16
VMEM_LIMIT = 56 * 1024 * 1024

NEG_INF = float("-inf")


def _rms(x, g):
    return x * lax.rsqrt(jnp.mean(x * x, axis=-1, keepdims=True) + EPS) * g


def _sigmoid(x):
    return 1.0 / (1.0 + jnp.exp(-x))


def _dot(a, b):
    return jnp.dot(a, b, preferred_element_type=F32)


def _dot_nt(a, b):
    return lax.dot_general(a, b, (((1,), (1,)), ((), ())), preferred_element_type=F32)


def _hgrn_prompt_kernel(x_ref, gmix_ref, win_ref, wout_ref, gn_ref, lb_ref, tril_ref, ones_ref,
                        y_ref, s_ref,
                        st_scr, q_scr, k_scr, v_scr, b_scr, p_scr, a_scr, o_scr):
    R, C, D = x_ref.shape
    M = R * C
    nsub = C // SUB
    ci = pl.program_id(1)

    @pl.when(ci == 0)
    def _():
        st_scr[...] = jnp.zeros_like(st_scr)

    xc = x_ref[...].reshape(M, D)
    hn = _rms(xc, gmix_ref[...])
    proj = _dot(hn.astype(BF16), win_ref[...])
    qz = proj[:, 0:D]
    fz = proj[:, D:2 * D]
    gz = proj[:, 3 * D:4 * D]
    lb = lb_ref[...]
    f = lb + (1.0 - lb) * _sigmoid(fz)
    q_scr[...] = qz * _sigmoid(qz)
    k_scr[...] = 1.0 - f
    v_scr[...] = proj[:, 2 * D:3 * D]
    lf = jnp.log(f)
    hi = lf.astype(BF16)
    rem = lf - hi.astype(F32)
    mid = rem.astype(BF16)
    lo = (rem - mid.astype(F32)).astype(BF16)
    tril = tril_ref[...]
    b_scr[...] = _dot(tril, hi) + _dot(tril, mid) + _dot(tril, lo)

    qe = (q_scr[...] * jnp.exp(b_scr[...])).astype(BF16)
    for r in range(R):
        for h in range(HEADS):
            L = slice(h * HEAD_DIM, (h + 1) * HEAD_DIM)
            o_scr[r * C:(r + 1) * C, L] = _dot_nt(qe[r * C:(r + 1) * C, L],
                                                  st_scr[r * HEADS + h].astype(BF16))

    t_iota = lax.broadcasted_iota(jnp.int32, (SUB, D), 0)
    for r in range(R):
        base = r * C
        for I in range(nsub):
            rows = slice(base + SUB * I, base + SUB * (I + 1))
            prev = slice(base, base + SUB * I)
            bI = b_scr[rows, :]
            qI = q_scr[rows, :]
            if I > 0:
                beta = b_scr[base + SUB * I - 1:base + SUB * I, :]
                qs = (qI * jnp.exp(bI - beta)).astype(BF16)
                ks = (k_scr[prev, :] * jnp.exp(beta - b_scr[prev, :])).astype(BF16)
                vs = v_scr[prev, :].astype(BF16)
                for h in range(HEADS):
                    L = slice(h * HEAD_DIM, (h + 1) * HEAD_DIM)
                    att = _dot_nt(qs[:, L], ks[:, L])
                    o_scr[rows, L] += _dot(att.astype(BF16), vs[:, L])
            for s in range(SUB):
                src = base + SUB * I + s
                keep = t_iota >= s
                dec = jnp.exp(jnp.where(keep, bI - b_scr[src:src + 1, :], 0.0))
                p = jnp.where(keep, qI * dec * k_scr[src:src + 1, :], 0.0)
                p_scr[src * SUB:(src + 1) * SUB, :] = p.astype(BF16)

    for h in range(HEADS):
        L = slice(h * HEAD_DIM, (h + 1) * HEAD_DIM)
        a_scr[:, L] = _dot(p_scr[:, L], ones_ref[...])
    for J in range(M // SUB):
        rows = slice(SUB * J, SUB * (J + 1))
        acc = o_scr[rows, :]
        for s in range(SUB):
            src = SUB * J + s
            acc = acc + a_scr[src * SUB:(src + 1) * SUB, :] * v_scr[src:src + 1, :]
        o_scr[rows, :] = acc

    for r in range(R):
        rows = slice(r * C, (r + 1) * C)
        b_last = b_scr[(r + 1) * C - 1:(r + 1) * C, :]
        kd = (k_scr[rows, :] * jnp.exp(b_last - b_scr[rows, :])).astype(BF16)
        eb_last = jnp.exp(b_last)
        for h in range(HEADS):
            L = slice(h * HEAD_DIM, (h + 1) * HEAD_DIM)
            vt = v_scr[rows, L].T.astype(BF16)
            st_scr[r * HEADS + h] = st_scr[r * HEADS + h] * eb_last[:, L] + _dot(vt, kd[:, L])

    gn = gn_ref[...]
    for h in range(HEADS):
        L = slice(h * HEAD_DIM, (h + 1) * HEAD_DIM)
        oh = o_scr[:, L]
        gzh = gz[:, L]
        o_scr[:, L] = _rms(oh, gn) * (gzh * _sigmoid(gzh))
    y_ref[...] = (xc + _dot(o_scr[...].astype(BF16), wout_ref[...])).reshape(R, C, D)

    @pl.when(ci == pl.num_programs(1) - 1)
    def _():
        for r in range(R):
            for h in range(HEADS):
                s_ref[r, h] = st_scr[r * HEADS + h].T


def _hgrn_prompt(x, gmix, w_in, w_out, gnorm, lb, chunk, seqs):
    B, T, D = x.shape
    C = min(chunk, T)
    R = min(seqs, B)
    assert T % C == 0 and C % SUB == 0 and B % R == 0
    M = R * C
    tril = jnp.kron(jnp.eye(R, dtype=F32), jnp.tril(jnp.ones((C, C), F32))).astype(BF16)
    ones = jnp.ones((HEAD_DIM, HEAD_DIM), BF16)
    const = lambda shape: pl.BlockSpec(shape, lambda b, c: (0,) * len(shape))
    return pl.pallas_call(
        _hgrn_prompt_kernel,
        grid=(B // R, T // C),
        in_specs=[
            pl.BlockSpec((R, C, D), lambda b, c: (b, c, 0)),
            const((1, D)), const((D, 4 * D)), const((D, D)), const((1, HEAD_DIM)), const((1, D)),
            const((M, M)), const((HEAD_DIM, HEAD_DIM)),
        ],
        out_specs=[
            pl.BlockSpec((R, C, D), lambda b, c: (b, c, 0)),
            pl.BlockSpec((R, HEADS, HEAD_DIM, HEAD_DIM), lambda b, c: (b, 0, 0, 0)),
        ],
        out_shape=[
            jax.ShapeDtypeStruct((B, T, D), F32),
            jax.ShapeDtypeStruct((B, HEADS, HEAD_DIM, HEAD_DIM), F32),
        ],
        scratch_shapes=[
            pltpu.VMEM((R * HEADS, HEAD_DIM, HEAD_DIM), F32),
            pltpu.VMEM((M, D), F32),
            pltpu.VMEM((M, D), F32),
            pltpu.VMEM((M, D), F32),
            pltpu.VMEM((M, D), F32),
            pltpu.VMEM((M * SUB, D), BF16),
            pltpu.VMEM((M * SUB, D), F32),
            pltpu.VMEM((M, D), F32),
        ],
        compiler_params=pltpu.CompilerParams(
            dimension_semantics=("arbitrary", "arbitrary"), vmem_limit_bytes=VMEM_LIMIT),
        name="hgrn_prompt",
    )(x, gmix, w_in, w_out, gnorm, lb, tril, ones)


def _hgrn_sample_kernel(x_ref, gmix_ref, win_ref, wout_ref, gn_ref, lb_ref, s_ref,
                        y_ref, snew_ref, proj_scr, o_scr):
    n = pl.program_id(0)
    D = x_ref.shape[1]

    @pl.when(n == 0)
    def _():
        hn = _rms(x_ref[...], gmix_ref[...])
        proj_scr[...] = _dot(hn.astype(BF16), win_ref[...])

    ts = s_ref.shape[0]
    tok0 = pl.multiple_of(n * ts, ts)
    rows = proj_scr[pl.ds(tok0, ts), :]
    qz = rows[:, 0:D]
    fz = rows[:, D:2 * D]
    v = rows[:, 2 * D:3 * D]
    lb = lb_ref[...]
    f = lb + (1.0 - lb) * _sigmoid(fz)
    q = qz * _sigmoid(qz)
    k = 1.0 - f

    def cols(a):
        return jnp.concatenate([a, jnp.zeros((HEAD_DIM - ts, HEAD_DIM), F32)], axis=0).T

    for h in range(HEADS):
        L = slice(h * HEAD_DIM, (h + 1) * HEAD_DIM)
        fc = cols(f[:, L])
        kc = cols(k[:, L])
        qc = cols(q[:, L])
        outs = []
        for j in range(ts):
            s_new = s_ref[j, h] * fc[:, j:j + 1] + kc[:, j:j + 1] * v[j:j + 1, L]
            snew_ref[j, h] = s_new
            outs.append(jnp.sum(qc[:, j:j + 1] * s_new, axis=0, keepdims=True))
        o_scr[pl.ds(tok0, ts), L] = jnp.concatenate(outs, axis=0)

    @pl.when(n == pl.num_programs(0) - 1)
    def _():
        gn = gn_ref[...]
        gz = proj_scr[:, 3 * D:4 * D]
        for h in range(HEADS):
            L = slice(h * HEAD_DIM, (h + 1) * HEAD_DIM)
            gzh = gz[:, L]
            o_scr[:, L] = _rms(o_scr[:, L], gn) * (gzh * _sigmoid(gzh))
        y_ref[...] = x_ref[...] + _dot(o_scr[...].astype(BF16), wout_ref[...])


def _hgrn_sample(x, gmix, w_in, w_out, gnorm, lb, state, block):
    N, D = x.shape
    ts = min(block, N)
    assert N % ts == 0 and ts % 8 == 0
    const = lambda shape: pl.BlockSpec(shape, lambda n: (0,) * len(shape))
    sblk = pl.BlockSpec((ts, HEADS, HEAD_DIM, HEAD_DIM), lambda n: (n, 0, 0, 0))
    return pl.pallas_call(
        _hgrn_sample_kernel,
        grid=(N // ts,),
        in_specs=[const((N, D)), const((1, D)), const((D, 4 * D)), const((D, D)),
                  const((1, HEAD_DIM)), const((1, D)), sblk],
        out_specs=[const((N, D)), sblk],
        out_shape=[jax.ShapeDtypeStruct((N, D), F32),
                   jax.ShapeDtypeStruct((N, HEADS, HEAD_DIM, HEAD_DIM), F32)],
        scratch_shapes=[pltpu.VMEM((N, 4 * D), F32), pltpu.VMEM((N, D), F32)],
        compiler_params=pltpu.CompilerParams(
            dimension_semantics=("arbitrary",), vmem_limit_bytes=VMEM_LIMIT),
        name="hgrn_sample",
    )(x, gmix, w_in, w_out, gnorm, lb, state)


CONV_PAD = 32


def _conv_tail(y, xc, bdw_ref, lng_ref, lnb_ref, pw2_ref, bpw2_ref):
    y = y + bdw_ref[...]
    mu = jnp.mean(y, axis=-1, keepdims=True)
    yc = y - mu
    var = jnp.mean(yc * yc, axis=-1, keepdims=True)
    yn = yc * lax.rsqrt(var + EPS) * lng_ref[...] + lnb_ref[...]
    z = yn * _sigmoid(yn)
    return xc + _dot(z.astype(BF16), pw2_ref[...]) + bpw2_ref[...]


def _glu(xc, gmix_ref, pw1_ref, bpw1_ref):
    D = xc.shape[1]
    hn = _rms(xc, gmix_ref[...])
    h2 = _dot(hn.astype(BF16), pw1_ref[...]) + bpw1_ref[...]
    return h2[:, 0:D] * _sigmoid(h2[:, D:2 * D])


def _conv_prompt_kernel(x_ref, gmix_ref, pw1_ref, bpw1_ref, wdw_ref, bdw_ref, lng_ref, lnb_ref,
                        pw2_ref, bpw2_ref, y_ref, buf_ref, full_scr, sh_scr, y_scr):
    C = x_ref.shape[1]
    D = x_ref.shape[2]
    ci = pl.program_id(1)
    hist = CONV_WIDTH - 1
    off = CONV_PAD - hist
    sh_rows = sh_scr.shape[1]

    @pl.when(ci == 0)
    def _():
        full_scr[0:CONV_PAD, :] = jnp.zeros((CONV_PAD, full_scr.shape[1]), F32)

    xc = x_ref[0]
    full_scr[CONV_PAD:CONV_PAD + C, :] = _glu(xc, gmix_ref, pw1_ref, bpw1_ref)

    def lane_chunk(c, carry):
        lanes = pl.ds(pl.multiple_of(c * LANES, LANES), LANES)
        for b in range(1, 8):
            sh_scr[b - 1, :, lanes] = full_scr[b:b + sh_rows, lanes]
        w = [jnp.broadcast_to(wdw_ref[j:j + 1, lanes], (8, LANES)) for j in range(CONV_WIDTH)]
        for r0 in range(0, C, 8):
            acc = None
            for j in range(CONV_WIDTH):
                a, b = divmod(off + j, 8)
                rows = slice(8 * a + r0, 8 * a + r0 + 8)
                src = full_scr[rows, lanes] if b == 0 else sh_scr[b - 1, rows, lanes]
                acc = src * w[j] if acc is None else acc + src * w[j]
            y_scr[r0:r0 + 8, lanes] = acc
        return carry

    lax.fori_loop(0, D // LANES, lane_chunk, 0)
    y_ref[0] = _conv_tail(y_scr[...], xc, bdw_ref, lng_ref, lnb_ref, pw2_ref, bpw2_ref)

    @pl.when(ci == pl.num_programs(1) - 1)
    def _():
        buf_ref[0] = full_scr[C + off:C + CONV_PAD, :]

    full_scr[0:CONV_PAD, :] = full_scr[C:C + CONV_PAD, :]


def _conv_prompt(x, gmix, pw1, bpw1, wdw, bdw, lng, lnb, pw2, bpw2, chunk):
    B, T, D = x.shape
    C = min(chunk, T)
    assert T % C == 0 and C >= CONV_PAD
    hist = CONV_WIDTH - 1
    const = lambda shape: pl.BlockSpec(shape, lambda b, c: (0,) * len(shape))
    return pl.pallas_call(
        _conv_prompt_kernel,
        grid=(B, T // C),
        in_specs=[pl.BlockSpec((1, C, D), lambda b, c: (b, c, 0)),
                  const((1, D)), const((D, 2 * D)), const((1, 2 * D)), const((CONV_WIDTH, D)),
                  const((1, D)), const((1, D)), const((1, D)), const((D, D)), const((1, D))],
        out_specs=[pl.BlockSpec((1, C, D), lambda b, c: (b, c, 0)),
                   pl.BlockSpec((1, hist, D), lambda b, c: (b, 0, 0))],
        out_shape=[jax.ShapeDtypeStruct((B, T, D), F32),
                   jax.ShapeDtypeStruct((B, hist, D), F32)],
        scratch_shapes=[pltpu.VMEM((CONV_PAD + C, D), F32),
                        pltpu.VMEM((7, C + CONV_PAD - 8, D), F32),
                        pltpu.VMEM((C, D), F32)],
        compiler_params=pltpu.CompilerParams(
            dimension_semantics=("arbitrary", "arbitrary"), vmem_limit_bytes=VMEM_LIMIT),
        name="conv_prompt",
    )(x, gmix, pw1, bpw1, wdw, bdw, lng, lnb, pw2, bpw2)


def _conv_sample_kernel(x_ref, gmix_ref, pw1_ref, bpw1_ref, wdw_ref, bdw_ref, lng_ref, lnb_ref,
                        pw2_ref, bpw2_ref, buf_ref, y_ref, nbuf_ref, u_scr, y_scr):
    nt = x_ref.shape[0]
    hist = CONV_WIDTH - 1
    xc = x_ref[...]
    u_scr[...] = _glu(xc, gmix_ref, pw1_ref, bpw1_ref)
    w_hist = wdw_ref[0:hist, :]
    w_last = wdw_ref[hist:hist + 1, :]

    def body(n, carry):
        buf = buf_ref[n]
        u = u_scr[pl.ds(n, 1), :]
        y_scr[pl.ds(n, 1), :] = jnp.sum(buf * w_hist, axis=0, keepdims=True) + u * w_last
        nbuf_ref[n, 0:hist - 1, :] = buf[1:hist, :]
        nbuf_ref[n, hist - 1:hist, :] = u
        return carry

    lax.fori_loop(0, nt, body, 0)
    y_ref[...] = _conv_tail(y_scr[...], xc, bdw_ref, lng_ref, lnb_ref, pw2_ref, bpw2_ref)


def _conv_sample(x, gmix, pw1, bpw1, wdw, bdw, lng, lnb, pw2, bpw2, buf, block):
    N, D = x.shape
    nt = min(block, N)
    assert N % nt == 0
    hist = CONV_WIDTH - 1
    const = lambda shape: pl.BlockSpec(shape, lambda i: (0,) * len(shape))
    return pl.pallas_call(
        _conv_sample_kernel,
        grid=(N // nt,),
        in_specs=[pl.BlockSpec((nt, D), lambda i: (i, 0)),
                  const((1, D)), const((D, 2 * D)), const((1, 2 * D)), const((CONV_WIDTH, D)),
                  const((1, D)), const((1, D)), const((1, D)), const((D, D)), const((1, D)),
                  pl.BlockSpec((nt, hist, D), lambda i: (i, 0, 0))],
        out_specs=[pl.BlockSpec((nt, D), lambda i: (i, 0)),
                   pl.BlockSpec((nt, hist, D), lambda i: (i, 0, 0))],
        out_shape=[jax.ShapeDtypeStruct((N, D), F32),
                   jax.ShapeDtypeStruct((N, hist, D), F32)],
        scratch_shapes=[pltpu.VMEM((nt, D), F32), pltpu.VMEM((nt, D), F32)],
        compiler_params=pltpu.CompilerParams(
            dimension_semantics=("arbitrary",), vmem_limit_bytes=VMEM_LIMIT),
        name="conv_sample",
    )(x, gmix, pw1, bpw1, wdw, bdw, lng, lnb, pw2, bpw2, buf)


def _topk_ranks(vals, ids, k):
    rank = jnp.full(vals.shape, float(k), F32)
    tops = []
    for r in range(k):
        m = jnp.max(vals, axis=0, keepdims=True)
        first = jnp.min(jnp.where(vals == m, ids, 1e9), axis=0, keepdims=True)
        sel = ids == first
        vals = jnp.where(sel, NEG_INF, vals)
        rank = jnp.where(sel, float(r), rank)
        tops.append(m)
    return rank, tops


INT_MIN = -2 ** 31


def _sort_key(s):
    b = lax.bitcast_convert_type(jnp.where(s == 0.0, 0.0, s), jnp.int32)
    return b ^ (lax.shift_right_arithmetic(b, 31) & jnp.int32(0x7FFFFFFF))


def _key_value(k):
    b = k ^ (lax.shift_right_arithmetic(k, 31) & jnp.int32(0x7FFFFFFF))
    return lax.bitcast_convert_type(b, F32)


def _topk_keys(keys, k):
    tops = []
    for r in range(k):
        m = jnp.max(keys, axis=0, keepdims=True)
        keys = jnp.where(keys == m, jnp.int32(INT_MIN + r), keys)
        tops.append(m)
    return keys, tops


def _selected(keys, k):
    sel = keys < jnp.int32(INT_MIN + k)
    n = jnp.sum(jnp.where(sel, 1.0, 0.0), axis=0, keepdims=True)
    return sel, jnp.where(n != float(k), 1.0, 0.0)


def _dup_bf16(x):
    u = lax.bitcast_convert_type(x.astype(BF16).astype(F32), jnp.int32)
    return u | lax.shift_right_logical(u, 16)


def _router_kernel(x_ref, g_ref, wq_ref, keys_ref,
                   xnt_ref, rank2_ref, e2_ref, c1_ref, e1_ref,
                   s_scr, rank_scr, top_scr, miss_scr):
    tn = x_ref.shape[0]
    xn = _rms(x_ref[...], g_ref[...])
    xnt = xn.T.astype(BF16)
    xnt_ref[...] = xnt
    qt = _dot(wq_ref[...], xnt).astype(BF16)
    for hp in range(2 * HEADS):
        s_scr[hp] = _dot(keys_ref[hp], qt[hp * HEAD_DIM:(hp + 1) * HEAD_DIM, :])

    outs = (rank2_ref, e2_ref, c1_ref, e1_ref)
    scr = (s_scr, rank_scr, top_scr)
    _select_experts(tn, scr, outs, miss_scr)


def _run_with_tie_fallback(n, per_group, problem, miss_scr):
    def fast(g, carry):
        miss = [problem(g * per_group + c, False) for c in range(per_group)]
        miss_scr[pl.ds(g, 1), :] = functools.reduce(jnp.maximum, miss)
        return carry

    lax.fori_loop(0, n // per_group, fast, 0, unroll=2 if per_group == 1 else 1)

    def fix(g, carry):
        @pl.when(jnp.max(miss_scr[pl.ds(g, 1), :]) > 0.0)
        def _():
            for c in range(per_group):
                problem(g * per_group + c, True)
        return carry

    lax.fori_loop(0, n // per_group, fix, 0)


def _select_experts(tn, scr, outs, miss_scr):
    s_scr, rank_scr, top_scr = scr
    rank2_ref, e2_ref, c1_ref, e1_ref = outs
    nch = tn // LANES
    key_ids = lax.broadcasted_iota(jnp.int32, (HEAD_DIM, LANES), 0).astype(F32)

    def half_problem(it, exact):
        hp = it // nch
        lanes = pl.ds(pl.multiple_of((it % nch) * LANES, LANES), LANES)
        s = s_scr[hp, :, lanes]
        miss = None
        if exact:
            rank, tops = _topk_ranks(s, key_ids, TOPK)
            top = jnp.concatenate(tops, axis=0)
        else:
            keys, tops = _topk_keys(_sort_key(s), TOPK)
            sel, miss = _selected(keys, TOPK)
            rank = jnp.where(sel, (keys ^ jnp.int32(INT_MIN)).astype(F32), float(TOPK))
            top = _key_value(jnp.concatenate(tops, axis=0))
        rank_scr[hp, :, lanes] = rank
        top_scr[hp, :, lanes] = top
        return miss

    _run_with_tie_fallback(2 * HEADS * nch, nch, half_problem, miss_scr)

    row16 = lax.broadcasted_iota(jnp.int32, (16, LANES), 0)
    row8 = lax.broadcasted_iota(jnp.int32, (8, LANES), 0)
    r16 = row16.astype(F32)
    r8 = row8.astype(F32)
    groups_valid = [row16 >= 0, row8 >= 0, row16 >= 2, row8 >= 2,
                    (row8 >= 2) & (row8 <= 4), (row8 >= 2) & (row8 <= 3), row8 == 2]
    groups_ids = [r16, 16.0 + r8, r16 * 16.0, r8 * 16.0 + 1.0, 32.0 + r8, 48.0 + r8, 64.0 + r8]
    cand_ids = jnp.concatenate(
        [jnp.where(v, i, 1e9) for v, i in zip(groups_valid, groups_ids)], axis=0)

    def head_problem(it, exact):
        h = it // nch
        lanes = pl.ds(pl.multiple_of((it % nch) * LANES, LANES), LANES)
        v1 = top_scr[2 * h, :, lanes]
        v2 = top_scr[2 * h + 1, :, lanes]
        groups = [v1[0:1] + v2, v1[1:2] + v2[0:8], v1 + v2[0:1], v1[0:8] + v2[1:2],
                  v1[2:3] + v2[0:8], v1[3:4] + v2[0:8], v1[4:5] + v2[0:8]]
        cand = jnp.concatenate(
            [jnp.where(v, g, NEG_INF) for v, g in zip(groups_valid, groups)], axis=0)
        miss = None
        if exact:
            crank, _ = _topk_ranks(cand, cand_ids, TOPK)
            sel = crank < float(TOPK)
        else:
            ckeys, _ = _topk_keys(_sort_key(cand), TOPK)
            sel, miss = _selected(ckeys, TOPK)
        z = jnp.sum(jnp.where(sel, jnp.exp(jnp.where(sel, cand - cand[0:1], 0.0)), 0.0),
                    axis=0, keepdims=True)
        self32 = jnp.where(sel, 1.0, 0.0)
        cnt = lambda a, b: jnp.sum(self32[a:b], axis=0, keepdims=True)
        per_r1 = (self32[24:40]
                  + jnp.concatenate([self32[40:48], jnp.zeros((8, LANES), F32)], axis=0)
                  + jnp.where(row16 == 0, cnt(0, 16), 0.0)
                  + jnp.where(row16 == 1, cnt(16, 24), 0.0)
                  + jnp.where(row16 == 2, cnt(48, 56), 0.0)
                  + jnp.where(row16 == 3, cnt(56, 64), 0.0)
                  + jnp.where(row16 == 4, cnt(64, 72), 0.0))
        rank1 = rank_scr[2 * h, :, lanes]
        rank2 = rank_scr[2 * h + 1, :, lanes]
        c1 = jnp.zeros((HEAD_DIM, LANES), F32)
        for j in range(1, TOPK // 3 + 1):
            reach = jnp.sum(jnp.where(per_r1 >= float(j), 1.0, 0.0), axis=0, keepdims=True)
            c1 = c1 + jnp.where(rank1 < reach, 1.0, 0.0)
        c1 = jnp.where(rank1 == 0.0, per_r1[0:1], jnp.where(rank1 == 1.0, per_r1[1:2], c1))
        s1 = s_scr[2 * h, :, lanes]
        s2 = s_scr[2 * h + 1, :, lanes]
        in1 = rank1 < float(TOPK)
        in2 = rank2 < float(TOPK)
        e1 = jnp.where(in1, jnp.exp(jnp.where(in1, s1 - v1[0:1], 0.0)), 0.0)
        e2 = jnp.where(in2, jnp.exp(jnp.where(in2, s2 - v2[0:1], 0.0)), 0.0) / z
        rank2_ref[h, :, lanes] = rank2.astype(BF16)
        e2_ref[h, :, lanes] = (0.5 * e2).astype(BF16)
        c1_ref[h, :, lanes] = _dup_bf16(c1)
        e1_ref[h, :, lanes] = _dup_bf16(e1)
        return miss

    _run_with_tie_fallback(HEADS * nch, nch, head_problem, miss_scr)


def _router(x, g, wq_t, keys, tn):
    N, D = x.shape
    assert N % tn == 0 and tn % LANES == 0
    const = lambda shape: pl.BlockSpec(shape, lambda i: (0,) * len(shape))
    gate_blk = pl.BlockSpec((HEADS, HEAD_DIM, tn), lambda i: (0, 0, i))
    gate_shape = lambda dt: jax.ShapeDtypeStruct((HEADS, HEAD_DIM, N), dt)
    return pl.pallas_call(
        _router_kernel,
        grid=(N // tn,),
        in_specs=[pl.BlockSpec((tn, D), lambda i: (i, 0)), const((1, D)),
                  const(wq_t.shape), const(keys.shape)],
        out_specs=[pl.BlockSpec((D, tn), lambda i: (0, i)),
                   gate_blk, gate_blk, gate_blk, gate_blk],
        out_shape=[jax.ShapeDtypeStruct((D, N), BF16),
                   gate_shape(BF16), gate_shape(BF16), gate_shape(jnp.int32),
                   gate_shape(jnp.int32)],
        scratch_shapes=[pltpu.VMEM((2 * HEADS, HEAD_DIM, tn), F32),
                        pltpu.VMEM((2 * HEADS, HEAD_DIM, tn), F32),
                        pltpu.VMEM((2 * HEADS, TOPK, tn), F32),
                        pltpu.VMEM((2 * HEADS * (tn // LANES), LANES), F32)],
        compiler_params=pltpu.CompilerParams(
            dimension_semantics=("arbitrary",), vmem_limit_bytes=VMEM_LIMIT),
        name="peer_router",
    )(x, g, wq_t, keys)


ROW_TILE = 16
MXU_TILE = 256


def _peer_dense_kernel(xnt_ref, u_ref, vt_ref, rank2_ref, e2_ref, c1_ref, e1_ref, x_ref, gfin_ref,
                       y_ref, hid0, hid1, g_scr, acc_scr, key2_scr, *, final_norm):
    t = pl.program_id(1)
    te, tn = hid0.shape
    nb = te // HEAD_DIM

    def key2_rows(ch, rt, slot):
        tile = (ch * (HEAD_DIM // ROW_TILE) + rt) * 2 * HEADS + slot
        return slice(tile * ROW_TILE, (tile + 1) * ROW_TILE)

    @pl.when(t == 0)
    def _():
        for ref in (hid0, hid1, acc_scr):
            ref[...] = jnp.zeros_like(ref)
        for ch in range(tn // LANES):
            lanes = slice(ch * LANES, (ch + 1) * LANES)
            for rt in range(HEAD_DIM // ROW_TILE):
                keys2 = slice(rt * ROW_TILE, (rt + 1) * ROW_TILE)
                for h in range(HEADS):
                    key2_scr[key2_rows(ch, rt, h), :] = rank2_ref[h, keys2, lanes]
                    key2_scr[key2_rows(ch, rt, HEADS + h), :] = e2_ref[h, keys2, lanes]

    def gate_group(hid_cur, i1, il, ch):
        lanes = slice(ch * LANES, (ch + 1) * LANES)
        bcast = lambda ref, h: pltpu.bitcast(
            jnp.broadcast_to(ref[h, i1:i1 + 1, lanes], (8, LANES)), BF16)
        c1 = [bcast(c1_ref, h) for h in range(HEADS)]
        e1 = [bcast(e1_ref, h) for h in range(HEADS)]
        for rt in range(HEAD_DIM // ROW_TILE):
            r0 = rt * ROW_TILE
            rows = slice(il * HEAD_DIM + r0, il * HEAD_DIM + r0 + ROW_TILE)
            w = jnp.zeros((ROW_TILE, LANES), BF16)
            for h in range(HEADS):
                w = w + jnp.where(key2_scr[key2_rows(ch, rt, h), :] < c1[h],
                                  key2_scr[key2_rows(ch, rt, HEADS + h), :] * e1[h],
                                  jnp.zeros_like(w))
            hid = hid_cur[rows, lanes]
            gelu2 = hid * (1.0 + lax.erf(hid * 0.7071067811865476))
            g_scr[rows, lanes] = w * gelu2.astype(BF16)

    def stage(hid_new, hid_cur, i1_base):
        width = min(MXU_TILE, tn)
        depth = min(MXU_TILE, te)
        for k0 in range(0, te, depth):
            for n0 in range(0, tn, width):
                hid_new[k0:k0 + depth, n0:n0 + width] = _dot(u_ref[0, k0:k0 + depth, :],
                                                             xnt_ref[:, n0:n0 + width])
                for il in range(k0 // HEAD_DIM, (k0 + depth) // HEAD_DIM):
                    for ch in range(n0 // LANES, (n0 + width) // LANES):
                        gate_group(hid_cur, i1_base + il, il, ch)
                acc_scr[:, n0:n0 + width] += _dot(vt_ref[0, 0, :, k0:k0 + depth],
                                                  g_scr[k0:k0 + depth, n0:n0 + width])

    group = c1_ref.shape[1]

    @pl.when(t % 2 == 0)
    def _():
        stage(hid0, hid1, nb % group)

    @pl.when(t % 2 == 1)
    def _():
        stage(hid1, hid0, 0)

    @pl.when(t == pl.num_programs(1) - 1)
    def _():
        y = x_ref[...] + acc_scr[...].T
        if final_norm:
            y = _rms(y, gfin_ref[...])
        y_ref[...] = y


def _peer_dense(xnt, u, vt, layer, rank2, e2, c1, e1, x, gfin, tn, final_norm):
    N, D = x.shape
    _, nblk, _, te = vt.shape
    assert N % tn == 0 and u.shape[1] == nblk * te and te % HEAD_DIM == 0
    nb = te // HEAD_DIM
    group = max(8, nb)
    assert group // nb <= 2 and HEAD_DIM % group == 0
    key2_blk = pl.BlockSpec((HEADS, HEAD_DIM, tn), lambda i, t: (0, 0, i))
    key1_blk = pl.BlockSpec(
        (HEADS, group, tn),
        lambda i, t: (0, jnp.clip(((t - 1) * nb) // group, 0, HEAD_DIM // group - 1), i))
    return pl.pallas_call(
        functools.partial(_peer_dense_kernel, final_norm=final_norm),
        grid=(N // tn, nblk + 1),
        in_specs=[pl.BlockSpec((D, tn), lambda i, t: (0, i)),
                  pl.BlockSpec((1, te, D), lambda i, t: (layer, jnp.minimum(t, nblk - 1), 0)),
                  pl.BlockSpec((1, 1, D, te),
                               lambda i, t: (layer, jnp.maximum(t - 1, 0), 0, 0)),
                  key2_blk, key2_blk, key1_blk, key1_blk,
                  pl.BlockSpec((tn, D), lambda i, t: (i, 0)),
                  pl.BlockSpec((1, D), lambda i, t: (0, 0))],
        out_specs=pl.BlockSpec((tn, D), lambda i, t: (i, 0)),
        out_shape=jax.ShapeDtypeStruct((N, D), F32),
        scratch_shapes=[pltpu.VMEM((te, tn), F32), pltpu.VMEM((te, tn), F32),
                        pltpu.VMEM((te, tn), BF16),
                        pltpu.VMEM((D, tn), F32),
                        pltpu.VMEM((2 * HEADS * HEAD_DIM * (tn // LANES), LANES), BF16)],
        compiler_params=pltpu.CompilerParams(
            dimension_semantics=("arbitrary", "arbitrary"), vmem_limit_bytes=VMEM_LIMIT),
        name="peer_dense",
    )(xnt, u, vt, rank2, e2, c1, e1, x, gfin)


def _peer(x, g, wq_t, keys, u, vt, layer, gfin, tn_router, tn_dense, final_norm):
    xnt, rank2, e2, c1, e1 = _router(x, g, wq_t, keys, tn_router)
    return _peer_dense(xnt, u, vt, layer, rank2, e2, c1, e1, x, gfin, tn_dense, final_norm)


HGRN_CHUNK = 64
HGRN_SEQS = 2
HGRN_SAMPLE_BLOCK = 8
CONV_CHUNK = 512
CONV_SAMPLE_BLOCK = 32
PEER_ROUTER_TOKENS = 512
PEER_DENSE_TOKENS = 512
PEER_EXPERTS = 512


def kernel(x_prompt, x_sample, state_hgrn, state_conv, norm_mix, norm_ffn, norm_final, hg_w_in, hg_w_out, hg_gnorm, hg_lb_logits, cv_w_pw1, cv_b_pw1, cv_w_dw, cv_b_dw, cv_ln_g, cv_ln_b, cv_w_pw2, cv_b_pw2, peer_w_q, peer_sub_keys, peer_u, peer_v):
    B, T, D = x_prompt.shape
    NS = x_sample.shape[0]
    depth = norm_mix.shape[0]
    row = lambda a: a.reshape(1, -1)

    lb_all = jnp.cumsum(jax.nn.softmax(hg_lb_logits.astype(F32), axis=0), axis=0)

    u_all = peer_u.astype(BF16)
    vt_all = peer_v.reshape(depth, -1, PEER_EXPERTS, D).transpose(0, 1, 3, 2).astype(BF16)

    xp = x_prompt
    xs = x_sample.reshape(NS, D)
    hg_p, hg_s, cv_p, cv_s = [], [], [], []
    for i in range(depth):
        j = i // 2
        gmix = row(norm_mix[i])
        if i % 2 == 0:
            w_in = hg_w_in[j].astype(BF16)
            w_out = hg_w_out[j].astype(BF16)
            gn = row(hg_gnorm[j])
            lb = row(lb_all[i])
            xp, sp = _hgrn_prompt(xp, gmix, w_in, w_out, gn, lb, HGRN_CHUNK, HGRN_SEQS)
            xs, ss = _hgrn_sample(xs, gmix, w_in, w_out, gn, lb, state_hgrn[j],
                                  HGRN_SAMPLE_BLOCK)
            hg_p.append(sp)
            hg_s.append(ss)
        else:
            cw = (cv_w_pw1[j].astype(BF16), row(cv_b_pw1[j]), cv_w_dw[j], row(cv_b_dw[j]),
                  row(cv_ln_g[j]), row(cv_ln_b[j]), cv_w_pw2[j].astype(BF16), row(cv_b_pw2[j]))
            xp, bp = _conv_prompt(xp, gmix, *cw, CONV_CHUNK)
            xs, bs = _conv_sample(xs, gmix, *cw, state_conv[j], CONV_SAMPLE_BLOCK)
            cv_p.append(bp)
            cv_s.append(bs)
        wq_t = peer_w_q[i].T.astype(BF16)
        keys = peer_sub_keys[i].reshape(2 * HEADS, HEAD_DIM, HEAD_DIM).astype(BF16)
        gffn = row(norm_ffn[i])
        gfin = row(norm_final)
        last = i == depth - 1
        xp = _peer(xp.reshape(B * T, D), gffn, wq_t, keys, u_all, vt_all, i, gfin,
                   min(PEER_ROUTER_TOKENS, B * T), min(PEER_DENSE_TOKENS, B * T),
                   last).reshape(B, T, D)
        xs = _peer(xs, gffn, wq_t, keys, u_all, vt_all, i, gfin, min(LANES, NS), min(LANES, NS),
                   last)
    return (xp, xs.reshape(NS, 1, D), jnp.stack(hg_p), jnp.stack(hg_s),
            jnp.stack(cv_p), jnp.stack(cv_s))
```

```python
import functools

import jax
import jax.numpy as jnp
from jax import lax
from jax.experimental import pallas as pl
from jax.experimental.pallas import tpu as pltpu

F32 = jnp.float32
BF16 = jnp.bfloat16
EPS = 1e-6

HEADS = 8
HEAD_DIM = 128
TOPK = 16
CONV_WIDTH = 31
LANES = 128
SUB = 16
VMEM_LIMIT = 56 * 1024 * 1024

NEG_INF = float("-inf")


def _rms(x, g):
    return x * lax.rsqrt(jnp.mean(x * x, axis=-1, keepdims=True) + EPS) * g


def _sigmoid(x):
    return 1.0 / (1.0 + jnp.exp(-x))


def _dot(a, b):
    return jnp.dot(a, b, preferred_element_type=F32)


def _dot_nt(a, b):
    return lax.dot_general(a, b, (((1,), (1,)), ((), ())), preferred_element_type=F32)


def _hgrn_prompt_kernel(x_ref, gmix_ref, win_ref, wout_ref, gn_ref, lb_ref, tril_ref, ones_ref,
                        y_ref, s_ref,
                        st_scr, q_scr, k_scr, v_scr, b_scr, p_scr, a_scr, o_scr):
    R, C, D = x_ref.shape
    M = R * C
    nsub = C // SUB
    ci = pl.program_id(1)

    @pl.when(ci == 0)
    def _():
        st_scr[...] = jnp.zeros_like(st_scr)

    xc = x_ref[...].reshape(M, D)
    hn = _rms(xc, gmix_ref[...])
    proj = _dot(hn.astype(BF16), win_ref[...])
    qz = proj[:, 0:D]
    fz = proj[:, D:2 * D]
    gz = proj[:, 3 * D:4 * D]
    lb = lb_ref[...]
    f = lb + (1.0 - lb) * _sigmoid(fz)
    q_scr[...] = qz * _sigmoid(qz)
    k_scr[...] = 1.0 - f
    v_scr[...] = proj[:, 2 * D:3 * D]
    lf = jnp.log(f)
    hi = lf.astype(BF16)
    rem = lf - hi.astype(F32)
    mid = rem.astype(BF16)
    lo = (rem - mid.astype(F32)).astype(BF16)
    tril = tril_ref[...]
    b_scr[...] = _dot(tril, hi) + _dot(tril, mid) + _dot(tril, lo)

    qe = (q_scr[...] * jnp.exp(b_scr[...])).astype(BF16)
    for r in range(R):
        for h in range(HEADS):
            L = slice(h * HEAD_DIM, (h + 1) * HEAD_DIM)
            o_scr[r * C:(r + 1) * C, L] = _dot_nt(qe[r * C:(r + 1) * C, L],
                                                  st_scr[r * HEADS + h].astype(BF16))

    t_iota = lax.broadcasted_iota(jnp.int32, (SUB, D), 0)
    for r in range(R):
        base = r * C
        for I in range(nsub):
            rows = slice(base + SUB * I, base + SUB * (I + 1))
            prev = slice(base, base + SUB * I)
            bI = b_scr[rows, :]
            qI = q_scr[rows, :]
            if I > 0:
                beta = b_scr[base + SUB * I - 1:base + SUB * I, :]
                qs = (qI * jnp.exp(bI - beta)).astype(BF16)
                ks = (k_scr[prev, :] * jnp.exp(beta - b_scr[prev, :])).astype(BF16)
                vs = v_scr[prev, :].astype(BF16)
                for h in range(HEADS):
                    L = slice(h * HEAD_DIM, (h + 1) * HEAD_DIM)
                    att = _dot_nt(qs[:, L], ks[:, L])
                    o_scr[rows, L] += _dot(att.astype(BF16), vs[:, L])
            for s in range(SUB):
                src = base + SUB * I + s
                keep = t_iota >= s
                dec = jnp.exp(jnp.where(keep, bI - b_scr[src:src + 1, :], 0.0))
                p = jnp.where(keep, qI * dec * k_scr[src:src + 1, :], 0.0)
                p_scr[src * SUB:(src + 1) * SUB, :] = p.astype(BF16)

    for h in range(HEADS):
        L = slice(h * HEAD_DIM, (h + 1) * HEAD_DIM)
        a_scr[:, L] = _dot(p_scr[:, L], ones_ref[...])
    for J in range(M // SUB):
        rows = slice(SUB * J, SUB * (J + 1))
        acc = o_scr[rows, :]
        for s in range(SUB):
            src = SUB * J + s
            acc = acc + a_scr[src * SUB:(src + 1) * SUB, :] * v_scr[src:src + 1, :]
        o_scr[rows, :] = acc

    for r in range(R):
        rows = slice(r * C, (r + 1) * C)
        b_last = b_scr[(r + 1) * C - 1:(r + 1) * C, :]
        kd = (k_scr[rows, :] * jnp.exp(b_last - b_scr[rows, :])).astype(BF16)
        eb_last = jnp.exp(b_last)
        for h in range(HEADS):
            L = slice(h * HEAD_DIM, (h + 1) * HEAD_DIM)
            vt = v_scr[rows, L].T.astype(BF16)
            st_scr[r * HEADS + h] = st_scr[r * HEADS + h] * eb_last[:, L] + _dot(vt, kd[:, L])

    gn = gn_ref[...]
    for h in range(HEADS):
        L = slice(h * HEAD_DIM, (h + 1) * HEAD_DIM)
        oh = o_scr[:, L]
        gzh = gz[:, L]
        o_scr[:, L] = _rms(oh, gn) * (gzh * _sigmoid(gzh))
    y_ref[...] = (xc + _dot(o_scr[...].astype(BF16), wout_ref[...])).reshape(R, C, D)

    @pl.when(ci == pl.num_programs(1) - 1)
    def _():
        for r in range(R):
            for h in range(HEADS):
                s_ref[r, h] = st_scr[r * HEADS + h].T


def _hgrn_prompt(x, gmix, w_in, w_out, gnorm, lb, chunk, seqs):
    B, T, D = x.shape
    C = min(chunk, T)
    R = min(seqs, B)
    assert T % C == 0 and C % SUB == 0 and B % R == 0
    M = R * C
    tril = jnp.kron(jnp.eye(R, dtype=F32), jnp.tril(jnp.ones((C, C), F32))).astype(BF16)
    ones = jnp.ones((HEAD_DIM, HEAD_DIM), BF16)
    const = lambda shape: pl.BlockSpec(shape, lambda b, c: (0,) * len(shape))
    return pl.pallas_call(
        _hgrn_prompt_kernel,
        grid=(B // R, T // C),
        in_specs=[
            pl.BlockSpec((R, C, D), lambda b, c: (b, c, 0)),
            const((1, D)), const((D, 4 * D)), const((D, D)), const((1, HEAD_DIM)), const((1, D)),
            const((M, M)), const((HEAD_DIM, HEAD_DIM)),
        ],
        out_specs=[
            pl.BlockSpec((R, C, D), lambda b, c: (b, c, 0)),
            pl.BlockSpec((R, HEADS, HEAD_DIM, HEAD_DIM), lambda b, c: (b, 0, 0, 0)),
        ],
        out_shape=[
            jax.ShapeDtypeStruct((B, T, D), F32),
            jax.ShapeDtypeStruct((B, HEADS, HEAD_DIM, HEAD_DIM), F32),
        ],
        scratch_shapes=[
            pltpu.VMEM((R * HEADS, HEAD_DIM, HEAD_DIM), F32),
            pltpu.VMEM((M, D), F32),
            pltpu.VMEM((M, D), F32),
            pltpu.VMEM((M, D), F32),
            pltpu.VMEM((M, D), F32),
            pltpu.VMEM((M * SUB, D), BF16),
            pltpu.VMEM((M * SUB, D), F32),
            pltpu.VMEM((M, D), F32),
        ],
        compiler_params=pltpu.CompilerParams(
            dimension_semantics=("arbitrary", "arbitrary"), vmem_limit_bytes=VMEM_LIMIT),
        name="hgrn_prompt",
    )(x, gmix, w_in, w_out, gnorm, lb, tril, ones)


def _hgrn_sample_kernel(x_ref, gmix_ref, win_ref, wout_ref, gn_ref, lb_ref, s_ref,
                        y_ref, snew_ref, proj_scr, o_scr):
    n = pl.program_id(0)
    D = x_ref.shape[1]

    @pl.when(n == 0)
    def _():
        hn = _rms(x_ref[...], gmix_ref[...])
        proj_scr[...] = _dot(hn.astype(BF16), win_ref[...])

    ts = s_ref.shape[0]
    tok0 = pl.multiple_of(n * ts, ts)
    rows = proj_scr[pl.ds(tok0, ts), :]
    qz = rows[:, 0:D]
    fz = rows[:, D:2 * D]
    v = rows[:, 2 * D:3 * D]
    lb = lb_ref[...]
    f = lb + (1.0 - lb) * _sigmoid(fz)
    q = qz * _sigmoid(qz)
    k = 1.0 - f

    def cols(a):
        return jnp.concatenate([a, jnp.zeros((HEAD_DIM - ts, HEAD_DIM), F32)], axis=0).T

    for h in range(HEADS):
        L = slice(h * HEAD_DIM, (h + 1) * HEAD_DIM)
        fc = cols(f[:, L])
        kc = cols(k[:, L])
        qc = cols(q[:, L])
        outs = []
        for j in range(ts):
            s_new = s_ref[j, h] * fc[:, j:j + 1] + kc[:, j:j + 1] * v[j:j + 1, L]
            snew_ref[j, h] = s_new
            outs.append(jnp.sum(qc[:, j:j + 1] * s_new, axis=0, keepdims=True))
        o_scr[pl.ds(tok0, ts), L] = jnp.concatenate(outs, axis=0)

    @pl.when(n == pl.num_programs(0) - 1)
    def _():
        gn = gn_ref[...]
        gz = proj_scr[:, 3 * D:4 * D]
        for h in range(HEADS):
            L = slice(h * HEAD_DIM, (h + 1) * HEAD_DIM)
            gzh = gz[:, L]
            o_scr[:, L] = _rms(o_scr[:, L], gn) * (gzh * _sigmoid(gzh))
        y_ref[...] = x_ref[...] + _dot(o_scr[...].astype(BF16), wout_ref[...])


def _hgrn_sample(x, gmix, w_in, w_out, gnorm, lb, state, block):
    N, D = x.shape
    ts = min(block, N)
    assert N % ts == 0 and ts % 8 == 0
    const = lambda shape: pl.BlockSpec(shape, lambda n: (0,) * len(shape))
    sblk = pl.BlockSpec((ts, HEADS, HEAD_DIM, HEAD_DIM), lambda n: (n, 0, 0, 0))
    return pl.pallas_call(
        _hgrn_sample_kernel,
        grid=(N // ts,),
        in_specs=[const((N, D)), const((1, D)), const((D, 4 * D)), const((D, D)),
                  const((1, HEAD_DIM)), const((1, D)), sblk],
        out_specs=[const((N, D)), sblk],
        out_shape=[jax.ShapeDtypeStruct((N, D), F32),
                   jax.ShapeDtypeStruct((N, HEADS, HEAD_DIM, HEAD_DIM), F32)],
        scratch_shapes=[pltpu.VMEM((N, 4 * D), F32), pltpu.VMEM((N, D), F32)],
        compiler_params=pltpu.CompilerParams(
            dimension_semantics=("arbitrary",), vmem_limit_bytes=VMEM_LIMIT),
        name="hgrn_sample",
    )(x, gmix, w_in, w_out, gnorm, lb, state)


CONV_PAD = 32


def _conv_tail(y, xc, bdw_ref, lng_ref, lnb_ref, pw2_ref, bpw2_ref):
    y = y + bdw_ref[...]
    mu = jnp.mean(y, axis=-1, keepdims=True)
    yc = y - mu
    var = jnp.mean(yc * yc, axis=-1, keepdims=True)
    yn = yc * lax.rsqrt(var + EPS) * lng_ref[...] + lnb_ref[...]
    z = yn * _sigmoid(yn)
    return xc + _dot(z.astype(BF16), pw2_ref[...]) + bpw2_ref[...]


def _glu(xc, gmix_ref, pw1_ref, bpw1_ref):
    D = xc.shape[1]
    hn = _rms(xc, gmix_ref[...])
    h2 = _dot(hn.astype(BF16), pw1_ref[...]) + bpw1_ref[...]
    return h2[:, 0:D] * _sigmoid(h2[:, D:2 * D])


def _conv_prompt_kernel(x_ref, gmix_ref, pw1_ref, bpw1_ref, wdw_ref, bdw_ref, lng_ref, lnb_ref,
                        pw2_ref, bpw2_ref, y_ref, buf_ref, full_scr, sh_scr, y_scr):
    C = x_ref.shape[1]
    D = x_ref.shape[2]
    ci = pl.program_id(1)
    hist = CONV_WIDTH - 1
    off = CONV_PAD - hist
    sh_rows = sh_scr.shape[1]

    @pl.when(ci == 0)
    def _():
        full_scr[0:CONV_PAD, :] = jnp.zeros((CONV_PAD, full_scr.shape[1]), F32)

    xc = x_ref[0]
    full_scr[CONV_PAD:CONV_PAD + C, :] = _glu(xc, gmix_ref, pw1_ref, bpw1_ref)

    def lane_chunk(c, carry):
        lanes = pl.ds(pl.multiple_of(c * LANES, LANES), LANES)
        for b in range(1, 8):
            sh_scr[b - 1, :, lanes] = full_scr[b:b + sh_rows, lanes]
        w = [jnp.broadcast_to(wdw_ref[j:j + 1, lanes], (8, LANES)) for j in range(CONV_WIDTH)]
        for r0 in range(0, C, 8):
            acc = None
            for j in range(CONV_WIDTH):
                a, b = divmod(off + j, 8)
                rows = slice(8 * a + r0, 8 * a + r0 + 8)
                src = full_scr[rows, lanes] if b == 0 else sh_scr[b - 1, rows, lanes]
                acc = src * w[j] if acc is None else acc + src * w[j]
            y_scr[r0:r0 + 8, lanes] = acc
        return carry

    lax.fori_loop(0, D // LANES, lane_chunk, 0)
    y_ref[0] = _conv_tail(y_scr[...], xc, bdw_ref, lng_ref, lnb_ref, pw2_ref, bpw2_ref)

    @pl.when(ci == pl.num_programs(1) - 1)
    def _():
        buf_ref[0] = full_scr[C + off:C + CONV_PAD, :]

    full_scr[0:CONV_PAD, :] = full_scr[C:C + CONV_PAD, :]


def _conv_prompt(x, gmix, pw1, bpw1, wdw, bdw, lng, lnb, pw2, bpw2, chunk):
    B, T, D = x.shape
    C = min(chunk, T)
    assert T % C == 0 and C >= CONV_PAD
    hist = CONV_WIDTH - 1
    const = lambda shape: pl.BlockSpec(shape, lambda b, c: (0,) * len(shape))
    return pl.pallas_call(
        _conv_prompt_kernel,
        grid=(B, T // C),
        in_specs=[pl.BlockSpec((1, C, D), lambda b, c: (b, c, 0)),
                  const((1, D)), const((D, 2 * D)), const((1, 2 * D)), const((CONV_WIDTH, D)),
                  const((1, D)), const((1, D)), const((1, D)), const((D, D)), const((1, D))],
        out_specs=[pl.BlockSpec((1, C, D), lambda b, c: (b, c, 0)),
                   pl.BlockSpec((1, hist, D), lambda b, c: (b, 0, 0))],
        out_shape=[jax.ShapeDtypeStruct((B, T, D), F32),
                   jax.ShapeDtypeStruct((B, hist, D), F32)],
        scratch_shapes=[pltpu.VMEM((CONV_PAD + C, D), F32),
                        pltpu.VMEM((7, C + CONV_PAD - 8, D), F32),
                        pltpu.VMEM((C, D), F32)],
        compiler_params=pltpu.CompilerParams(
            dimension_semantics=("arbitrary", "arbitrary"), vmem_limit_bytes=VMEM_LIMIT),
        name="conv_prompt",
    )(x, gmix, pw1, bpw1, wdw, bdw, lng, lnb, pw2, bpw2)


def _conv_sample_kernel(x_ref, gmix_ref, pw1_ref, bpw1_ref, wdw_ref, bdw_ref, lng_ref, lnb_ref,
                        pw2_ref, bpw2_ref, buf_ref, y_ref, nbuf_ref, u_scr, y_scr):
    nt = x_ref.shape[0]
    hist = CONV_WIDTH - 1
    xc = x_ref[...]
    u_scr[...] = _glu(xc, gmix_ref, pw1_ref, bpw1_ref)
    w_hist = wdw_ref[0:hist, :]
    w_last = wdw_ref[hist:hist + 1, :]

    def body(n, carry):
        buf = buf_ref[n]
        u = u_scr[pl.ds(n, 1), :]
        y_scr[pl.ds(n, 1), :] = jnp.sum(buf * w_hist, axis=0, keepdims=True) + u * w_last
        nbuf_ref[n, 0:hist - 1, :] = buf[1:hist, :]
        nbuf_ref[n, hist - 1:hist, :] = u
        return carry

    lax.fori_loop(0, nt, body, 0)
    y_ref[...] = _conv_tail(y_scr[...], xc, bdw_ref, lng_ref, lnb_ref, pw2_ref, bpw2_ref)


def _conv_sample(x, gmix, pw1, bpw1, wdw, bdw, lng, lnb, pw2, bpw2, buf, block):
    N, D = x.shape
    nt = min(block, N)
    assert N % nt == 0
    hist = CONV_WIDTH - 1
    const = lambda shape: pl.BlockSpec(shape, lambda i: (0,) * len(shape))
    return pl.pallas_call(
        _conv_sample_kernel,
        grid=(N // nt,),
        in_specs=[pl.BlockSpec((nt, D), lambda i: (i, 0)),
                  const((1, D)), const((D, 2 * D)), const((1, 2 * D)), const((CONV_WIDTH, D)),
                  const((1, D)), const((1, D)), const((1, D)), const((D, D)), const((1, D)),
                  pl.BlockSpec((nt, hist, D), lambda i: (i, 0, 0))],
        out_specs=[pl.BlockSpec((nt, D), lambda i: (i, 0)),
                   pl.BlockSpec((nt, hist, D), lambda i: (i, 0, 0))],
        out_shape=[jax.ShapeDtypeStruct((N, D), F32),
                   jax.ShapeDtypeStruct((N, hist, D), F32)],
        scratch_shapes=[pltpu.VMEM((nt, D), F32), pltpu.VMEM((nt, D), F32)],
        compiler_params=pltpu.CompilerParams(
            dimension_semantics=("arbitrary",), vmem_limit_bytes=VMEM_LIMIT),
        name="conv_sample",
    )(x, gmix, pw1, bpw1, wdw, bdw, lng, lnb, pw2, bpw2, buf)


def _topk_ranks(vals, ids, k):
    rank = jnp.full(vals.shape, float(k), F32)
    tops = []
    for r in range(k):
        m = jnp.max(vals, axis=0, keepdims=True)
        first = jnp.min(jnp.where(vals == m, ids, 1e9), axis=0, keepdims=True)
        sel = ids == first
        vals = jnp.where(sel, NEG_INF, vals)
        rank = jnp.where(sel, float(r), rank)
        tops.append(m)
    return rank, tops


INT_MIN = -2 ** 31


def _sort_key(s):
    b = lax.bitcast_convert_type(jnp.where(s == 0.0, 0.0, s), jnp.int32)
    return b ^ (lax.shift_right_arithmetic(b, 31) & jnp.int32(0x7FFFFFFF))


def _key_value(k):
    b = k ^ (lax.shift_right_arithmetic(k, 31) & jnp.int32(0x7FFFFFFF))
    return lax.bitcast_convert_type(b, F32)


def _topk_keys(keys, k):
    tops = []
    for r in range(k):
        m = jnp.max(keys, axis=0, keepdims=True)
        keys = jnp.where(keys == m, jnp.int32(INT_MIN + r), keys)
        tops.append(m)
    return keys, tops


def _selected(keys, k):
    sel = keys < jnp.int32(INT_MIN + k)
    n = jnp.sum(jnp.where(sel, 1.0, 0.0), axis=0, keepdims=True)
    return sel, jnp.where(n != float(k), 1.0, 0.0)


def _dup_bf16(x):
    u = lax.bitcast_convert_type(x.astype(BF16).astype(F32), jnp.int32)
    return u | lax.shift_right_logical(u, 16)


def _router_kernel(x_ref, g_ref, wq_ref, keys_ref,
                   xnt_ref, rank2_ref, e2_ref, c1_ref, e1_ref,
                   s_scr, rank_scr, top_scr, miss_scr):
    tn = x_ref.shape[0]
    xn = _rms(x_ref[...], g_ref[...])
    xnt = xn.T.astype(BF16)
    xnt_ref[...] = xnt
    qt = _dot(wq_ref[...], xnt).astype(BF16)
    for hp in range(2 * HEADS):
        s_scr[hp] = _dot(keys_ref[hp], qt[hp * HEAD_DIM:(hp + 1) * HEAD_DIM, :])

    outs = (rank2_ref, e2_ref, c1_ref, e1_ref)
    scr = (s_scr, rank_scr, top_scr)
    _select_experts(tn, scr, outs, miss_scr)


def _run_with_tie_fallback(n, per_group, problem, miss_scr):
    def fast(g, carry):
        miss = [problem(g * per_group + c, False) for c in range(per_group)]
        miss_scr[pl.ds(g, 1), :] = functools.reduce(jnp.maximum, miss)
        return carry

    lax.fori_loop(0, n // per_group, fast, 0, unroll=2 if per_group == 1 else 1)

    def fix(g, carry):
        @pl.when(jnp.max(miss_scr[pl.ds(g, 1), :]) > 0.0)
        def _():
            for c in range(per_group):
                problem(g * per_group + c, True)
        return carry

    lax.fori_loop(0, n // per_group, fix, 0)


def _select_experts(tn, scr, outs, miss_scr):
    s_scr, rank_scr, top_scr = scr
    rank2_ref, e2_ref, c1_ref, e1_ref = outs
    nch = tn // LANES
    key_ids = lax.broadcasted_iota(jnp.int32, (HEAD_DIM, LANES), 0).astype(F32)

    def half_problem(it, exact):
        hp = it // nch
        lanes = pl.ds(pl.multiple_of((it % nch) * LANES, LANES), LANES)
        s = s_scr[hp, :, lanes]
        miss = None
        if exact:
            rank, tops = _topk_ranks(s, key_ids, TOPK)
            top = jnp.concatenate(tops, axis=0)
        else:
            keys, tops = _topk_keys(_sort_key(s), TOPK)
            sel, miss = _selected(keys, TOPK)
            rank = jnp.where(sel, (keys ^ jnp.int32(INT_MIN)).astype(F32), float(TOPK))
            top = _key_value(jnp.concatenate(tops, axis=0))
        rank_scr[hp, :, lanes] = rank
        top_scr[hp, :, lanes] = top
        return miss

    _run_with_tie_fallback(2 * HEADS * nch, nch, half_problem, miss_scr)

    row16 = lax.broadcasted_iota(jnp.int32, (16, LANES), 0)
    row8 = lax.broadcasted_iota(jnp.int32, (8, LANES), 0)
    r16 = row16.astype(F32)
    r8 = row8.astype(F32)
    groups_valid = [row16 >= 0, row8 >= 0, row16 >= 2, row8 >= 2,
                    (row8 >= 2) & (row8 <= 4), (row8 >= 2) & (row8 <= 3), row8 == 2]
    groups_ids = [r16, 16.0 + r8, r16 * 16.0, r8 * 16.0 + 1.0, 32.0 + r8, 48.0 + r8, 64.0 + r8]
    cand_ids = jnp.concatenate(
        [jnp.where(v, i, 1e9) for v, i in zip(groups_valid, groups_ids)], axis=0)

    def head_problem(it, exact):
        h = it // nch
        lanes = pl.ds(pl.multiple_of((it % nch) * LANES, LANES), LANES)
        v1 = top_scr[2 * h, :, lanes]
        v2 = top_scr[2 * h + 1, :, lanes]
        groups = [v1[0:1] + v2, v1[1:2] + v2[0:8], v1 + v2[0:1], v1[0:8] + v2[1:2],
                  v1[2:3] + v2[0:8], v1[3:4] + v2[0:8], v1[4:5] + v2[0:8]]
        cand = jnp.concatenate(
            [jnp.where(v, g, NEG_INF) for v, g in zip(groups_valid, groups)], axis=0)
        miss = None
        if exact:
            crank, _ = _topk_ranks(cand, cand_ids, TOPK)
            sel = crank < float(TOPK)
        else:
            ckeys, _ = _topk_keys(_sort_key(cand), TOPK)
            sel, miss = _selected(ckeys, TOPK)
        z = jnp.sum(jnp.where(sel, jnp.exp(jnp.where(sel, cand - cand[0:1], 0.0)), 0.0),
                    axis=0, keepdims=True)
        self32 = jnp.where(sel, 1.0, 0.0)
        cnt = lambda a, b: jnp.sum(self32[a:b], axis=0, keepdims=True)
        per_r1 = (self32[24:40]
                  + jnp.concatenate([self32[40:48], jnp.zeros((8, LANES), F32)], axis=0)
                  + jnp.where(row16 == 0, cnt(0, 16), 0.0)
                  + jnp.where(row16 == 1, cnt(16, 24), 0.0)
                  + jnp.where(row16 == 2, cnt(48, 56), 0.0)
                  + jnp.where(row16 == 3, cnt(56, 64), 0.0)
                  + jnp.where(row16 == 4, cnt(64, 72), 0.0))
        rank1 = rank_scr[2 * h, :, lanes]
        rank2 = rank_scr[2 * h + 1, :, lanes]
        c1 = jnp.zeros((HEAD_DIM, LANES), F32)
        for j in range(1, TOPK // 3 + 1):
            reach = jnp.sum(jnp.where(per_r1 >= float(j), 1.0, 0.0), axis=0, keepdims=True)
            c1 = c1 + jnp.where(rank1 < reach, 1.0, 0.0)
        c1 = jnp.where(rank1 == 0.0, per_r1[0:1], jnp.where(rank1 == 1.0, per_r1[1:2], c1))
        s1 = s_scr[2 * h, :, lanes]
        s2 = s_scr[2 * h + 1, :, lanes]
        in1 = rank1 < float(TOPK)
        in2 = rank2 < float(TOPK)
        e1 = jnp.where(in1, jnp.exp(jnp.where(in1, s1 - v1[0:1], 0.0)), 0.0)
        e2 = jnp.where(in2, jnp.exp(jnp.where(in2, s2 - v2[0:1], 0.0)), 0.0) / z
        rank2_ref[h, :, lanes] = rank2.astype(BF16)
        e2_ref[h, :, lanes] = (0.5 * e2).astype(BF16)
        c1_ref[h, :, lanes] = _dup_bf16(c1)
        e1_ref[h, :, lanes] = _dup_bf16(e1)
        return miss

    _run_with_tie_fallback(HEADS * nch, nch, head_problem, miss_scr)


def _router(x, g, wq_t, keys, tn):
    N, D = x.shape
    assert N % tn == 0 and tn % LANES == 0
    const = lambda shape: pl.BlockSpec(shape, lambda i: (0,) * len(shape))
    gate_blk = pl.BlockSpec((HEADS, HEAD_DIM, tn), lambda i: (0, 0, i))
    gate_shape = lambda dt: jax.ShapeDtypeStruct((HEADS, HEAD_DIM, N), dt)
    return pl.pallas_call(
        _router_kernel,
        grid=(N // tn,),
        in_specs=[pl.BlockSpec((tn, D), lambda i: (i, 0)), const((1, D)),
                  const(wq_t.shape), const(keys.shape)],
        out_specs=[pl.BlockSpec((D, tn), lambda i: (0, i)),
                   gate_blk, gate_blk, gate_blk, gate_blk],
        out_shape=[jax.ShapeDtypeStruct((D, N), BF16),
                   gate_shape(BF16), gate_shape(BF16), gate_shape(jnp.int32),
                   gate_shape(jnp.int32)],
        scratch_shapes=[pltpu.VMEM((2 * HEADS, HEAD_DIM, tn), F32),
                        pltpu.VMEM((2 * HEADS, HEAD_DIM, tn), F32),
                        pltpu.VMEM((2 * HEADS, TOPK, tn), F32),
                        pltpu.VMEM((2 * HEADS * (tn // LANES), LANES), F32)],
        compiler_params=pltpu.CompilerParams(
            dimension_semantics=("arbitrary",), vmem_limit_bytes=VMEM_LIMIT),
        name="peer_router",
    )(x, g, wq_t, keys)


ROW_TILE = 16
MXU_TILE = 256


def _peer_dense_kernel(xnt_ref, u_ref, vt_ref, rank2_ref, e2_ref, c1_ref, e1_ref, x_ref, gfin_ref,
                       y_ref, hid0, hid1, g_scr, acc_scr, key2_scr, *, final_norm):
    t = pl.program_id(1)
    te, tn = hid0.shape
    nb = te // HEAD_DIM

    def key2_rows(ch, rt, slot):
        tile = (ch * (HEAD_DIM // ROW_TILE) + rt) * 2 * HEADS + slot
        return slice(tile * ROW_TILE, (tile + 1) * ROW_TILE)

    @pl.when(t == 0)
    def _():
        for ref in (hid0, hid1, acc_scr):
            ref[...] = jnp.zeros_like(ref)
        for ch in range(tn // LANES):
            lanes = slice(ch * LANES, (ch + 1) * LANES)
            for rt in range(HEAD_DIM // ROW_TILE):
                keys2 = slice(rt * ROW_TILE, (rt + 1) * ROW_TILE)
                for h in range(HEADS):
                    key2_scr[key2_rows(ch, rt, h), :] = rank2_ref[h, keys2, lanes].astype(F32)
                    key2_scr[key2_rows(ch, rt, HEADS + h), :] = (
                        e2_ref[h, keys2, lanes].astype(F32))

    def gate_group(hid_cur, i1, il, ch):
        lanes = slice(ch * LANES, (ch + 1) * LANES)
        bcast = lambda ref, h: jnp.broadcast_to(
            lax.bitcast_convert_type(ref[h, i1:i1 + 1, lanes] & jnp.int32(-65536), F32),
            (ROW_TILE, LANES))
        c1 = [bcast(c1_ref, h) for h in range(HEADS)]
        e1 = [bcast(e1_ref, h) for h in range(HEADS)]
        for rt in range(HEAD_DIM // ROW_TILE):
            r0 = rt * ROW_TILE
            rows = slice(il * HEAD_DIM + r0, il * HEAD_DIM + r0 + ROW_TILE)
            w = jnp.zeros((ROW_TILE, LANES), F32)
            for h in range(HEADS):
                w = w + jnp.where(key2_scr[key2_rows(ch, rt, h), :] < c1[h],
                                  key2_scr[key2_rows(ch, rt, HEADS + h), :] * e1[h], 0.0)
            hid = hid_cur[rows, lanes]
            gelu2 = hid * (1.0 + lax.erf(hid * 0.7071067811865476))
            g_scr[rows, lanes] = (w * gelu2).astype(BF16)

    def stage(hid_new, hid_cur, i1_base):
        width = min(MXU_TILE, tn)
        depth = min(MXU_TILE, te)
        for k0 in range(0, te, depth):
            for n0 in range(0, tn, width):
                hid_new[k0:k0 + depth, n0:n0 + width] = _dot(u_ref[0, k0:k0 + depth, :],
                                                             xnt_ref[:, n0:n0 + width])
                for il in range(k0 // HEAD_DIM, (k0 + depth) // HEAD_DIM):
                    for ch in range(n0 // LANES, (n0 + width) // LANES):
                        gate_group(hid_cur, i1_base + il, il, ch)
                acc_scr[:, n0:n0 + width] += _dot(vt_ref[0, 0, :, k0:k0 + depth],
                                                  g_scr[k0:k0 + depth, n0:n0 + width])

    group = c1_ref.shape[1]

    @pl.when(t % 2 == 0)
    def _():
        stage(hid0, hid1, nb % group)

    @pl.when(t % 2 == 1)
    def _():
        stage(hid1, hid0, 0)

    @pl.when(t == pl.num_programs(1) - 1)
    def _():
        y = x_ref[...] + acc_scr[...].T
        if final_norm:
            y = _rms(y, gfin_ref[...])
        y_ref[...] = y


def _peer_dense(xnt, u, vt, layer, rank2, e2, c1, e1, x, gfin, tn, final_norm):
    N, D = x.shape
    _, nblk, _, te = vt.shape
    assert N % tn == 0 and u.shape[1] == nblk * te and te % HEAD_DIM == 0
    nb = te // HEAD_DIM
    group = max(8, nb)
    assert group // nb <= 2 and HEAD_DIM % group == 0
    key2_blk = pl.BlockSpec((HEADS, HEAD_DIM, tn), lambda i, t: (0, 0, i))
    key1_blk = pl.BlockSpec(
        (HEADS, group, tn),
        lambda i, t: (0, jnp.clip(((t - 1) * nb) // group, 0, HEAD_DIM // group - 1), i))
    return pl.pallas_call(
        functools.partial(_peer_dense_kernel, final_norm=final_norm),
        grid=(N // tn, nblk + 1),
        in_specs=[pl.BlockSpec((D, tn), lambda i, t: (0, i)),
                  pl.BlockSpec((1, te, D), lambda i, t: (layer, jnp.minimum(t, nblk - 1), 0)),
                  pl.BlockSpec((1, 1, D, te),
                               lambda i, t: (layer, jnp.maximum(t - 1, 0), 0, 0)),
                  key2_blk, key2_blk, key1_blk, key1_blk,
                  pl.BlockSpec((tn, D), lambda i, t: (i, 0)),
                  pl.BlockSpec((1, D), lambda i, t: (0, 0))],
        out_specs=pl.BlockSpec((tn, D), lambda i, t: (i, 0)),
        out_shape=jax.ShapeDtypeStruct((N, D), F32),
        scratch_shapes=[pltpu.VMEM((te, tn), F32), pltpu.VMEM((te, tn), F32),
                        pltpu.VMEM((te, tn), BF16),
                        pltpu.VMEM((D, tn), F32),
                        pltpu.VMEM((2 * HEADS * HEAD_DIM * (tn // LANES), LANES), F32)],
        compiler_params=pltpu.CompilerParams(
            dimension_semantics=("arbitrary", "arbitrary"), vmem_limit_bytes=VMEM_LIMIT),
        name="peer_dense",
    )(xnt, u, vt, rank2, e2, c1, e1, x, gfin)


def _peer(x, g, wq_t, keys, u, vt, layer, gfin, tn_router, tn_dense, final_norm):
    xnt, rank2, e2, c1, e1 = _router(x, g, wq_t, keys, tn_router)
    return _peer_dense(xnt, u, vt, layer, rank2, e2, c1, e1, x, gfin, tn_dense, final_norm)


HGRN_CHUNK = 64
HGRN_SEQS = 2
HGRN_SAMPLE_BLOCK = 8
CONV_CHUNK = 512
CONV_SAMPLE_BLOCK = 32
PEER_ROUTER_TOKENS = 512
PEER_DENSE_TOKENS = 512
PEER_EXPERTS = 512


def kernel(x_prompt, x_sample, state_hgrn, state_conv, norm_mix, norm_ffn, norm_final, hg_w_in, hg_w_out, hg_gnorm, hg_lb_logits, cv_w_pw1, cv_b_pw1, cv_w_dw, cv_b_dw, cv_ln_g, cv_ln_b, cv_w_pw2, cv_b_pw2, peer_w_q, peer_sub_keys, peer_u, peer_v):
    B, T, D = x_prompt.shape
    NS = x_sample.shape[0]
    depth = norm_mix.shape[0]
    row = lambda a: a.reshape(1, -1)

    lb_all = jnp.cumsum(jax.nn.softmax(hg_lb_logits.astype(F32), axis=0), axis=0)

    u_all = peer_u.astype(BF16)
    vt_all = peer_v.reshape(depth, -1, PEER_EXPERTS, D).transpose(0, 1, 3, 2).astype(BF16)

    xp = x_prompt
    xs = x_sample.reshape(NS, D)
    hg_p, hg_s, cv_p, cv_s = [], [], [], []
    for i in range(depth):
        j = i // 2
        gmix = row(norm_mix[i])
        if i % 2 == 0:
            w_in = hg_w_in[j].astype(BF16)
            w_out = hg_w_out[j].astype(BF16)
            gn = row(hg_gnorm[j])
            lb = row(lb_all[i])
            xp, sp = _hgrn_prompt(xp, gmix, w_in, w_out, gn, lb, HGRN_CHUNK, HGRN_SEQS)
            xs, ss = _hgrn_sample(xs, gmix, w_in, w_out, gn, lb, state_hgrn[j],
                                  HGRN_SAMPLE_BLOCK)
            hg_p.append(sp)
            hg_s.append(ss)
        else:
            cw = (cv_w_pw1[j].astype(BF16), row(cv_b_pw1[j]), cv_w_dw[j], row(cv_b_dw[j]),
                  row(cv_ln_g[j]), row(cv_ln_b[j]), cv_w_pw2[j].astype(BF16), row(cv_b_pw2[j]))
            xp, bp = _conv_prompt(xp, gmix, *cw, CONV_CHUNK)
            xs, bs = _conv_sample(xs, gmix, *cw, state_conv[j], CONV_SAMPLE_BLOCK)
            cv_p.append(bp)
            cv_s.append(bs)
        wq_t = peer_w_q[i].T.astype(BF16)
        keys = peer_sub_keys[i].reshape(2 * HEADS, HEAD_DIM, HEAD_DIM).astype(BF16)
        gffn = row(norm_ffn[i])
        gfin = row(norm_final)
        last = i == depth - 1
        xp = _peer(xp.reshape(B * T, D), gffn, wq_t, keys, u_all, vt_all, i, gfin,
                   min(PEER_ROUTER_TOKENS, B * T), min(PEER_DENSE_TOKENS, B * T),
                   last).reshape(B, T, D)
        xs = _peer(xs, gffn, wq_t, keys, u_all, vt_all, i, gfin, min(LANES, NS), min(LANES, NS),
                   last)
    return (xp, xs.reshape(NS, 1, D), jnp.stack(hg_p), jnp.stack(hg_s),
            jnp.stack(cv_p), jnp.stack(cv_s))
```

```python
import functools

import jax
import jax.numpy as jnp
from jax import lax
from jax.experimental import pallas as pl
from jax.experimental.pallas import tpu as pltpu

F32 = jnp.float32
BF16 = jnp.bfloat16
EPS = 1e-6

HEADS = 8
HEAD_DIM = 128
TOPK = 16
CONV_WIDTH = 31
LANES = 128
SUB = 16
VMEM_LIMIT = 56 * 1024 * 1024

NEG_INF = float("-inf")


def _rms(x, g):
    return x * lax.rsqrt(jnp.mean(x * x, axis=-1, keepdims=True) + EPS) * g


def _sigmoid(x):
    return 1.0 / (1.0 + jnp.exp(-x))


def _dot(a, b):
    return jnp.dot(a, b, preferred_element_type=F32)


def _dot_nt(a, b):
    return lax.dot_general(a, b, (((1,), (1,)), ((), ())), preferred_element_type=F32)


def _hgrn_prompt_kernel(x_ref, gmix_ref, win_ref, wout_ref, gn_ref, lb_ref, tril_ref, ones_ref,
                        y_ref, s_ref,
                        st_scr, q_scr, k_scr, v_scr, b_scr, p_scr, a_scr, o_scr):
    R, C, D = x_ref.shape
    M = R * C
    nsub = C // SUB
    ci = pl.program_id(1)

    @pl.when(ci == 0)
    def _():
        st_scr[...] = jnp.zeros_like(st_scr)

    xc = x_ref[...].reshape(M, D)
    hn = _rms(xc, gmix_ref[...])
    proj = _dot(hn.astype(BF16), win_ref[...])
    qz = proj[:, 0:D]
    fz = proj[:, D:2 * D]
    gz = proj[:, 3 * D:4 * D]
    lb = lb_ref[...]
    f = lb + (1.0 - lb) * _sigmoid(fz)
    q_scr[...] = qz * _sigmoid(qz)
    k_scr[...] = 1.0 - f
    v_scr[...] = proj[:, 2 * D:3 * D]
    lf = jnp.log(f)
    hi = lf.astype(BF16)
    rem = lf - hi.astype(F32)
    mid = rem.astype(BF16)
    lo = (rem - mid.astype(F32)).astype(BF16)
    tril = tril_ref[...]
    b_scr[...] = _dot(tril, hi) + _dot(tril, mid) + _dot(tril, lo)

    qe = (q_scr[...] * jnp.exp(b_scr[...])).astype(BF16)
    for r in range(R):
        for h in range(HEADS):
            L = slice(h * HEAD_DIM, (h + 1) * HEAD_DIM)
            o_scr[r * C:(r + 1) * C, L] = _dot_nt(qe[r * C:(r + 1) * C, L],
                                                  st_scr[r * HEADS + h].astype(BF16))

    t_iota = lax.broadcasted_iota(jnp.int32, (SUB, D), 0)
    for r in range(R):
        base = r * C
        for I in range(nsub):
            rows = slice(base + SUB * I, base + SUB * (I + 1))
            prev = slice(base, base + SUB * I)
            bI = b_scr[rows, :]
            qI = q_scr[rows, :]
            if I > 0:
                beta = b_scr[base + SUB * I - 1:base + SUB * I, :]
                qs = (qI * jnp.exp(bI - beta)).astype(BF16)
                ks = (k_scr[prev, :] * jnp.exp(beta - b_scr[prev, :])).astype(BF16)
                vs = v_scr[prev, :].astype(BF16)
                for h in range(HEADS):
                    L = slice(h * HEAD_DIM, (h + 1) * HEAD_DIM)
                    att = _dot_nt(qs[:, L], ks[:, L])
                    o_scr[rows, L] += _dot(att.astype(BF16), vs[:, L])
            for s in range(SUB):
                src = base + SUB * I + s
                keep = t_iota >= s
                dec = jnp.exp(jnp.where(keep, bI - b_scr[src:src + 1, :], 0.0))
                p = jnp.where(keep, qI * dec * k_scr[src:src + 1, :], 0.0)
                p_scr[src * SUB:(src + 1) * SUB, :] = p.astype(BF16)

    for h in range(HEADS):
        L = slice(h * HEAD_DIM, (h + 1) * HEAD_DIM)
        a_scr[:, L] = _dot(p_scr[:, L], ones_ref[...])
    for J in range(M // SUB):
        rows = slice(SUB * J, SUB * (J + 1))
        acc = o_scr[rows, :]
        for s in range(SUB):
            src = SUB * J + s
            acc = acc + a_scr[src * SUB:(src + 1) * SUB, :] * v_scr[src:src + 1, :]
        o_scr[rows, :] = acc

    for r in range(R):
        rows = slice(r * C, (r + 1) * C)
        b_last = b_scr[(r + 1) * C - 1:(r + 1) * C, :]
        kd = (k_scr[rows, :] * jnp.exp(b_last - b_scr[rows, :])).astype(BF16)
        eb_last = jnp.exp(b_last)
        for h in range(HEADS):
            L = slice(h * HEAD_DIM, (h + 1) * HEAD_DIM)
            vt = v_scr[rows, L].T.astype(BF16)
            st_scr[r * HEADS + h] = st_scr[r * HEADS + h] * eb_last[:, L] + _dot(vt, kd[:, L])

    gn = gn_ref[...]
    for h in range(HEADS):
        L = slice(h * HEAD_DIM, (h + 1) * HEAD_DIM)
        oh = o_scr[:, L]
        gzh = gz[:, L]
        o_scr[:, L] = _rms(oh, gn) * (gzh * _sigmoid(gzh))
    y_ref[...] = (xc + _dot(o_scr[...].astype(BF16), wout_ref[...])).reshape(R, C, D)

    @pl.when(ci == pl.num_programs(1) - 1)
    def _():
        for r in range(R):
            for h in range(HEADS):
                s_ref[r, h] = st_scr[r * HEADS + h].T


def _hgrn_prompt(x, gmix, w_in, w_out, gnorm, lb, chunk, seqs):
    B, T, D = x.shape
    C = min(chunk, T)
    R = min(seqs, B)
    assert T % C == 0 and C % SUB == 0 and B % R == 0
    M = R * C
    tril = jnp.kron(jnp.eye(R, dtype=F32), jnp.tril(jnp.ones((C, C), F32))).astype(BF16)
    ones = jnp.ones((HEAD_DIM, HEAD_DIM), BF16)
    const = lambda shape: pl.BlockSpec(shape, lambda b, c: (0,) * len(shape))
    return pl.pallas_call(
        _hgrn_prompt_kernel,
        grid=(B // R, T // C),
        in_specs=[
            pl.BlockSpec((R, C, D), lambda b, c: (b, c, 0)),
            const((1, D)), const((D, 4 * D)), const((D, D)), const((1, HEAD_DIM)), const((1, D)),
            const((M, M)), const((HEAD_DIM, HEAD_DIM)),
        ],
        out_specs=[
            pl.BlockSpec((R, C, D), lambda b, c: (b, c, 0)),
            pl.BlockSpec((R, HEADS, HEAD_DIM, HEAD_DIM), lambda b, c: (b, 0, 0, 0)),
        ],
        out_shape=[
            jax.ShapeDtypeStruct((B, T, D), F32),
            jax.ShapeDtypeStruct((B, HEADS, HEAD_DIM, HEAD_DIM), F32),
        ],
        scratch_shapes=[
            pltpu.VMEM((R * HEADS, HEAD_DIM, HEAD_DIM), F32),
            pltpu.VMEM((M, D), F32),
            pltpu.VMEM((M, D), F32),
            pltpu.VMEM((M, D), F32),
            pltpu.VMEM((M, D), F32),
            pltpu.VMEM((M * SUB, D), BF16),
            pltpu.VMEM((M * SUB, D), F32),
            pltpu.VMEM((M, D), F32),
        ],
        compiler_params=pltpu.CompilerParams(
            dimension_semantics=("arbitrary", "arbitrary"), vmem_limit_bytes=VMEM_LIMIT),
        name="hgrn_prompt",
    )(x, gmix, w_in, w_out, gnorm, lb, tril, ones)


def _hgrn_sample_kernel(x_ref, gmix_ref, win_ref, wout_ref, gn_ref, lb_ref, s_ref,
                        y_ref, snew_ref, proj_scr, o_scr):
    n = pl.program_id(0)
    D = x_ref.shape[1]

    @pl.when(n == 0)
    def _():
        hn = _rms(x_ref[...], gmix_ref[...])
        proj_scr[...] = _dot(hn.astype(BF16), win_ref[...])

    ts = s_ref.shape[0]
    tok0 = pl.multiple_of(n * ts, ts)
    rows = proj_scr[pl.ds(tok0, ts), :]
    qz = rows[:, 0:D]
    fz = rows[:, D:2 * D]
    v = rows[:, 2 * D:3 * D]
    lb = lb_ref[...]
    f = lb + (1.0 - lb) * _sigmoid(fz)
    q = qz * _sigmoid(qz)
    k = 1.0 - f

    def cols(a):
        return jnp.concatenate([a, jnp.zeros((HEAD_DIM - ts, HEAD_DIM), F32)], axis=0).T

    for h in range(HEADS):
        L = slice(h * HEAD_DIM, (h + 1) * HEAD_DIM)
        fc = cols(f[:, L])
        kc = cols(k[:, L])
        qc = cols(q[:, L])
        outs = []
        for j in range(ts):
            s_new = s_ref[j, h] * fc[:, j:j + 1] + kc[:, j:j + 1] * v[j:j + 1, L]
            snew_ref[j, h] = s_new
            outs.append(jnp.sum(qc[:, j:j + 1] * s_new, axis=0, keepdims=True))
        o_scr[pl.ds(tok0, ts), L] = jnp.concatenate(outs, axis=0)

    @pl.when(n == pl.num_programs(0) - 1)
    def _():
        gn = gn_ref[...]
        gz = proj_scr[:, 3 * D:4 * D]
        for h in range(HEADS):
            L = slice(h * HEAD_DIM, (h + 1) * HEAD_DIM)
            gzh = gz[:, L]
            o_scr[:, L] = _rms(o_scr[:, L], gn) * (gzh * _sigmoid(gzh))
        y_ref[...] = x_ref[...] + _dot(o_scr[...].astype(BF16), wout_ref[...])


def _hgrn_sample(x, gmix, w_in, w_out, gnorm, lb, state, block):
    N, D = x.shape
    ts = min(block, N)
    assert N % ts == 0 and ts % 8 == 0
    const = lambda shape: pl.BlockSpec(shape, lambda n: (0,) * len(shape))
    sblk = pl.BlockSpec((ts, HEADS, HEAD_DIM, HEAD_DIM), lambda n: (n, 0, 0, 0))
    return pl.pallas_call(
        _hgrn_sample_kernel,
        grid=(N // ts,),
        in_specs=[const((N, D)), const((1, D)), const((D, 4 * D)), const((D, D)),
                  const((1, HEAD_DIM)), const((1, D)), sblk],
        out_specs=[const((N, D)), sblk],
        out_shape=[jax.ShapeDtypeStruct((N, D), F32),
                   jax.ShapeDtypeStruct((N, HEADS, HEAD_DIM, HEAD_DIM), F32)],
        scratch_shapes=[pltpu.VMEM((N, 4 * D), F32), pltpu.VMEM((N, D), F32)],
        compiler_params=pltpu.CompilerParams(
            dimension_semantics=("arbitrary",), vmem_limit_bytes=VMEM_LIMIT),
        name="hgrn_sample",
    )(x, gmix, w_in, w_out, gnorm, lb, state)


CONV_PAD = 32


def _conv_tail(y, xc, bdw_ref, lng_ref, lnb_ref, pw2_ref, bpw2_ref):
    y = y + bdw_ref[...]
    mu = jnp.mean(y, axis=-1, keepdims=True)
    yc = y - mu
    var = jnp.mean(yc * yc, axis=-1, keepdims=True)
    yn = yc * lax.rsqrt(var + EPS) * lng_ref[...] + lnb_ref[...]
    z = yn * _sigmoid(yn)
    return xc + _dot(z.astype(BF16), pw2_ref[...]) + bpw2_ref[...]


def _glu(xc, gmix_ref, pw1_ref, bpw1_ref):
    D = xc.shape[1]
    hn = _rms(xc, gmix_ref[...])
    h2 = _dot(hn.astype(BF16), pw1_ref[...]) + bpw1_ref[...]
    return h2[:, 0:D] * _sigmoid(h2[:, D:2 * D])


def _conv_prompt_kernel(x_ref, gmix_ref, pw1_ref, bpw1_ref, wdw_ref, bdw_ref, lng_ref, lnb_ref,
                        pw2_ref, bpw2_ref, y_ref, buf_ref, full_scr, sh_scr, y_scr):
    C = x_ref.shape[1]
    D = x_ref.shape[2]
    ci = pl.program_id(1)
    hist = CONV_WIDTH - 1
    off = CONV_PAD - hist
    sh_rows = sh_scr.shape[1]

    @pl.when(ci == 0)
    def _():
        full_scr[0:CONV_PAD, :] = jnp.zeros((CONV_PAD, full_scr.shape[1]), F32)

    xc = x_ref[0]
    full_scr[CONV_PAD:CONV_PAD + C, :] = _glu(xc, gmix_ref, pw1_ref, bpw1_ref)

    def lane_chunk(c, carry):
        lanes = pl.ds(pl.multiple_of(c * LANES, LANES), LANES)
        for b in range(1, 8):
            sh_scr[b - 1, :, lanes] = full_scr[b:b + sh_rows, lanes]
        w = [jnp.broadcast_to(wdw_ref[j:j + 1, lanes], (8, LANES)) for j in range(CONV_WIDTH)]
        for r0 in range(0, C, 8):
            acc = None
            for j in range(CONV_WIDTH):
                a, b = divmod(off + j, 8)
                rows = slice(8 * a + r0, 8 * a + r0 + 8)
                src = full_scr[rows, lanes] if b == 0 else sh_scr[b - 1, rows, lanes]
                acc = src * w[j] if acc is None else acc + src * w[j]
            y_scr[r0:r0 + 8, lanes] = acc
        return carry

    lax.fori_loop(0, D // LANES, lane_chunk, 0)
    y_ref[0] = _conv_tail(y_scr[...], xc, bdw_ref, lng_ref, lnb_ref, pw2_ref, bpw2_ref)

    @pl.when(ci == pl.num_programs(1) - 1)
    def _():
        buf_ref[0] = full_scr[C + off:C + CONV_PAD, :]

    full_scr[0:CONV_PAD, :] = full_scr[C:C + CONV_PAD, :]


def _conv_prompt(x, gmix, pw1, bpw1, wdw, bdw, lng, lnb, pw2, bpw2, chunk):
    B, T, D = x.shape
    C = min(chunk, T)
    assert T % C == 0 and C >= CONV_PAD
    hist = CONV_WIDTH - 1
    const = lambda shape: pl.BlockSpec(shape, lambda b, c: (0,) * len(shape))
    return pl.pallas_call(
        _conv_prompt_kernel,
        grid=(B, T // C),
        in_specs=[pl.BlockSpec((1, C, D), lambda b, c: (b, c, 0)),
                  const((1, D)), const((D, 2 * D)), const((1, 2 * D)), const((CONV_WIDTH, D)),
                  const((1, D)), const((1, D)), const((1, D)), const((D, D)), const((1, D))],
        out_specs=[pl.BlockSpec((1, C, D), lambda b, c: (b, c, 0)),
                   pl.BlockSpec((1, hist, D), lambda b, c: (b, 0, 0))],
        out_shape=[jax.ShapeDtypeStruct((B, T, D), F32),
                   jax.ShapeDtypeStruct((B, hist, D), F32)],
        scratch_shapes=[pltpu.VMEM((CONV_PAD + C, D), F32),
                        pltpu.VMEM((7, C + CONV_PAD - 8, D), F32),
                        pltpu.VMEM((C, D), F32)],
        compiler_params=pltpu.CompilerParams(
            dimension_semantics=("arbitrary", "arbitrary"), vmem_limit_bytes=VMEM_LIMIT),
        name="conv_prompt",
    )(x, gmix, pw1, bpw1, wdw, bdw, lng, lnb, pw2, bpw2)


def _conv_sample_kernel(x_ref, gmix_ref, pw1_ref, bpw1_ref, wdw_ref, bdw_ref, lng_ref, lnb_ref,
                        pw2_ref, bpw2_ref, buf_ref, y_ref, nbuf_ref, u_scr, y_scr):
    nt = x_ref.shape[0]
    hist = CONV_WIDTH - 1
    xc = x_ref[...]
    u_scr[...] = _glu(xc, gmix_ref, pw1_ref, bpw1_ref)
    w_hist = wdw_ref[0:hist, :]
    w_last = wdw_ref[hist:hist + 1, :]

    def body(n, carry):
        buf = buf_ref[n]
        u = u_scr[pl.ds(n, 1), :]
        y_scr[pl.ds(n, 1), :] = jnp.sum(buf * w_hist, axis=0, keepdims=True) + u * w_last
        nbuf_ref[n, 0:hist - 1, :] = buf[1:hist, :]
        nbuf_ref[n, hist - 1:hist, :] = u
        return carry

    lax.fori_loop(0, nt, body, 0)
    y_ref[...] = _conv_tail(y_scr[...], xc, bdw_ref, lng_ref, lnb_ref, pw2_ref, bpw2_ref)


def _conv_sample(x, gmix, pw1, bpw1, wdw, bdw, lng, lnb, pw2, bpw2, buf, block):
    N, D = x.shape
    nt = min(block, N)
    assert N % nt == 0
    hist = CONV_WIDTH - 1
    const = lambda shape: pl.BlockSpec(shape, lambda i: (0,) * len(shape))
    return pl.pallas_call(
        _conv_sample_kernel,
        grid=(N // nt,),
        in_specs=[pl.BlockSpec((nt, D), lambda i: (i, 0)),
                  const((1, D)), const((D, 2 * D)), const((1, 2 * D)), const((CONV_WIDTH, D)),
                  const((1, D)), const((1, D)), const((1, D)), const((D, D)), const((1, D)),
                  pl.BlockSpec((nt, hist, D), lambda i: (i, 0, 0))],
        out_specs=[pl.BlockSpec((nt, D), lambda i: (i, 0)),
                   pl.BlockSpec((nt, hist, D), lambda i: (i, 0, 0))],
        out_shape=[jax.ShapeDtypeStruct((N, D), F32),
                   jax.ShapeDtypeStruct((N, hist, D), F32)],
        scratch_shapes=[pltpu.VMEM((nt, D), F32), pltpu.VMEM((nt, D), F32)],
        compiler_params=pltpu.CompilerParams(
            dimension_semantics=("arbitrary",), vmem_limit_bytes=VMEM_LIMIT),
        name="conv_sample",
    )(x, gmix, pw1, bpw1, wdw, bdw, lng, lnb, pw2, bpw2, buf)


def _topk_ranks(vals, ids, k):
    rank = jnp.full(vals.shape, float(k), F32)
    tops = []
    for r in range(k):
        m = jnp.max(vals, axis=0, keepdims=True)
        first = jnp.min(jnp.where(vals == m, ids, 1e9), axis=0, keepdims=True)
        sel = ids == first
        vals = jnp.where(sel, NEG_INF, vals)
        rank = jnp.where(sel, float(r), rank)
        tops.append(m)
    return rank, tops


INT_MIN = -2 ** 31


def _sort_key(s):
    b = lax.bitcast_convert_type(jnp.where(s == 0.0, 0.0, s), jnp.int32)
    return b ^ (lax.shift_right_arithmetic(b, 31) & jnp.int32(0x7FFFFFFF))


def _key_value(k):
    b = k ^ (lax.shift_right_arithmetic(k, 31) & jnp.int32(0x7FFFFFFF))
    return lax.bitcast_convert_type(b, F32)


def _topk_keys(keys, k):
    tops = []
    for r in range(k):
        m = jnp.max(keys, axis=0, keepdims=True)
        keys = jnp.where(keys == m, jnp.int32(INT_MIN + r), keys)
        tops.append(m)
    return keys, tops


def _selected(keys, k):
    sel = keys < jnp.int32(INT_MIN + k)
    n = jnp.sum(jnp.where(sel, 1.0, 0.0), axis=0, keepdims=True)
    return sel, jnp.where(n != float(k), 1.0, 0.0)


def _dup_bf16(x):
    u = lax.bitcast_convert_type(x.astype(BF16).astype(F32), jnp.int32)
    return u | lax.shift_right_logical(u, 16)


def _router_kernel(x_ref, g_ref, wq_ref, keys_ref,
                   xnt_ref, rank2_ref, e2_ref, c1_ref, e1_ref,
                   s_scr, rank_scr, top_scr, miss_scr):
    tn = x_ref.shape[0]
    xn = _rms(x_ref[...], g_ref[...])
    xnt = xn.T.astype(BF16)
    xnt_ref[...] = xnt
    qt = _dot(wq_ref[...], xnt).astype(BF16)
    for hp in range(2 * HEADS):
        s_scr[hp] = _dot(keys_ref[hp], qt[hp * HEAD_DIM:(hp + 1) * HEAD_DIM, :])

    outs = (rank2_ref, e2_ref, c1_ref, e1_ref)
    scr = (s_scr, rank_scr, top_scr)
    _select_experts(tn, scr, outs, miss_scr)


def _run_with_tie_fallback(n, per_group, problem, miss_scr):
    def fast(g, carry):
        miss = [problem(g * per_group + c, False) for c in range(per_group)]
        miss_scr[pl.ds(g, 1), :] = functools.reduce(jnp.maximum, miss)
        return carry

    lax.fori_loop(0, n // per_group, fast, 0, unroll=2 if per_group == 1 else 1)

    def fix(g, carry):
        @pl.when(jnp.max(miss_scr[pl.ds(g, 1), :]) > 0.0)
        def _():
            for c in range(per_group):
                problem(g * per_group + c, True)
        return carry

    lax.fori_loop(0, n // per_group, fix, 0)


def _select_experts(tn, scr, outs, miss_scr):
    s_scr, rank_scr, top_scr = scr
    rank2_ref, e2_ref, c1_ref, e1_ref = outs
    nch = tn // LANES
    key_ids = lax.broadcasted_iota(jnp.int32, (HEAD_DIM, LANES), 0).astype(F32)

    def half_problem(it, exact):
        hp = it // nch
        lanes = pl.ds(pl.multiple_of((it % nch) * LANES, LANES), LANES)
        s = s_scr[hp, :, lanes]
        miss = None
        if exact:
            rank, tops = _topk_ranks(s, key_ids, TOPK)
            top = jnp.concatenate(tops, axis=0)
        else:
            keys, tops = _topk_keys(_sort_key(s), TOPK)
            sel, miss = _selected(keys, TOPK)
            rank = jnp.where(sel, (keys ^ jnp.int32(INT_MIN)).astype(F32), float(TOPK))
            top = _key_value(jnp.concatenate(tops, axis=0))
        rank_scr[hp, :, lanes] = rank
        top_scr[hp, :, lanes] = top
        return miss

    _run_with_tie_fallback(2 * HEADS * nch, nch, half_problem, miss_scr)

    row16 = lax.broadcasted_iota(jnp.int32, (16, LANES), 0)
    row8 = lax.broadcasted_iota(jnp.int32, (8, LANES), 0)
    r16 = row16.astype(F32)
    r8 = row8.astype(F32)
    groups_valid = [row16 >= 0, row8 >= 0, row16 >= 2, row8 >= 2,
                    (row8 >= 2) & (row8 <= 4), (row8 >= 2) & (row8 <= 3), row8 == 2]
    groups_ids = [r16, 16.0 + r8, r16 * 16.0, r8 * 16.0 + 1.0, 32.0 + r8, 48.0 + r8, 64.0 + r8]
    cand_ids = jnp.concatenate(
        [jnp.where(v, i, 1e9) for v, i in zip(groups_valid, groups_ids)], axis=0)

    def head_problem(it, exact):
        h = it // nch
        lanes = pl.ds(pl.multiple_of((it % nch) * LANES, LANES), LANES)
        v1 = top_scr[2 * h, :, lanes]
        v2 = top_scr[2 * h + 1, :, lanes]
        groups = [v1[0:1] + v2, v1[1:2] + v2[0:8], v1 + v2[0:1], v1[0:8] + v2[1:2],
                  v1[2:3] + v2[0:8], v1[3:4] + v2[0:8], v1[4:5] + v2[0:8]]
        cand = jnp.concatenate(
            [jnp.where(v, g, NEG_INF) for v, g in zip(groups_valid, groups)], axis=0)
        miss = None
        if exact:
            crank, _ = _topk_ranks(cand, cand_ids, TOPK)
            sel = crank < float(TOPK)
        else:
            ckeys, _ = _topk_keys(_sort_key(cand), TOPK)
            sel, miss = _selected(ckeys, TOPK)
        z = jnp.sum(jnp.where(sel, jnp.exp(jnp.where(sel, cand - cand[0:1], 0.0)), 0.0),
                    axis=0, keepdims=True)
        self32 = jnp.where(sel, 1.0, 0.0)
        cnt = lambda a, b: jnp.sum(self32[a:b], axis=0, keepdims=True)
        per_r1 = (self32[24:40]
                  + jnp.concatenate([self32[40:48], jnp.zeros((8, LANES), F32)], axis=0)
                  + jnp.where(row16 == 0, cnt(0, 16), 0.0)
                  + jnp.where(row16 == 1, cnt(16, 24), 0.0)
                  + jnp.where(row16 == 2, cnt(48, 56), 0.0)
                  + jnp.where(row16 == 3, cnt(56, 64), 0.0)
                  + jnp.where(row16 == 4, cnt(64, 72), 0.0))
        rank1 = rank_scr[2 * h, :, lanes]
        rank2 = rank_scr[2 * h + 1, :, lanes]
        c1 = jnp.zeros((HEAD_DIM, LANES), F32)
        for j in range(1, TOPK // 3 + 1):
            reach = jnp.sum(jnp.where(per_r1 >= float(j), 1.0, 0.0), axis=0, keepdims=True)
            c1 = c1 + jnp.where(rank1 < reach, 1.0, 0.0)
        c1 = jnp.where(rank1 == 0.0, per_r1[0:1], jnp.where(rank1 == 1.0, per_r1[1:2], c1))
        s1 = s_scr[2 * h, :, lanes]
        s2 = s_scr[2 * h + 1, :, lanes]
        in1 = rank1 < float(TOPK)
        in2 = rank2 < float(TOPK)
        e1 = jnp.where(in1, jnp.exp(jnp.where(in1, s1 - v1[0:1], 0.0)), 0.0)
        e2 = jnp.where(in2, jnp.exp(jnp.where(in2, s2 - v2[0:1], 0.0)), 0.0) / z
        rank2_ref[h, :, lanes] = rank2.astype(BF16)
        e2_ref[h, :, lanes] = (0.5 * e2).astype(BF16)
        c1_ref[h, :, lanes] = _dup_bf16(c1)
        e1_ref[h, :, lanes] = _dup_bf16(e1)
        return miss

    _run_with_tie_fallback(HEADS * nch, nch, head_problem, miss_scr)


def _router(x, g, wq_t, keys, tn):
    N, D = x.shape
    assert N % tn == 0 and tn % LANES == 0
    const = lambda shape: pl.BlockSpec(shape, lambda i: (0,) * len(shape))
    gate_blk = pl.BlockSpec((HEADS, HEAD_DIM, tn), lambda i: (0, 0, i))
    gate_shape = lambda dt: jax.ShapeDtypeStruct((HEADS, HEAD_DIM, N), dt)
    return pl.pallas_call(
        _router_kernel,
        grid=(N // tn,),
        in_specs=[pl.BlockSpec((tn, D), lambda i: (i, 0)), const((1, D)),
                  const(wq_t.shape), const(keys.shape)],
        out_specs=[pl.BlockSpec((D, tn), lambda i: (0, i)),
                   gate_blk, gate_blk, gate_blk, gate_blk],
        out_shape=[jax.ShapeDtypeStruct((D, N), BF16),
                   gate_shape(BF16), gate_shape(BF16), gate_shape(jnp.int32),
                   gate_shape(jnp.int32)],
        scratch_shapes=[pltpu.VMEM((2 * HEADS, HEAD_DIM, tn), F32),
                        pltpu.VMEM((2 * HEADS, HEAD_DIM, tn), F32),
                        pltpu.VMEM((2 * HEADS, TOPK, tn), F32),
                        pltpu.VMEM((2 * HEADS * (tn // LANES), LANES), F32)],
        compiler_params=pltpu.CompilerParams(
            dimension_semantics=("arbitrary",), vmem_limit_bytes=VMEM_LIMIT),
        name="peer_router",
    )(x, g, wq_t, keys)


ROW_TILE = 16
MXU_TILE = 256


def _peer_dense_kernel(xnt_ref, u_ref, vt_ref, rank2_ref, e2_ref, c1_ref, e1_ref, x_ref, gfin_ref,
                       y_ref, hid0, hid1, g_scr, acc_scr, key2_scr, *, final_norm):
    t = pl.program_id(1)
    te, tn = hid0.shape
    nb = te // HEAD_DIM

    def key2_rows(ch, rt, slot):
        tile = (ch * (HEAD_DIM // ROW_TILE) + rt) * 2 * HEADS + slot
        return slice(tile * ROW_TILE, (tile + 1) * ROW_TILE)

    @pl.when(t == 0)
    def _():
        for ref in (hid0, hid1, acc_scr):
            ref[...] = jnp.zeros_like(ref)
        for ch in range(tn // LANES):
            lanes = slice(ch * LANES, (ch + 1) * LANES)
            for rt in range(HEAD_DIM // ROW_TILE):
                keys2 = slice(rt * ROW_TILE, (rt + 1) * ROW_TILE)
                for h in range(HEADS):
                    key2_scr[key2_rows(ch, rt, h), :] = rank2_ref[h, keys2, lanes].astype(F32)
                    key2_scr[key2_rows(ch, rt, HEADS + h), :] = (
                        e2_ref[h, keys2, lanes].astype(F32))

    def gate_group(hid_cur, i1, il, ch):
        lanes = slice(ch * LANES, (ch + 1) * LANES)
        bcast = lambda ref, h: jnp.broadcast_to(
            lax.bitcast_convert_type(ref[h, i1:i1 + 1, lanes] & jnp.int32(-65536), F32),
            (ROW_TILE, LANES))
        c1 = [bcast(c1_ref, h) for h in range(HEADS)]
        e1 = [bcast(e1_ref, h) for h in range(HEADS)]
        for rt in range(HEAD_DIM // ROW_TILE):
            r0 = rt * ROW_TILE
            rows = slice(il * HEAD_DIM + r0, il * HEAD_DIM + r0 + ROW_TILE)
            terms = [jnp.where(key2_scr[key2_rows(ch, rt, h), :] < c1[h],
                               key2_scr[key2_rows(ch, rt, HEADS + h), :] * e1[h], 0.0)
                     for h in range(HEADS)]
            while len(terms) > 1:
                terms = [a + b for a, b in zip(terms[0::2], terms[1::2])]
            w = terms[0]
            hid = hid_cur[rows, lanes]
            gelu2 = hid * (1.0 + lax.erf(hid * 0.7071067811865476))
            g_scr[rows, lanes] = (w * gelu2).astype(BF16)

    def stage(hid_new, hid_cur, i1_base):
        width = min(MXU_TILE, tn)
        depth = min(MXU_TILE, te)
        for k0 in range(0, te, depth):
            for n0 in range(0, tn, width):
                hid_new[k0:k0 + depth, n0:n0 + width] = _dot(u_ref[0, k0:k0 + depth, :],
                                                             xnt_ref[:, n0:n0 + width])
                for il in range(k0 // HEAD_DIM, (k0 + depth) // HEAD_DIM):
                    for ch in range(n0 // LANES, (n0 + width) // LANES):
                        gate_group(hid_cur, i1_base + il, il, ch)
                acc_scr[:, n0:n0 + width] += _dot(vt_ref[0, 0, :, k0:k0 + depth],
                                                  g_scr[k0:k0 + depth, n0:n0 + width])

    group = c1_ref.shape[1]

    @pl.when(t % 2 == 0)
    def _():
        stage(hid0, hid1, nb % group)

    @pl.when(t % 2 == 1)
    def _():
        stage(hid1, hid0, 0)

    @pl.when(t == pl.num_programs(1) - 1)
    def _():
        y = x_ref[...] + acc_scr[...].T
        if final_norm:
            y = _rms(y, gfin_ref[...])
        y_ref[...] = y


def _peer_dense(xnt, u, vt, layer, rank2, e2, c1, e1, x, gfin, tn, final_norm):
    N, D = x.shape
    _, nblk, _, te = vt.shape
    assert N % tn == 0 and u.shape[1] == nblk * te and te % HEAD_DIM == 0
    nb = te // HEAD_DIM
    group = max(8, nb)
    assert group // nb <= 2 and HEAD_DIM % group == 0
    key2_blk = pl.BlockSpec((HEADS, HEAD_DIM, tn), lambda i, t: (0, 0, i))
    key1_blk = pl.BlockSpec(
        (HEADS, group, tn),
        lambda i, t: (0, jnp.clip(((t - 1) * nb) // group, 0, HEAD_DIM // group - 1), i))
    return pl.pallas_call(
        functools.partial(_peer_dense_kernel, final_norm=final_norm),
        grid=(N // tn, nblk + 1),
        in_specs=[pl.BlockSpec((D, tn), lambda i, t: (0, i)),
                  pl.BlockSpec((1, te, D), lambda i, t: (layer, jnp.minimum(t, nblk - 1), 0)),
                  pl.BlockSpec((1, 1, D, te),
                               lambda i, t: (layer, jnp.maximum(t - 1, 0), 0, 0)),
                  key2_blk, key2_blk, key1_blk, key1_blk,
                  pl.BlockSpec((tn, D), lambda i, t: (i, 0)),
                  pl.BlockSpec((1, D), lambda i, t: (0, 0))],
        out_specs=pl.BlockSpec((tn, D), lambda i, t: (i, 0)),
        out_shape=jax.ShapeDtypeStruct((N, D), F32),
        scratch_shapes=[pltpu.VMEM((te, tn), F32), pltpu.VMEM((te, tn), F32),
                        pltpu.VMEM((te, tn), BF16),
                        pltpu.VMEM((D, tn), F32),
                        pltpu.VMEM((2 * HEADS * HEAD_DIM * (tn // LANES), LANES), F32)],
        compiler_params=pltpu.CompilerParams(
            dimension_semantics=("arbitrary", "arbitrary"), vmem_limit_bytes=VMEM_LIMIT),
        name="peer_dense",
    )(xnt, u, vt, rank2, e2, c1, e1, x, gfin)


def _peer(x, g, wq_t, keys, u, vt, layer, gfin, tn_router, tn_dense, final_norm):
    xnt, rank2, e2, c1, e1 = _router(x, g, wq_t, keys, tn_router)
    return _peer_dense(xnt, u, vt, layer, rank2, e2, c1, e1, x, gfin, tn_dense, final_norm)


HGRN_CHUNK = 64
HGRN_SEQS = 2
HGRN_SAMPLE_BLOCK = 8
CONV_CHUNK = 512
CONV_SAMPLE_BLOCK = 32
PEER_ROUTER_TOKENS = 512
PEER_DENSE_TOKENS = 512
PEER_EXPERTS = 512


def kernel(x_prompt, x_sample, state_hgrn, state_conv, norm_mix, norm_ffn, norm_final, hg_w_in, hg_w_out, hg_gnorm, hg_lb_logits, cv_w_pw1, cv_b_pw1, cv_w_dw, cv_b_dw, cv_ln_g, cv_ln_b, cv_w_pw2, cv_b_pw2, peer_w_q, peer_sub_keys, peer_u, peer_v):
    B, T, D = x_prompt.shape
    NS = x_sample.shape[0]
    depth = norm_mix.shape[0]
    row = lambda a: a.reshape(1, -1)

    lb_all = jnp.cumsum(jax.nn.softmax(hg_lb_logits.astype(F32), axis=0), axis=0)

    u_all = peer_u.astype(BF16)
    vt_all = peer_v.reshape(depth, -1, PEER_EXPERTS, D).transpose(0, 1, 3, 2).astype(BF16)

    xp = x_prompt
    xs = x_sample.reshape(NS, D)
    hg_p, hg_s, cv_p, cv_s = [], [], [], []
    for i in range(depth):
        j = i // 2
        gmix = row(norm_mix[i])
        if i % 2 == 0:
            w_in = hg_w_in[j].astype(BF16)
            w_out = hg_w_out[j].astype(BF16)
            gn = row(hg_gnorm[j])
            lb = row(lb_all[i])
            xp, sp = _hgrn_prompt(xp, gmix, w_in, w_out, gn, lb, HGRN_CHUNK, HGRN_SEQS)
            xs, ss = _hgrn_sample(xs, gmix, w_in, w_out, gn, lb, state_hgrn[j],
                                  HGRN_SAMPLE_BLOCK)
            hg_p.append(sp)
            hg_s.append(ss)
        else:
            cw = (cv_w_pw1[j].astype(BF16), row(cv_b_pw1[j]), cv_w_dw[j], row(cv_b_dw[j]),
                  row(cv_ln_g[j]), row(cv_ln_b[j]), cv_w_pw2[j].astype(BF16), row(cv_b_pw2[j]))
            xp, bp = _conv_prompt(xp, gmix, *cw, CONV_CHUNK)
            xs, bs = _conv_sample(xs, gmix, *cw, state_conv[j], CONV_SAMPLE_BLOCK)
            cv_p.append(bp)
            cv_s.append(bs)
        wq_t = peer_w_q[i].T.astype(BF16)
        keys = peer_sub_keys[i].reshape(2 * HEADS, HEAD_DIM, HEAD_DIM).astype(BF16)
        gffn = row(norm_ffn[i])
        gfin = row(norm_final)
        last = i == depth - 1
        xp = _peer(xp.reshape(B * T, D), gffn, wq_t, keys, u_all, vt_all, i, gfin,
                   min(PEER_ROUTER_TOKENS, B * T), min(PEER_DENSE_TOKENS, B * T),
                   last).reshape(B, T, D)
        xs = _peer(xs, gffn, wq_t, keys, u_all, vt_all, i, gfin, min(LANES, NS), min(LANES, NS),
                   last)
    return (xp, xs.reshape(NS, 1, D), jnp.stack(hg_p), jnp.stack(hg_s),
            jnp.stack(cv_p), jnp.stack(cv_s))
```

```python
import functools

import jax
import jax.numpy as jnp
from jax import lax
from jax.experimental import pallas as pl
from jax.experimental.pallas import tpu as pltpu

F32 = jnp.float32
BF16 = jnp.bfloat16
EPS = 1e-6

HEADS = 8
HEAD_DIM = 128
TOPK = 16
CONV_WIDTH = 31
LANES = 128
SUB = 16
VMEM_LIMIT = 56 * 1024 * 1024

NEG_INF = float("-inf")


def _rms(x, g):
    return x * lax.rsqrt(jnp.mean(x * x, axis=-1, keepdims=True) + EPS) * g


def _sigmoid(x):
    return 1.0 / (1.0 + jnp.exp(-x))


def _dot(a, b):
    return jnp.dot(a, b, preferred_element_type=F32)


def _dot_nt(a, b):
    return lax.dot_general(a, b, (((1,), (1,)), ((), ())), preferred_element_type=F32)


def _hgrn_prompt_kernel(x_ref, gmix_ref, win_ref, wout_ref, gn_ref, lb_ref, tril_ref, ones_ref,
                        y_ref, s_ref,
                        st_scr, q_scr, k_scr, v_scr, b_scr, p_scr, a_scr, o_scr):
    R, C, D = x_ref.shape
    M = R * C
    nsub = C // SUB
    ci = pl.program_id(1)

    @pl.when(ci == 0)
    def _():
        st_scr[...] = jnp.zeros_like(st_scr)

    xc = x_ref[...].reshape(M, D)
    hn = _rms(xc, gmix_ref[...])
    proj = _dot(hn.astype(BF16), win_ref[...])
    qz = proj[:, 0:D]
    fz = proj[:, D:2 * D]
    gz = proj[:, 3 * D:4 * D]
    lb = lb_ref[...]
    f = lb + (1.0 - lb) * _sigmoid(fz)
    q_scr[...] = qz * _sigmoid(qz)
    k_scr[...] = 1.0 - f
    v_scr[...] = proj[:, 2 * D:3 * D]
    lf = jnp.log(f)
    hi = lf.astype(BF16)
    rem = lf - hi.astype(F32)
    mid = rem.astype(BF16)
    lo = (rem - mid.astype(F32)).astype(BF16)
    tril = tril_ref[...]
    b_scr[...] = _dot(tril, hi) + _dot(tril, mid) + _dot(tril, lo)

    qe = (q_scr[...] * jnp.exp(b_scr[...])).astype(BF16)
    for r in range(R):
        for h in range(HEADS):
            L = slice(h * HEAD_DIM, (h + 1) * HEAD_DIM)
            o_scr[r * C:(r + 1) * C, L] = _dot_nt(qe[r * C:(r + 1) * C, L],
                                                  st_scr[r * HEADS + h].astype(BF16))

    t_iota = lax.broadcasted_iota(jnp.int32, (SUB, D), 0)
    for r in range(R):
        base = r * C
        for I in range(nsub):
            rows = slice(base + SUB * I, base + SUB * (I + 1))
            prev = slice(base, base + SUB * I)
            bI = b_scr[rows, :]
            qI = q_scr[rows, :]
            if I > 0:
                beta = b_scr[base + SUB * I - 1:base + SUB * I, :]
                qs = (qI * jnp.exp(bI - beta)).astype(BF16)
                ks = (k_scr[prev, :] * jnp.exp(beta - b_scr[prev, :])).astype(BF16)
                vs = v_scr[prev, :].astype(BF16)
                for h in range(HEADS):
                    L = slice(h * HEAD_DIM, (h + 1) * HEAD_DIM)
                    att = _dot_nt(qs[:, L], ks[:, L])
                    o_scr[rows, L] += _dot(att.astype(BF16), vs[:, L])
            for s in range(SUB):
                src = base + SUB * I + s
                keep = t_iota >= s
                dec = jnp.exp(jnp.where(keep, bI - b_scr[src:src + 1, :], 0.0))
                p = jnp.where(keep, qI * dec * k_scr[src:src + 1, :], 0.0)
                p_scr[src * SUB:(src + 1) * SUB, :] = p.astype(BF16)

    for h in range(HEADS):
        L = slice(h * HEAD_DIM, (h + 1) * HEAD_DIM)
        a_scr[:, L] = _dot(p_scr[:, L], ones_ref[...])
    for J in range(M // SUB):
        rows = slice(SUB * J, SUB * (J + 1))
        acc = o_scr[rows, :]
        for s in range(SUB):
            src = SUB * J + s
            acc = acc + a_scr[src * SUB:(src + 1) * SUB, :] * v_scr[src:src + 1, :]
        o_scr[rows, :] = acc

    for r in range(R):
        rows = slice(r * C, (r + 1) * C)
        b_last = b_scr[(r + 1) * C - 1:(r + 1) * C, :]
        kd = (k_scr[rows, :] * jnp.exp(b_last - b_scr[rows, :])).astype(BF16)
        eb_last = jnp.exp(b_last)
        for h in range(HEADS):
            L = slice(h * HEAD_DIM, (h + 1) * HEAD_DIM)
            vt = v_scr[rows, L].T.astype(BF16)
            st_scr[r * HEADS + h] = st_scr[r * HEADS + h] * eb_last[:, L] + _dot(vt, kd[:, L])

    gn = gn_ref[...]
    for h in range(HEADS):
        L = slice(h * HEAD_DIM, (h + 1) * HEAD_DIM)
        oh = o_scr[:, L]
        gzh = gz[:, L]
        o_scr[:, L] = _rms(oh, gn) * (gzh * _sigmoid(gzh))
    y_ref[...] = (xc + _dot(o_scr[...].astype(BF16), wout_ref[...])).reshape(R, C, D)

    @pl.when(ci == pl.num_programs(1) - 1)
    def _():
        for r in range(R):
            for h in range(HEADS):
                s_ref[r, h] = st_scr[r * HEADS + h].T


def _hgrn_prompt(x, gmix, w_in, w_out, gnorm, lb, chunk, seqs):
    B, T, D = x.shape
    C = min(chunk, T)
    R = min(seqs, B)
    assert T % C == 0 and C % SUB == 0 and B % R == 0
    M = R * C
    tril = jnp.kron(jnp.eye(R, dtype=F32), jnp.tril(jnp.ones((C, C), F32))).astype(BF16)
    ones = jnp.ones((HEAD_DIM, HEAD_DIM), BF16)
    const = lambda shape: pl.BlockSpec(shape, lambda b, c: (0,) * len(shape))
    return pl.pallas_call(
        _hgrn_prompt_kernel,
        grid=(B // R, T // C),
        in_specs=[
            pl.BlockSpec((R, C, D), lambda b, c: (b, c, 0)),
            const((1, D)), const((D, 4 * D)), const((D, D)), const((1, HEAD_DIM)), const((1, D)),
            const((M, M)), const((HEAD_DIM, HEAD_DIM)),
        ],
        out_specs=[
            pl.BlockSpec((R, C, D), lambda b, c: (b, c, 0)),
            pl.BlockSpec((R, HEADS, HEAD_DIM, HEAD_DIM), lambda b, c: (b, 0, 0, 0)),
        ],
        out_shape=[
            jax.ShapeDtypeStruct((B, T, D), F32),
            jax.ShapeDtypeStruct((B, HEADS, HEAD_DIM, HEAD_DIM), F32),
        ],
        scratch_shapes=[
            pltpu.VMEM((R * HEADS, HEAD_DIM, HEAD_DIM), F32),
            pltpu.VMEM((M, D), F32),
            pltpu.VMEM((M, D), F32),
            pltpu.VMEM((M, D), F32),
            pltpu.VMEM((M, D), F32),
            pltpu.VMEM((M * SUB, D), BF16),
            pltpu.VMEM((M * SUB, D), F32),
            pltpu.VMEM((M, D), F32),
        ],
        compiler_params=pltpu.CompilerParams(
            dimension_semantics=("arbitrary", "arbitrary"), vmem_limit_bytes=VMEM_LIMIT),
        name="hgrn_prompt",
    )(x, gmix, w_in, w_out, gnorm, lb, tril, ones)


def _hgrn_sample_kernel(x_ref, gmix_ref, win_ref, wout_ref, gn_ref, lb_ref, s_ref,
                        y_ref, snew_ref, proj_scr, o_scr):
    n = pl.program_id(0)
    D = x_ref.shape[1]

    @pl.when(n == 0)
    def _():
        hn = _rms(x_ref[...], gmix_ref[...])
        proj_scr[...] = _dot(hn.astype(BF16), win_ref[...])

    ts = s_ref.shape[0]
    tok0 = pl.multiple_of(n * ts, ts)
    rows = proj_scr[pl.ds(tok0, ts), :]
    qz = rows[:, 0:D]
    fz = rows[:, D:2 * D]
    v = rows[:, 2 * D:3 * D]
    lb = lb_ref[...]
    f = lb + (1.0 - lb) * _sigmoid(fz)
    q = qz * _sigmoid(qz)
    k = 1.0 - f

    def cols(a):
        return jnp.concatenate([a, jnp.zeros((HEAD_DIM - ts, HEAD_DIM), F32)], axis=0).T

    for h in range(HEADS):
        L = slice(h * HEAD_DIM, (h + 1) * HEAD_DIM)
        fc = cols(f[:, L])
        kc = cols(k[:, L])
        qc = cols(q[:, L])
        outs = []
        for j in range(ts):
            s_new = s_ref[j, h] * fc[:, j:j + 1] + kc[:, j:j + 1] * v[j:j + 1, L]
            snew_ref[j, h] = s_new
            outs.append(jnp.sum(qc[:, j:j + 1] * s_new, axis=0, keepdims=True))
        o_scr[pl.ds(tok0, ts), L] = jnp.concatenate(outs, axis=0)

    @pl.when(n == pl.num_programs(0) - 1)
    def _():
        gn = gn_ref[...]
        gz = proj_scr[:, 3 * D:4 * D]
        for h in range(HEADS):
            L = slice(h * HEAD_DIM, (h + 1) * HEAD_DIM)
            gzh = gz[:, L]
            o_scr[:, L] = _rms(o_scr[:, L], gn) * (gzh * _sigmoid(gzh))
        y_ref[...] = x_ref[...] + _dot(o_scr[...].astype(BF16), wout_ref[...])


def _hgrn_sample(x, gmix, w_in, w_out, gnorm, lb, state, block):
    N, D = x.shape
    ts = min(block, N)
    assert N % ts == 0 and ts % 8 == 0
    const = lambda shape: pl.BlockSpec(shape, lambda n: (0,) * len(shape))
    sblk = pl.BlockSpec((ts, HEADS, HEAD_DIM, HEAD_DIM), lambda n: (n, 0, 0, 0))
    return pl.pallas_call(
        _hgrn_sample_kernel,
        grid=(N // ts,),
        in_specs=[const((N, D)), const((1, D)), const((D, 4 * D)), const((D, D)),
                  const((1, HEAD_DIM)), const((1, D)), sblk],
        out_specs=[const((N, D)), sblk],
        out_shape=[jax.ShapeDtypeStruct((N, D), F32),
                   jax.ShapeDtypeStruct((N, HEADS, HEAD_DIM, HEAD_DIM), F32)],
        scratch_shapes=[pltpu.VMEM((N, 4 * D), F32), pltpu.VMEM((N, D), F32)],
        compiler_params=pltpu.CompilerParams(
            dimension_semantics=("arbitrary",), vmem_limit_bytes=VMEM_LIMIT),
        name="hgrn_sample",
    )(x, gmix, w_in, w_out, gnorm, lb, state)


CONV_PAD = 32


def _conv_tail(y, xc, bdw_ref, lng_ref, lnb_ref, pw2_ref, bpw2_ref):
    y = y + bdw_ref[...]
    mu = jnp.mean(y, axis=-1, keepdims=True)
    yc = y - mu
    var = jnp.mean(yc * yc, axis=-1, keepdims=True)
    yn = yc * lax.rsqrt(var + EPS) * lng_ref[...] + lnb_ref[...]
    z = yn * _sigmoid(yn)
    return xc + _dot(z.astype(BF16), pw2_ref[...]) + bpw2_ref[...]


def _glu(xc, gmix_ref, pw1_ref, bpw1_ref):
    D = xc.shape[1]
    hn = _rms(xc, gmix_ref[...])
    h2 = _dot(hn.astype(BF16), pw1_ref[...]) + bpw1_ref[...]
    return h2[:, 0:D] * _sigmoid(h2[:, D:2 * D])


def _conv_prompt_kernel(x_ref, gmix_ref, pw1_ref, bpw1_ref, wdw_ref, bdw_ref, lng_ref, lnb_ref,
                        pw2_ref, bpw2_ref, y_ref, buf_ref, full_scr, sh_scr, y_scr):
    C = x_ref.shape[1]
    D = x_ref.shape[2]
    ci = pl.program_id(1)
    hist = CONV_WIDTH - 1
    off = CONV_PAD - hist
    sh_rows = sh_scr.shape[1]

    @pl.when(ci == 0)
    def _():
        full_scr[0:CONV_PAD, :] = jnp.zeros((CONV_PAD, full_scr.shape[1]), F32)

    xc = x_ref[0]
    full_scr[CONV_PAD:CONV_PAD + C, :] = _glu(xc, gmix_ref, pw1_ref, bpw1_ref)

    def lane_chunk(c, carry):
        lanes = pl.ds(pl.multiple_of(c * LANES, LANES), LANES)
        for b in range(1, 8):
            sh_scr[b - 1, :, lanes] = full_scr[b:b + sh_rows, lanes]
        w = [jnp.broadcast_to(wdw_ref[j:j + 1, lanes], (8, LANES)) for j in range(CONV_WIDTH)]
        for r0 in range(0, C, 8):
            acc = None
            for j in range(CONV_WIDTH):
                a, b = divmod(off + j, 8)
                rows = slice(8 * a + r0, 8 * a + r0 + 8)
                src = full_scr[rows, lanes] if b == 0 else sh_scr[b - 1, rows, lanes]
                acc = src * w[j] if acc is None else acc + src * w[j]
            y_scr[r0:r0 + 8, lanes] = acc
        return carry

    lax.fori_loop(0, D // LANES, lane_chunk, 0)
    y_ref[0] = _conv_tail(y_scr[...], xc, bdw_ref, lng_ref, lnb_ref, pw2_ref, bpw2_ref)

    @pl.when(ci == pl.num_programs(1) - 1)
    def _():
        buf_ref[0] = full_scr[C + off:C + CONV_PAD, :]

    full_scr[0:CONV_PAD, :] = full_scr[C:C + CONV_PAD, :]


def _conv_prompt(x, gmix, pw1, bpw1, wdw, bdw, lng, lnb, pw2, bpw2, chunk):
    B, T, D = x.shape
    C = min(chunk, T)
    assert T % C == 0 and C >= CONV_PAD
    hist = CONV_WIDTH - 1
    const = lambda shape: pl.BlockSpec(shape, lambda b, c: (0,) * len(shape))
    return pl.pallas_call(
        _conv_prompt_kernel,
        grid=(B, T // C),
        in_specs=[pl.BlockSpec((1, C, D), lambda b, c: (b, c, 0)),
                  const((1, D)), const((D, 2 * D)), const((1, 2 * D)), const((CONV_WIDTH, D)),
                  const((1, D)), const((1, D)), const((1, D)), const((D, D)), const((1, D))],
        out_specs=[pl.BlockSpec((1, C, D), lambda b, c: (b, c, 0)),
                   pl.BlockSpec((1, hist, D), lambda b, c: (b, 0, 0))],
        out_shape=[jax.ShapeDtypeStruct((B, T, D), F32),
                   jax.ShapeDtypeStruct((B, hist, D), F32)],
        scratch_shapes=[pltpu.VMEM((CONV_PAD + C, D), F32),
                        pltpu.VMEM((7, C + CONV_PAD - 8, D), F32),
                        pltpu.VMEM((C, D), F32)],
        compiler_params=pltpu.CompilerParams(
            dimension_semantics=("arbitrary", "arbitrary"), vmem_limit_bytes=VMEM_LIMIT),
        name="conv_prompt",
    )(x, gmix, pw1, bpw1, wdw, bdw, lng, lnb, pw2, bpw2)


def _conv_sample_kernel(x_ref, gmix_ref, pw1_ref, bpw1_ref, wdw_ref, bdw_ref, lng_ref, lnb_ref,
                        pw2_ref, bpw2_ref, buf_ref, y_ref, nbuf_ref, u_scr, y_scr):
    nt = x_ref.shape[0]
    hist = CONV_WIDTH - 1
    xc = x_ref[...]
    u_scr[...] = _glu(xc, gmix_ref, pw1_ref, bpw1_ref)
    w_hist = wdw_ref[0:hist, :]
    w_last = wdw_ref[hist:hist + 1, :]

    def body(n, carry):
        buf = buf_ref[n]
        u = u_scr[pl.ds(n, 1), :]
        y_scr[pl.ds(n, 1), :] = jnp.sum(buf * w_hist, axis=0, keepdims=True) + u * w_last
        nbuf_ref[n, 0:hist - 1, :] = buf[1:hist, :]
        nbuf_ref[n, hist - 1:hist, :] = u
        return carry

    lax.fori_loop(0, nt, body, 0)
    y_ref[...] = _conv_tail(y_scr[...], xc, bdw_ref, lng_ref, lnb_ref, pw2_ref, bpw2_ref)


def _conv_sample(x, gmix, pw1, bpw1, wdw, bdw, lng, lnb, pw2, bpw2, buf, block):
    N, D = x.shape
    nt = min(block, N)
    assert N % nt == 0
    hist = CONV_WIDTH - 1
    const = lambda shape: pl.BlockSpec(shape, lambda i: (0,) * len(shape))
    return pl.pallas_call(
        _conv_sample_kernel,
        grid=(N // nt,),
        in_specs=[pl.BlockSpec((nt, D), lambda i: (i, 0)),
                  const((1, D)), const((D, 2 * D)), const((1, 2 * D)), const((CONV_WIDTH, D)),
                  const((1, D)), const((1, D)), const((1, D)), const((D, D)), const((1, D)),
                  pl.BlockSpec((nt, hist, D), lambda i: (i, 0, 0))],
        out_specs=[pl.BlockSpec((nt, D), lambda i: (i, 0)),
                   pl.BlockSpec((nt, hist, D), lambda i: (i, 0, 0))],
        out_shape=[jax.ShapeDtypeStruct((N, D), F32),
                   jax.ShapeDtypeStruct((N, hist, D), F32)],
        scratch_shapes=[pltpu.VMEM((nt, D), F32), pltpu.VMEM((nt, D), F32)],
        compiler_params=pltpu.CompilerParams(
            dimension_semantics=("arbitrary",), vmem_limit_bytes=VMEM_LIMIT),
        name="conv_sample",
    )(x, gmix, pw1, bpw1, wdw, bdw, lng, lnb, pw2, bpw2, buf)


def _topk_ranks(vals, ids, k):
    rank = jnp.full(vals.shape, float(k), F32)
    tops = []
    for r in range(k):
        m = jnp.max(vals, axis=0, keepdims=True)
        first = jnp.min(jnp.where(vals == m, ids, 1e9), axis=0, keepdims=True)
        sel = ids == first
        vals = jnp.where(sel, NEG_INF, vals)
        rank = jnp.where(sel, float(r), rank)
        tops.append(m)
    return rank, tops


INT_MIN = -2 ** 31


def _sort_key(s):
    b = lax.bitcast_convert_type(jnp.where(s == 0.0, 0.0, s), jnp.int32)
    return b ^ (lax.shift_right_arithmetic(b, 31) & jnp.int32(0x7FFFFFFF))


def _key_value(k):
    b = k ^ (lax.shift_right_arithmetic(k, 31) & jnp.int32(0x7FFFFFFF))
    return lax.bitcast_convert_type(b, F32)


def _topk_keys(keys, k):
    tops = []
    for r in range(k):
        m = jnp.max(keys, axis=0, keepdims=True)
        keys = jnp.where(keys == m, jnp.int32(INT_MIN + r), keys)
        tops.append(m)
    return keys, tops


def _selected(keys, k):
    sel = keys < jnp.int32(INT_MIN + k)
    n = jnp.sum(jnp.where(sel, 1.0, 0.0), axis=0, keepdims=True)
    return sel, jnp.where(n != float(k), 1.0, 0.0)


def _dup_bf16(x):
    u = lax.bitcast_convert_type(x.astype(BF16).astype(F32), jnp.int32)
    return u | lax.shift_right_logical(u, 16)


def _router_kernel(x_ref, g_ref, wq_ref, keys_ref,
                   xnt_ref, rank2_ref, e2_ref, c1_ref, e1_ref,
                   s_scr, rank_scr, top_scr, miss_scr):
    tn = x_ref.shape[0]
    xn = _rms(x_ref[...], g_ref[...])
    xnt = xn.T.astype(BF16)
    xnt_ref[...] = xnt
    qt = _dot(wq_ref[...], xnt).astype(BF16)
    for hp in range(2 * HEADS):
        s_scr[hp] = _dot(keys_ref[hp], qt[hp * HEAD_DIM:(hp + 1) * HEAD_DIM, :])

    outs = (rank2_ref, e2_ref, c1_ref, e1_ref)
    scr = (s_scr, rank_scr, top_scr)
    _select_experts(tn, scr, outs, miss_scr)


def _run_with_tie_fallback(n, per_group, problem, miss_scr):
    def fast(g, carry):
        miss = [problem(g * per_group + c, False) for c in range(per_group)]
        miss_scr[pl.ds(g, 1), :] = functools.reduce(jnp.maximum, miss)
        return carry

    lax.fori_loop(0, n // per_group, fast, 0, unroll=2 if per_group == 1 else 1)

    def fix(g, carry):
        @pl.when(jnp.max(miss_scr[pl.ds(g, 1), :]) > 0.0)
        def _():
            for c in range(per_group):
                problem(g * per_group + c, True)
        return carry

    lax.fori_loop(0, n // per_group, fix, 0)


def _select_experts(tn, scr, outs, miss_scr):
    s_scr, rank_scr, top_scr = scr
    rank2_ref, e2_ref, c1_ref, e1_ref = outs
    nch = tn // LANES
    key_ids = lax.broadcasted_iota(jnp.int32, (HEAD_DIM, LANES), 0).astype(F32)

    def half_problem(it, exact):
        hp = it // nch
        lanes = pl.ds(pl.multiple_of((it % nch) * LANES, LANES), LANES)
        s = s_scr[hp, :, lanes]
        miss = None
        if exact:
            rank, tops = _topk_ranks(s, key_ids, TOPK)
            top = jnp.concatenate(tops, axis=0)
        else:
            keys, tops = _topk_keys(_sort_key(s), TOPK)
            sel, miss = _selected(keys, TOPK)
            rank = jnp.where(sel, (keys ^ jnp.int32(INT_MIN)).astype(F32), float(TOPK))
            top = _key_value(jnp.concatenate(tops, axis=0))
        rank_scr[hp, :, lanes] = rank
        top_scr[hp, :, lanes] = top
        return miss

    _run_with_tie_fallback(2 * HEADS * nch, nch, half_problem, miss_scr)

    row16 = lax.broadcasted_iota(jnp.int32, (16, LANES), 0)
    row8 = lax.broadcasted_iota(jnp.int32, (8, LANES), 0)
    r16 = row16.astype(F32)
    r8 = row8.astype(F32)
    groups_valid = [row16 >= 0, row8 >= 0, row16 >= 2, row8 >= 2,
                    (row8 >= 2) & (row8 <= 4), (row8 >= 2) & (row8 <= 3), row8 == 2]
    groups_ids = [r16, 16.0 + r8, r16 * 16.0, r8 * 16.0 + 1.0, 32.0 + r8, 48.0 + r8, 64.0 + r8]
    cand_ids = jnp.concatenate(
        [jnp.where(v, i, 1e9) for v, i in zip(groups_valid, groups_ids)], axis=0)

    def head_problem(it, exact):
        h = it // nch
        lanes = pl.ds(pl.multiple_of((it % nch) * LANES, LANES), LANES)
        v1 = top_scr[2 * h, :, lanes]
        v2 = top_scr[2 * h + 1, :, lanes]
        groups = [v1[0:1] + v2, v1[1:2] + v2[0:8], v1 + v2[0:1], v1[0:8] + v2[1:2],
                  v1[2:3] + v2[0:8], v1[3:4] + v2[0:8], v1[4:5] + v2[0:8]]
        cand = jnp.concatenate(
            [jnp.where(v, g, NEG_INF) for v, g in zip(groups_valid, groups)], axis=0)
        miss = None
        if exact:
            crank, _ = _topk_ranks(cand, cand_ids, TOPK)
            sel = crank < float(TOPK)
        else:
            ckeys, _ = _topk_keys(_sort_key(cand), TOPK)
            sel, miss = _selected(ckeys, TOPK)
        z = jnp.sum(jnp.where(sel, jnp.exp(jnp.where(sel, cand - cand[0:1], 0.0)), 0.0),
                    axis=0, keepdims=True)
        self32 = jnp.where(sel, 1.0, 0.0)
        cnt = lambda a, b: jnp.sum(self32[a:b], axis=0, keepdims=True)
        per_r1 = (self32[24:40]
                  + jnp.concatenate([self32[40:48], jnp.zeros((8, LANES), F32)], axis=0)
                  + jnp.where(row16 == 0, cnt(0, 16), 0.0)
                  + jnp.where(row16 == 1, cnt(16, 24), 0.0)
                  + jnp.where(row16 == 2, cnt(48, 56), 0.0)
                  + jnp.where(row16 == 3, cnt(56, 64), 0.0)
                  + jnp.where(row16 == 4, cnt(64, 72), 0.0))
        rank1 = rank_scr[2 * h, :, lanes]
        rank2 = rank_scr[2 * h + 1, :, lanes]
        c1 = jnp.zeros((HEAD_DIM, LANES), F32)
        for j in range(1, TOPK // 3 + 1):
            reach = jnp.sum(jnp.where(per_r1 >= float(j), 1.0, 0.0), axis=0, keepdims=True)
            c1 = c1 + jnp.where(rank1 < reach, 1.0, 0.0)
        c1 = jnp.where(rank1 == 0.0, per_r1[0:1], jnp.where(rank1 == 1.0, per_r1[1:2], c1))
        s1 = s_scr[2 * h, :, lanes]
        s2 = s_scr[2 * h + 1, :, lanes]
        in1 = rank1 < float(TOPK)
        in2 = rank2 < float(TOPK)
        e1 = jnp.where(in1, jnp.exp(jnp.where(in1, s1 - v1[0:1], 0.0)), 0.0)
        e2 = jnp.where(in2, jnp.exp(jnp.where(in2, s2 - v2[0:1], 0.0)), 0.0) / z
        rank2_ref[h, :, lanes] = rank2.astype(BF16)
        e2_ref[h, :, lanes] = (0.5 * e2).astype(BF16)
        c1_ref[h, :, lanes] = _dup_bf16(c1)
        e1_ref[h, :, lanes] = _dup_bf16(e1)
        return miss

    _run_with_tie_fallback(HEADS * nch, nch, head_problem, miss_scr)


def _router(x, g, wq_t, keys, tn):
    N, D = x.shape
    assert N % tn == 0 and tn % LANES == 0
    const = lambda shape: pl.BlockSpec(shape, lambda i: (0,) * len(shape))
    gate_blk = pl.BlockSpec((HEADS, HEAD_DIM, tn), lambda i: (0, 0, i))
    gate_shape = lambda dt: jax.ShapeDtypeStruct((HEADS, HEAD_DIM, N), dt)
    return pl.pallas_call(
        _router_kernel,
        grid=(N // tn,),
        in_specs=[pl.BlockSpec((tn, D), lambda i: (i, 0)), const((1, D)),
                  const(wq_t.shape), const(keys.shape)],
        out_specs=[pl.BlockSpec((D, tn), lambda i: (0, i)),
                   gate_blk, gate_blk, gate_blk, gate_blk],
        out_shape=[jax.ShapeDtypeStruct((D, N), BF16),
                   gate_shape(BF16), gate_shape(BF16), gate_shape(jnp.int32),
                   gate_shape(jnp.int32)],
        scratch_shapes=[pltpu.VMEM((2 * HEADS, HEAD_DIM, tn), F32),
                        pltpu.VMEM((2 * HEADS, HEAD_DIM, tn), F32),
                        pltpu.VMEM((2 * HEADS, TOPK, tn), F32),
                        pltpu.VMEM((2 * HEADS * (tn // LANES), LANES), F32)],
        compiler_params=pltpu.CompilerParams(
            dimension_semantics=("arbitrary",), vmem_limit_bytes=VMEM_LIMIT),
        name="peer_router",
    )(x, g, wq_t, keys)


ROW_TILE = 16
MXU_TILE = 256


def _peer_dense_kernel(xnt_ref, u_ref, vt_ref, rank2_ref, e2_ref, c1_ref, e1_ref, x_ref, gfin_ref,
                       y_ref, hid0, hid1, g_scr, acc_scr, key2_scr, *, final_norm):
    t = pl.program_id(1)
    te, tn = hid0.shape
    nb = te // HEAD_DIM

    def key2_rows(ch, rt, slot):
        tile = (ch * (HEAD_DIM // ROW_TILE) + rt) * 2 * HEADS + slot
        return slice(tile * ROW_TILE, (tile + 1) * ROW_TILE)

    @pl.when(t == 0)
    def _():
        for ref in (hid0, hid1, acc_scr):
            ref[...] = jnp.zeros_like(ref)
        for ch in range(tn // LANES):
            lanes = slice(ch * LANES, (ch + 1) * LANES)
            for rt in range(HEAD_DIM // ROW_TILE):
                keys2 = slice(rt * ROW_TILE, (rt + 1) * ROW_TILE)
                for h in range(HEADS):
                    key2_scr[key2_rows(ch, rt, h), :] = rank2_ref[h, keys2, lanes].astype(F32)
                    key2_scr[key2_rows(ch, rt, HEADS + h), :] = (
                        e2_ref[h, keys2, lanes].astype(F32))

    def gate_group(hid_cur, i1, il, ch):
        lanes = slice(ch * LANES, (ch + 1) * LANES)
        bcast = lambda ref, h: jnp.broadcast_to(
            lax.bitcast_convert_type(ref[h, i1:i1 + 1, lanes] & jnp.int32(-65536), F32),
            (ROW_TILE, LANES))
        c1 = [bcast(c1_ref, h) for h in range(HEADS)]
        e1 = [bcast(e1_ref, h) for h in range(HEADS)]
        for rt in range(HEAD_DIM // ROW_TILE):
            r0 = rt * ROW_TILE
            rows = slice(il * HEAD_DIM + r0, il * HEAD_DIM + r0 + ROW_TILE)
            w = jnp.zeros((ROW_TILE, LANES), F32)
            for h in range(HEADS):
                w = w + jnp.where(key2_scr[key2_rows(ch, rt, h), :] < c1[h],
                                  key2_scr[key2_rows(ch, rt, HEADS + h), :] * e1[h], 0.0)
            hid = hid_cur[rows, lanes]
            gelu2 = hid * (1.0 + lax.erf(hid * 0.7071067811865476))
            g_scr[rows, lanes] = (w * gelu2).astype(BF16)

    def stage(hid_new, hid_cur, i1_base):
        width = min(MXU_TILE, tn)
        depth = min(MXU_TILE, te)
        for k0 in range(0, te, depth):
            for n0 in range(0, tn, width):
                hid_new[k0:k0 + depth, n0:n0 + width] = _dot(u_ref[0, k0:k0 + depth, :],
                                                             xnt_ref[:, n0:n0 + width])
                for il in range(k0 // HEAD_DIM, (k0 + depth) // HEAD_DIM):
                    for ch in range(n0 // LANES, (n0 + width) // LANES):
                        gate_group(hid_cur, i1_base + il, il, ch)
                acc_scr[:, n0:n0 + width] += _dot(vt_ref[0, 0, :, k0:k0 + depth],
                                                  g_scr[k0:k0 + depth, n0:n0 + width])

    group = c1_ref.shape[1]

    @pl.when(t % 2 == 0)
    def _():
        stage(hid0, hid1, nb % group)

    @pl.when(t % 2 == 1)
    def _():
        stage(hid1, hid0, 0)

    @pl.when(t == pl.num_programs(1) - 1)
    def _():
        y = x_ref[...] + acc_scr[...].T
        if final_norm:
            y = _rms(y, gfin_ref[...])
        y_ref[...] = y


def _peer_dense(xnt, u, vt, layer, rank2, e2, c1, e1, x, gfin, tn, final_norm):
    N, D = x.shape
    _, nblk, _, te = vt.shape
    assert N % tn == 0 and u.shape[1] == nblk * te and te % HEAD_DIM == 0
    nb = te // HEAD_DIM
    group = max(8, nb)
    assert group // nb <= 2 and HEAD_DIM % group == 0
    key2_blk = pl.BlockSpec((HEADS, HEAD_DIM, tn), lambda i, t: (0, 0, i))
    key1_blk = pl.BlockSpec(
        (HEADS, group, tn),
        lambda i, t: (0, jnp.clip(((t - 1) * nb) // group, 0, HEAD_DIM // group - 1), i))
    return pl.pallas_call(
        functools.partial(_peer_dense_kernel, final_norm=final_norm),
        grid=(N // tn, nblk + 1),
        in_specs=[pl.BlockSpec((D, tn), lambda i, t: (0, i)),
                  pl.BlockSpec((1, te, D), lambda i, t: (layer, jnp.minimum(t, nblk - 1), 0)),
                  pl.BlockSpec((1, 1, D, te),
                               lambda i, t: (layer, jnp.maximum(t - 1, 0), 0, 0)),
                  key2_blk, key2_blk, key1_blk, key1_blk,
                  pl.BlockSpec((tn, D), lambda i, t: (i, 0)),
                  pl.BlockSpec((1, D), lambda i, t: (0, 0))],
        out_specs=pl.BlockSpec((tn, D), lambda i, t: (i, 0)),
        out_shape=jax.ShapeDtypeStruct((N, D), F32),
        scratch_shapes=[pltpu.VMEM((te, tn), F32), pltpu.VMEM((te, tn), F32),
                        pltpu.VMEM((te, tn), BF16),
                        pltpu.VMEM((D, tn), F32),
                        pltpu.VMEM((2 * HEADS * HEAD_DIM * (tn // LANES), LANES), F32)],
        compiler_params=pltpu.CompilerParams(
            dimension_semantics=("arbitrary", "arbitrary"), vmem_limit_bytes=VMEM_LIMIT),
        name="peer_dense",
    )(xnt, u, vt, rank2, e2, c1, e1, x, gfin)


def _peer(x, g, wq_t, keys, u, vt, layer, gfin, tn_router, tn_dense, final_norm):
    xnt, rank2, e2, c1, e1 = _router(x, g, wq_t, keys, tn_router)
    return _peer_dense(xnt, u, vt, layer, rank2, e2, c1, e1, x, gfin, tn_dense, final_norm)


HGRN_CHUNK = 64
HGRN_SEQS = 2
HGRN_SAMPLE_BLOCK = 8
CONV_CHUNK = 512
CONV_SAMPLE_BLOCK = 32
PEER_ROUTER_TOKENS = 512
PEER_DENSE_TOKENS = 512
PEER_EXPERTS = 1024


def kernel(x_prompt, x_sample, state_hgrn, state_conv, norm_mix, norm_ffn, norm_final, hg_w_in, hg_w_out, hg_gnorm, hg_lb_logits, cv_w_pw1, cv_b_pw1, cv_w_dw, cv_b_dw, cv_ln_g, cv_ln_b, cv_w_pw2, cv_b_pw2, peer_w_q, peer_sub_keys, peer_u, peer_v):
    B, T, D = x_prompt.shape
    NS = x_sample.shape[0]
    depth = norm_mix.shape[0]
    row = lambda a: a.reshape(1, -1)

    lb_all = jnp.cumsum(jax.nn.softmax(hg_lb_logits.astype(F32), axis=0), axis=0)

    u_all = peer_u.astype(BF16)
    vt_all = peer_v.reshape(depth, -1, PEER_EXPERTS, D).transpose(0, 1, 3, 2).astype(BF16)

    xp = x_prompt
    xs = x_sample.reshape(NS, D)
    hg_p, hg_s, cv_p, cv_s = [], [], [], []
    for i in range(depth):
        j = i // 2
        gmix = row(norm_mix[i])
        if i % 2 == 0:
            w_in = hg_w_in[j].astype(BF16)
            w_out = hg_w_out[j].astype(BF16)
            gn = row(hg_gnorm[j])
            lb = row(lb_all[i])
            xp, sp = _hgrn_prompt(xp, gmix, w_in, w_out, gn, lb, HGRN_CHUNK, HGRN_SEQS)
            xs, ss = _hgrn_sample(xs, gmix, w_in, w_out, gn, lb, state_hgrn[j],
                                  HGRN_SAMPLE_BLOCK)
            hg_p.append(sp)
            hg_s.append(ss)
        else:
            cw = (cv_w_pw1[j].astype(BF16), row(cv_b_pw1[j]), cv_w_dw[j], row(cv_b_dw[j]),
                  row(cv_ln_g[j]), row(cv_ln_b[j]), cv_w_pw2[j].astype(BF16), row(cv_b_pw2[j]))
            xp, bp = _conv_prompt(xp, gmix, *cw, CONV_CHUNK)
            xs, bs = _conv_sample(xs, gmix, *cw, state_conv[j], CONV_SAMPLE_BLOCK)
            cv_p.append(bp)
            cv_s.append(bs)
        wq_t = peer_w_q[i].T.astype(BF16)
        keys = peer_sub_keys[i].reshape(2 * HEADS, HEAD_DIM, HEAD_DIM).astype(BF16)
        gffn = row(norm_ffn[i])
        gfin = row(norm_final)
        last = i == depth - 1
        xp = _peer(xp.reshape(B * T, D), gffn, wq_t, keys, u_all, vt_all, i, gfin,
                   min(PEER_ROUTER_TOKENS, B * T), min(PEER_DENSE_TOKENS, B * T),
                   last).reshape(B, T, D)
        xs = _peer(xs, gffn, wq_t, keys, u_all, vt_all, i, gfin, min(LANES, NS), min(LANES, NS),
                   last)
    return (xp, xs.reshape(NS, 1, D), jnp.stack(hg_p), jnp.stack(hg_s),
            jnp.stack(cv_p), jnp.stack(cv_s))
```

```python
import functools

import jax
import jax.numpy as jnp
from jax import lax
from jax.experimental import pallas as pl
from jax.experimental.pallas import tpu as pltpu

F32 = jnp.float32
BF16 = jnp.bfloat16
EPS = 1e-6

HEADS = 8
HEAD_DIM = 128
TOPK = 16
CONV_WIDTH = 31
LANES = 128
SUB = 16
VMEM_LIMIT = 56 * 1024 * 1024

NEG_INF = float("-inf")


def _rms(x, g):
    return x * lax.rsqrt(jnp.mean(x * x, axis=-1, keepdims=True) + EPS) * g


def _sigmoid(x):
    return 1.0 / (1.0 + jnp.exp(-x))


def _dot(a, b):
    return jnp.dot(a, b, preferred_element_type=F32)


def _dot_nt(a, b):
    return lax.dot_general(a, b, (((1,), (1,)), ((), ())), preferred_element_type=F32)


def _hgrn_prompt_kernel(x_ref, gmix_ref, win_ref, wout_ref, gn_ref, lb_ref, tril_ref, ones_ref,
                        y_ref, s_ref,
                        st_scr, q_scr, k_scr, v_scr, b_scr, p_scr, a_scr, o_scr):
    R, C, D = x_ref.shape
    M = R * C
    nsub = C // SUB
    ci = pl.program_id(1)

    @pl.when(ci == 0)
    def _():
        st_scr[...] = jnp.zeros_like(st_scr)

    xc = x_ref[...].reshape(M, D)
    hn = _rms(xc, gmix_ref[...])
    proj = _dot(hn.astype(BF16), win_ref[...])
    qz = proj[:, 0:D]
    fz = proj[:, D:2 * D]
    gz = proj[:, 3 * D:4 * D]
    lb = lb_ref[...]
    f = lb + (1.0 - lb) * _sigmoid(fz)
    q_scr[...] = qz * _sigmoid(qz)
    k_scr[...] = 1.0 - f
    v_scr[...] = proj[:, 2 * D:3 * D]
    lf = jnp.log(f)
    hi = lf.astype(BF16)
    rem = lf - hi.astype(F32)
    mid = rem.astype(BF16)
    lo = (rem - mid.astype(F32)).astype(BF16)
    tril = tril_ref[...]
    b_scr[...] = _dot(tril, hi) + _dot(tril, mid) + _dot(tril, lo)

    qe = (q_scr[...] * jnp.exp(b_scr[...])).astype(BF16)
    for r in range(R):
        for h in range(HEADS):
            L = slice(h * HEAD_DIM, (h + 1) * HEAD_DIM)
            o_scr[r * C:(r + 1) * C, L] = _dot_nt(qe[r * C:(r + 1) * C, L],
                                                  st_scr[r * HEADS + h].astype(BF16))

    t_iota = lax.broadcasted_iota(jnp.int32, (SUB, D), 0)
    for r in range(R):
        base = r * C
        for I in range(nsub):
            rows = slice(base + SUB * I, base + SUB * (I + 1))
            prev = slice(base, base + SUB * I)
            bI = b_scr[rows, :]
            qI = q_scr[rows, :]
            if I > 0:
                beta = b_scr[base + SUB * I - 1:base + SUB * I, :]
                qs = (qI * jnp.exp(bI - beta)).astype(BF16)
                ks = (k_scr[prev, :] * jnp.exp(beta - b_scr[prev, :])).astype(BF16)
                vs = v_scr[prev, :].astype(BF16)
                for h in range(HEADS):
                    L = slice(h * HEAD_DIM, (h + 1) * HEAD_DIM)
                    att = _dot_nt(qs[:, L], ks[:, L])
                    o_scr[rows, L] += _dot(att.astype(BF16), vs[:, L])
            for s in range(SUB):
                src = base + SUB * I + s
                keep = t_iota >= s
                dec = jnp.exp(jnp.where(keep, bI - b_scr[src:src + 1, :], 0.0))
                p = jnp.where(keep, qI * dec * k_scr[src:src + 1, :], 0.0)
                p_scr[src * SUB:(src + 1) * SUB, :] = p.astype(BF16)

    for h in range(HEADS):
        L = slice(h * HEAD_DIM, (h + 1) * HEAD_DIM)
        a_scr[:, L] = _dot(p_scr[:, L], ones_ref[...])
    for J in range(M // SUB):
        rows = slice(SUB * J, SUB * (J + 1))
        acc = o_scr[rows, :]
        for s in range(SUB):
            src = SUB * J + s
            acc = acc + a_scr[src * SUB:(src + 1) * SUB, :] * v_scr[src:src + 1, :]
        o_scr[rows, :] = acc

    for r in range(R):
        rows = slice(r * C, (r + 1) * C)
        b_last = b_scr[(r + 1) * C - 1:(r + 1) * C, :]
        kd = (k_scr[rows, :] * jnp.exp(b_last - b_scr[rows, :])).astype(BF16)
        eb_last = jnp.exp(b_last)
        for h in range(HEADS):
            L = slice(h * HEAD_DIM, (h + 1) * HEAD_DIM)
            vt = v_scr[rows, L].T.astype(BF16)
            st_scr[r * HEADS + h] = st_scr[r * HEADS + h] * eb_last[:, L] + _dot(vt, kd[:, L])

    gn = gn_ref[...]
    for h in range(HEADS):
        L = slice(h * HEAD_DIM, (h + 1) * HEAD_DIM)
        oh = o_scr[:, L]
        gzh = gz[:, L]
        o_scr[:, L] = _rms(oh, gn) * (gzh * _sigmoid(gzh))
    y_ref[...] = (xc + _dot(o_scr[...].astype(BF16), wout_ref[...])).reshape(R, C, D)

    @pl.when(ci == pl.num_programs(1) - 1)
    def _():
        for r in range(R):
            for h in range(HEADS):
                s_ref[r, h] = st_scr[r * HEADS + h].T


def _hgrn_prompt(x, gmix, w_in, w_out, gnorm, lb, chunk, seqs):
    B, T, D = x.shape
    C = min(chunk, T)
    R = min(seqs, B)
    assert T % C == 0 and C % SUB == 0 and B % R == 0
    M = R * C
    tril = jnp.kron(jnp.eye(R, dtype=F32), jnp.tril(jnp.ones((C, C), F32))).astype(BF16)
    ones = jnp.ones((HEAD_DIM, HEAD_DIM), BF16)
    const = lambda shape: pl.BlockSpec(shape, lambda b, c: (0,) * len(shape))
    return pl.pallas_call(
        _hgrn_prompt_kernel,
        grid=(B // R, T // C),
        in_specs=[
            pl.BlockSpec((R, C, D), lambda b, c: (b, c, 0)),
            const((1, D)), const((D, 4 * D)), const((D, D)), const((1, HEAD_DIM)), const((1, D)),
            const((M, M)), const((HEAD_DIM, HEAD_DIM)),
        ],
        out_specs=[
            pl.BlockSpec((R, C, D), lambda b, c: (b, c, 0)),
            pl.BlockSpec((R, HEADS, HEAD_DIM, HEAD_DIM), lambda b, c: (b, 0, 0, 0)),
        ],
        out_shape=[
            jax.ShapeDtypeStruct((B, T, D), F32),
            jax.ShapeDtypeStruct((B, HEADS, HEAD_DIM, HEAD_DIM), F32),
        ],
        scratch_shapes=[
            pltpu.VMEM((R * HEADS, HEAD_DIM, HEAD_DIM), F32),
            pltpu.VMEM((M, D), F32),
            pltpu.VMEM((M, D), F32),
            pltpu.VMEM((M, D), F32),
            pltpu.VMEM((M, D), F32),
            pltpu.VMEM((M * SUB, D), BF16),
            pltpu.VMEM((M * SUB, D), F32),
            pltpu.VMEM((M, D), F32),
        ],
        compiler_params=pltpu.CompilerParams(
            dimension_semantics=("arbitrary", "arbitrary"), vmem_limit_bytes=VMEM_LIMIT),
        name="hgrn_prompt",
    )(x, gmix, w_in, w_out, gnorm, lb, tril, ones)


def _hgrn_sample_kernel(x_ref, gmix_ref, win_ref, wout_ref, gn_ref, lb_ref, s_ref,
                        y_ref, snew_ref, proj_scr, o_scr):
    n = pl.program_id(0)
    D = x_ref.shape[1]

    @pl.when(n == 0)
    def _():
        hn = _rms(x_ref[...], gmix_ref[...])
        proj_scr[...] = _dot(hn.astype(BF16), win_ref[...])

    ts = s_ref.shape[0]
    tok0 = pl.multiple_of(n * ts, ts)
    rows = proj_scr[pl.ds(tok0, ts), :]
    qz = rows[:, 0:D]
    fz = rows[:, D:2 * D]
    v = rows[:, 2 * D:3 * D]
    lb = lb_ref[...]
    f = lb + (1.0 - lb) * _sigmoid(fz)
    q = qz * _sigmoid(qz)
    k = 1.0 - f

    def cols(a):
        return jnp.concatenate([a, jnp.zeros((HEAD_DIM - ts, HEAD_DIM), F32)], axis=0).T

    for h in range(HEADS):
        L = slice(h * HEAD_DIM, (h + 1) * HEAD_DIM)
        fc = cols(f[:, L])
        kc = cols(k[:, L])
        qc = cols(q[:, L])
        outs = []
        for j in range(ts):
            s_new = s_ref[j, h] * fc[:, j:j + 1] + kc[:, j:j + 1] * v[j:j + 1, L]
            snew_ref[j, h] = s_new
            outs.append(jnp.sum(qc[:, j:j + 1] * s_new, axis=0, keepdims=True))
        o_scr[pl.ds(tok0, ts), L] = jnp.concatenate(outs, axis=0)

    @pl.when(n == pl.num_programs(0) - 1)
    def _():
        gn = gn_ref[...]
        gz = proj_scr[:, 3 * D:4 * D]
        for h in range(HEADS):
            L = slice(h * HEAD_DIM, (h + 1) * HEAD_DIM)
            gzh = gz[:, L]
            o_scr[:, L] = _rms(o_scr[:, L], gn) * (gzh * _sigmoid(gzh))
        y_ref[...] = x_ref[...] + _dot(o_scr[...].astype(BF16), wout_ref[...])


def _hgrn_sample(x, gmix, w_in, w_out, gnorm, lb, state, block):
    N, D = x.shape
    ts = min(block, N)
    assert N % ts == 0 and ts % 8 == 0
    const = lambda shape: pl.BlockSpec(shape, lambda n: (0,) * len(shape))
    sblk = pl.BlockSpec((ts, HEADS, HEAD_DIM, HEAD_DIM), lambda n: (n, 0, 0, 0))
    return pl.pallas_call(
        _hgrn_sample_kernel,
        grid=(N // ts,),
        in_specs=[const((N, D)), const((1, D)), const((D, 4 * D)), const((D, D)),
                  const((1, HEAD_DIM)), const((1, D)), sblk],
        out_specs=[const((N, D)), sblk],
        out_shape=[jax.ShapeDtypeStruct((N, D), F32),
                   jax.ShapeDtypeStruct((N, HEADS, HEAD_DIM, HEAD_DIM), F32)],
        scratch_shapes=[pltpu.VMEM((N, 4 * D), F32), pltpu.VMEM((N, D), F32)],
        compiler_params=pltpu.CompilerParams(
            dimension_semantics=("arbitrary",), vmem_limit_bytes=VMEM_LIMIT),
        name="hgrn_sample",
    )(x, gmix, w_in, w_out, gnorm, lb, state)


CONV_PAD = 32


def _conv_tail(y, xc, bdw_ref, lng_ref, lnb_ref, pw2_ref, bpw2_ref):
    y = y + bdw_ref[...]
    mu = jnp.mean(y, axis=-1, keepdims=True)
    yc = y - mu
    var = jnp.mean(yc * yc, axis=-1, keepdims=True)
    yn = yc * lax.rsqrt(var + EPS) * lng_ref[...] + lnb_ref[...]
    z = yn * _sigmoid(yn)
    return xc + _dot(z.astype(BF16), pw2_ref[...]) + bpw2_ref[...]


def _glu(xc, gmix_ref, pw1_ref, bpw1_ref):
    D = xc.shape[1]
    hn = _rms(xc, gmix_ref[...])
    h2 = _dot(hn.astype(BF16), pw1_ref[...]) + bpw1_ref[...]
    return h2[:, 0:D] * _sigmoid(h2[:, D:2 * D])


def _conv_prompt_kernel(x_ref, gmix_ref, pw1_ref, bpw1_ref, wdw_ref, bdw_ref, lng_ref, lnb_ref,
                        pw2_ref, bpw2_ref, y_ref, buf_ref, full_scr, sh_scr, y_scr):
    C = x_ref.shape[1]
    D = x_ref.shape[2]
    ci = pl.program_id(1)
    hist = CONV_WIDTH - 1
    off = CONV_PAD - hist
    sh_rows = sh_scr.shape[1]

    @pl.when(ci == 0)
    def _():
        full_scr[0:CONV_PAD, :] = jnp.zeros((CONV_PAD, full_scr.shape[1]), F32)

    xc = x_ref[0]
    full_scr[CONV_PAD:CONV_PAD + C, :] = _glu(xc, gmix_ref, pw1_ref, bpw1_ref)

    def lane_chunk(c, carry):
        lanes = pl.ds(pl.multiple_of(c * LANES, LANES), LANES)
        for b in range(1, 8):
            sh_scr[b - 1, :, lanes] = full_scr[b:b + sh_rows, lanes]
        w = [jnp.broadcast_to(wdw_ref[j:j + 1, lanes], (8, LANES)) for j in range(CONV_WIDTH)]
        for r0 in range(0, C, 8):
            acc = None
            for j in range(CONV_WIDTH):
                a, b = divmod(off + j, 8)
                rows = slice(8 * a + r0, 8 * a + r0 + 8)
                src = full_scr[rows, lanes] if b == 0 else sh_scr[b - 1, rows, lanes]
                acc = src * w[j] if acc is None else acc + src * w[j]
            y_scr[r0:r0 + 8, lanes] = acc
        return carry

    lax.fori_loop(0, D // LANES, lane_chunk, 0)
    y_ref[0] = _conv_tail(y_scr[...], xc, bdw_ref, lng_ref, lnb_ref, pw2_ref, bpw2_ref)

    @pl.when(ci == pl.num_programs(1) - 1)
    def _():
        buf_ref[0] = full_scr[C + off:C + CONV_PAD, :]

    full_scr[0:CONV_PAD, :] = full_scr[C:C + CONV_PAD, :]


def _conv_prompt(x, gmix, pw1, bpw1, wdw, bdw, lng, lnb, pw2, bpw2, chunk):
    B, T, D = x.shape
    C = min(chunk, T)
    assert T % C == 0 and C >= CONV_PAD
    hist = CONV_WIDTH - 1
    const = lambda shape: pl.BlockSpec(shape, lambda b, c: (0,) * len(shape))
    return pl.pallas_call(
        _conv_prompt_kernel,
        grid=(B, T // C),
        in_specs=[pl.BlockSpec((1, C, D), lambda b, c: (b, c, 0)),
                  const((1, D)), const((D, 2 * D)), const((1, 2 * D)), const((CONV_WIDTH, D)),
                  const((1, D)), const((1, D)), const((1, D)), const((D, D)), const((1, D))],
        out_specs=[pl.BlockSpec((1, C, D), lambda b, c: (b, c, 0)),
                   pl.BlockSpec((1, hist, D), lambda b, c: (b, 0, 0))],
        out_shape=[jax.ShapeDtypeStruct((B, T, D), F32),
                   jax.ShapeDtypeStruct((B, hist, D), F32)],
        scratch_shapes=[pltpu.VMEM((CONV_PAD + C, D), F32),
                        pltpu.VMEM((7, C + CONV_PAD - 8, D), F32),
                        pltpu.VMEM((C, D), F32)],
        compiler_params=pltpu.CompilerParams(
            dimension_semantics=("arbitrary", "arbitrary"), vmem_limit_bytes=VMEM_LIMIT),
        name="conv_prompt",
    )(x, gmix, pw1, bpw1, wdw, bdw, lng, lnb, pw2, bpw2)


def _conv_sample_kernel(x_ref, gmix_ref, pw1_ref, bpw1_ref, wdw_ref, bdw_ref, lng_ref, lnb_ref,
                        pw2_ref, bpw2_ref, buf_ref, y_ref, nbuf_ref, u_scr, y_scr):
    nt = x_ref.shape[0]
    hist = CONV_WIDTH - 1
    xc = x_ref[...]
    u_scr[...] = _glu(xc, gmix_ref, pw1_ref, bpw1_ref)
    w_hist = wdw_ref[0:hist, :]
    w_last = wdw_ref[hist:hist + 1, :]

    def body(n, carry):
        buf = buf_ref[n]
        u = u_scr[pl.ds(n, 1), :]
        y_scr[pl.ds(n, 1), :] = jnp.sum(buf * w_hist, axis=0, keepdims=True) + u * w_last
        nbuf_ref[n, 0:hist - 1, :] = buf[1:hist, :]
        nbuf_ref[n, hist - 1:hist, :] = u
        return carry

    lax.fori_loop(0, nt, body, 0)
    y_ref[...] = _conv_tail(y_scr[...], xc, bdw_ref, lng_ref, lnb_ref, pw2_ref, bpw2_ref)


def _conv_sample(x, gmix, pw1, bpw1, wdw, bdw, lng, lnb, pw2, bpw2, buf, block):
    N, D = x.shape
    nt = min(block, N)
    assert N % nt == 0
    hist = CONV_WIDTH - 1
    const = lambda shape: pl.BlockSpec(shape, lambda i: (0,) * len(shape))
    return pl.pallas_call(
        _conv_sample_kernel,
        grid=(N // nt,),
        in_specs=[pl.BlockSpec((nt, D), lambda i: (i, 0)),
                  const((1, D)), const((D, 2 * D)), const((1, 2 * D)), const((CONV_WIDTH, D)),
                  const((1, D)), const((1, D)), const((1, D)), const((D, D)), const((1, D)),
                  pl.BlockSpec((nt, hist, D), lambda i: (i, 0, 0))],
        out_specs=[pl.BlockSpec((nt, D), lambda i: (i, 0)),
                   pl.BlockSpec((nt, hist, D), lambda i: (i, 0, 0))],
        out_shape=[jax.ShapeDtypeStruct((N, D), F32),
                   jax.ShapeDtypeStruct((N, hist, D), F32)],
        scratch_shapes=[pltpu.VMEM((nt, D), F32), pltpu.VMEM((nt, D), F32)],
        compiler_params=pltpu.CompilerParams(
            dimension_semantics=("arbitrary",), vmem_limit_bytes=VMEM_LIMIT),
        name="conv_sample",
    )(x, gmix, pw1, bpw1, wdw, bdw, lng, lnb, pw2, bpw2, buf)


def _topk_ranks(vals, ids, k):
    rank = jnp.full(vals.shape, float(k), F32)
    tops = []
    for r in range(k):
        m = jnp.max(vals, axis=0, keepdims=True)
        first = jnp.min(jnp.where(vals == m, ids, 1e9), axis=0, keepdims=True)
        sel = ids == first
        vals = jnp.where(sel, NEG_INF, vals)
        rank = jnp.where(sel, float(r), rank)
        tops.append(m)
    return rank, tops


INT_MIN = -2 ** 31


def _sort_key(s):
    b = lax.bitcast_convert_type(jnp.where(s == 0.0, 0.0, s), jnp.int32)
    return b ^ (lax.shift_right_arithmetic(b, 31) & jnp.int32(0x7FFFFFFF))


def _key_value(k):
    b = k ^ (lax.shift_right_arithmetic(k, 31) & jnp.int32(0x7FFFFFFF))
    return lax.bitcast_convert_type(b, F32)


def _topk_keys(keys, k):
    tops = []
    for r in range(k):
        m = jnp.max(keys, axis=0, keepdims=True)
        keys = jnp.where(keys == m, jnp.int32(INT_MIN + r), keys)
        tops.append(m)
    return keys, tops


def _selected(keys, k):
    sel = keys < jnp.int32(INT_MIN + k)
    n = jnp.sum(jnp.where(sel, 1.0, 0.0), axis=0, keepdims=True)
    return sel, jnp.where(n != float(k), 1.0, 0.0)


def _router_kernel(x_ref, g_ref, wq_ref, keys_ref,
                   xnt_ref, rank2_ref, e2_ref, c1_ref, e1_ref,
                   s_scr, rank_scr, top_scr, miss_scr):
    tn = x_ref.shape[0]
    xn = _rms(x_ref[...], g_ref[...])
    xnt = xn.T.astype(BF16)
    xnt_ref[...] = xnt
    qt = _dot(wq_ref[...], xnt).astype(BF16)
    for hp in range(2 * HEADS):
        s_scr[hp] = _dot(keys_ref[hp], qt[hp * HEAD_DIM:(hp + 1) * HEAD_DIM, :])

    outs = (rank2_ref, e2_ref, c1_ref, e1_ref)
    scr = (s_scr, rank_scr, top_scr)
    _select_experts(tn, scr, outs, miss_scr)


def _run_with_tie_fallback(n, per_group, problem, miss_scr):
    def fast(g, carry):
        miss = [problem(g * per_group + c, False) for c in range(per_group)]
        miss_scr[pl.ds(g, 1), :] = functools.reduce(jnp.maximum, miss)
        return carry

    lax.fori_loop(0, n // per_group, fast, 0, unroll=2 if per_group == 1 else 1)

    def fix(g, carry):
        @pl.when(jnp.max(miss_scr[pl.ds(g, 1), :]) > 0.0)
        def _():
            for c in range(per_group):
                problem(g * per_group + c, True)
        return carry

    lax.fori_loop(0, n // per_group, fix, 0)


def _select_experts(tn, scr, outs, miss_scr):
    s_scr, rank_scr, top_scr = scr
    rank2_ref, e2_ref, c1_ref, e1_ref = outs
    nch = tn // LANES
    key_ids = lax.broadcasted_iota(jnp.int32, (HEAD_DIM, LANES), 0).astype(F32)

    def half_problem(it, exact):
        hp = it // nch
        lanes = pl.ds(pl.multiple_of((it % nch) * LANES, LANES), LANES)
        s = s_scr[hp, :, lanes]
        miss = None
        if exact:
            rank, tops = _topk_ranks(s, key_ids, TOPK)
            top = jnp.concatenate(tops, axis=0)
        else:
            keys, tops = _topk_keys(_sort_key(s), TOPK)
            sel, miss = _selected(keys, TOPK)
            rank = jnp.where(sel, (keys ^ jnp.int32(INT_MIN)).astype(F32), float(TOPK))
            top = _key_value(jnp.concatenate(tops, axis=0))
        rank_scr[hp, :, lanes] = rank
        top_scr[hp, :, lanes] = top
        return miss

    _run_with_tie_fallback(2 * HEADS * nch, nch, half_problem, miss_scr)

    row16 = lax.broadcasted_iota(jnp.int32, (16, LANES), 0)
    row8 = lax.broadcasted_iota(jnp.int32, (8, LANES), 0)
    r16 = row16.astype(F32)
    r8 = row8.astype(F32)
    groups_valid = [row16 >= 0, row8 >= 0, row16 >= 2, row8 >= 2,
                    (row8 >= 2) & (row8 <= 4), (row8 >= 2) & (row8 <= 3), row8 == 2]
    groups_ids = [r16, 16.0 + r8, r16 * 16.0, r8 * 16.0 + 1.0, 32.0 + r8, 48.0 + r8, 64.0 + r8]
    cand_ids = jnp.concatenate(
        [jnp.where(v, i, 1e9) for v, i in zip(groups_valid, groups_ids)], axis=0)

    def head_problem(it, exact):
        h = it // nch
        lanes = pl.ds(pl.multiple_of((it % nch) * LANES, LANES), LANES)
        v1 = top_scr[2 * h, :, lanes]
        v2 = top_scr[2 * h + 1, :, lanes]
        groups = [v1[0:1] + v2, v1[1:2] + v2[0:8], v1 + v2[0:1], v1[0:8] + v2[1:2],
                  v1[2:3] + v2[0:8], v1[3:4] + v2[0:8], v1[4:5] + v2[0:8]]
        cand = jnp.concatenate(
            [jnp.where(v, g, NEG_INF) for v, g in zip(groups_valid, groups)], axis=0)
        miss = None
        if exact:
            crank, _ = _topk_ranks(cand, cand_ids, TOPK)
            sel = crank < float(TOPK)
        else:
            ckeys, _ = _topk_keys(_sort_key(cand), TOPK)
            sel, miss = _selected(ckeys, TOPK)
        z = jnp.sum(jnp.where(sel, jnp.exp(jnp.where(sel, cand - cand[0:1], 0.0)), 0.0),
                    axis=0, keepdims=True)
        self32 = jnp.where(sel, 1.0, 0.0)
        cnt = lambda a, b: jnp.sum(self32[a:b], axis=0, keepdims=True)
        per_r1 = (self32[24:40]
                  + jnp.concatenate([self32[40:48], jnp.zeros((8, LANES), F32)], axis=0)
                  + jnp.where(row16 == 0, cnt(0, 16), 0.0)
                  + jnp.where(row16 == 1, cnt(16, 24), 0.0)
                  + jnp.where(row16 == 2, cnt(48, 56), 0.0)
                  + jnp.where(row16 == 3, cnt(56, 64), 0.0)
                  + jnp.where(row16 == 4, cnt(64, 72), 0.0))
        rank1 = rank_scr[2 * h, :, lanes]
        rank2 = rank_scr[2 * h + 1, :, lanes]
        c1 = jnp.zeros((HEAD_DIM, LANES), F32)
        for j in range(1, TOPK // 3 + 1):
            reach = jnp.sum(jnp.where(per_r1 >= float(j), 1.0, 0.0), axis=0, keepdims=True)
            c1 = c1 + jnp.where(rank1 < reach, 1.0, 0.0)
        c1 = jnp.where(rank1 == 0.0, per_r1[0:1], jnp.where(rank1 == 1.0, per_r1[1:2], c1))
        s1 = s_scr[2 * h, :, lanes]
        s2 = s_scr[2 * h + 1, :, lanes]
        in1 = rank1 < float(TOPK)
        in2 = rank2 < float(TOPK)
        e1 = jnp.where(in1, jnp.exp(jnp.where(in1, s1 - v1[0:1], 0.0)), 0.0)
        e2 = jnp.where(in2, jnp.exp(jnp.where(in2, s2 - v2[0:1], 0.0)), 0.0) / z
        rank2_ref[h, :, lanes] = rank2
        e2_ref[h, :, lanes] = 0.5 * e2
        c1_ref[h, :, lanes] = c1
        e1_ref[h, :, lanes] = e1
        return miss

    _run_with_tie_fallback(HEADS * nch, nch, head_problem, miss_scr)


def _router(x, g, wq_t, keys, tn):
    N, D = x.shape
    assert N % tn == 0 and tn % LANES == 0
    const = lambda shape: pl.BlockSpec(shape, lambda i: (0,) * len(shape))
    gate_blk = pl.BlockSpec((HEADS, HEAD_DIM, tn), lambda i: (0, 0, i))
    gate_shape = lambda dt: jax.ShapeDtypeStruct((HEADS, HEAD_DIM, N), dt)
    return pl.pallas_call(
        _router_kernel,
        grid=(N // tn,),
        in_specs=[pl.BlockSpec((tn, D), lambda i: (i, 0)), const((1, D)),
                  const(wq_t.shape), const(keys.shape)],
        out_specs=[pl.BlockSpec((D, tn), lambda i: (0, i)),
                   gate_blk, gate_blk, gate_blk, gate_blk],
        out_shape=[jax.ShapeDtypeStruct((D, N), BF16),
                   gate_shape(F32), gate_shape(F32), gate_shape(F32), gate_shape(F32)],
        scratch_shapes=[pltpu.VMEM((2 * HEADS, HEAD_DIM, tn), F32),
                        pltpu.VMEM((2 * HEADS, HEAD_DIM, tn), F32),
                        pltpu.VMEM((2 * HEADS, TOPK, tn), F32),
                        pltpu.VMEM((2 * HEADS * (tn // LANES), LANES), F32)],
        compiler_params=pltpu.CompilerParams(
            dimension_semantics=("arbitrary",), vmem_limit_bytes=VMEM_LIMIT),
        name="peer_router",
    )(x, g, wq_t, keys)


ROW_TILE = 16
MXU_TILE = 256


def _peer_dense_kernel(xnt_ref, u_ref, vt_ref, rank2_ref, e2_ref, c1_ref, e1_ref, x_ref, gfin_ref,
                       y_ref, hid0, hid1, g_scr, acc_scr, key2_scr, *, final_norm):
    t = pl.program_id(1)
    te, tn = hid0.shape
    nb = te // HEAD_DIM

    def key2_rows(ch, rt, slot):
        tile = (ch * (HEAD_DIM // ROW_TILE) + rt) * 2 * HEADS + slot
        return slice(tile * ROW_TILE, (tile + 1) * ROW_TILE)

    @pl.when(t == 0)
    def _():
        for ref in (hid0, hid1, acc_scr):
            ref[...] = jnp.zeros_like(ref)
        for ch in range(tn // LANES):
            lanes = slice(ch * LANES, (ch + 1) * LANES)
            for rt in range(HEAD_DIM // ROW_TILE):
                keys2 = slice(rt * ROW_TILE, (rt + 1) * ROW_TILE)
                for h in range(HEADS):
                    key2_scr[key2_rows(ch, rt, h), :] = rank2_ref[h, keys2, lanes]
                    key2_scr[key2_rows(ch, rt, HEADS + h), :] = e2_ref[h, keys2, lanes]

    def gate_group(hid_cur, i1, il, ch):
        lanes = slice(ch * LANES, (ch + 1) * LANES)
        bcast = lambda ref, h: jnp.broadcast_to(ref[h, i1:i1 + 1, lanes], (ROW_TILE, LANES))
        c1 = [bcast(c1_ref, h) for h in range(HEADS)]
        e1 = [bcast(e1_ref, h) for h in range(HEADS)]
        for rt in range(HEAD_DIM // ROW_TILE):
            r0 = rt * ROW_TILE
            rows = slice(il * HEAD_DIM + r0, il * HEAD_DIM + r0 + ROW_TILE)
            w = jnp.zeros((ROW_TILE, LANES), F32)
            for h in range(HEADS):
                w = w + jnp.where(key2_scr[key2_rows(ch, rt, h), :] < c1[h],
                                  key2_scr[key2_rows(ch, rt, HEADS + h), :] * e1[h], 0.0)
            hid = hid_cur[rows, lanes]
            gelu2 = hid * (1.0 + lax.erf(hid * 0.7071067811865476))
            g_scr[rows, lanes] = (w * gelu2).astype(BF16)

    def stage(hid_new, hid_cur, i1_base):
        width = min(MXU_TILE, tn)
        depth = min(MXU_TILE, te)
        for k0 in range(0, te, depth):
            for n0 in range(0, tn, width):
                hid_new[k0:k0 + depth, n0:n0 + width] = _dot(u_ref[0, k0:k0 + depth, :],
                                                             xnt_ref[:, n0:n0 + width])
                for il in range(k0 // HEAD_DIM, (k0 + depth) // HEAD_DIM):
                    for ch in range(n0 // LANES, (n0 + width) // LANES):
                        gate_group(hid_cur, i1_base + il, il, ch)
                acc_scr[:, n0:n0 + width] += _dot(vt_ref[0, 0, :, k0:k0 + depth],
                                                  g_scr[k0:k0 + depth, n0:n0 + width])

    group = c1_ref.shape[1]

    @pl.when(t % 2 == 0)
    def _():
        stage(hid0, hid1, nb % group)

    @pl.when(t % 2 == 1)
    def _():
        stage(hid1, hid0, 0)

    @pl.when(t == pl.num_programs(1) - 1)
    def _():
        y = x_ref[...] + acc_scr[...].T
        if final_norm:
            y = _rms(y, gfin_ref[...])
        y_ref[...] = y


def _peer_dense(xnt, u, vt, layer, rank2, e2, c1, e1, x, gfin, tn, final_norm):
    N, D = x.shape
    _, nblk, _, te = vt.shape
    assert N % tn == 0 and u.shape[1] == nblk * te and te % HEAD_DIM == 0
    nb = te // HEAD_DIM
    group = max(8, nb)
    assert group // nb <= 2 and HEAD_DIM % group == 0
    key2_blk = pl.BlockSpec((HEADS, HEAD_DIM, tn), lambda i, t: (0, 0, i))
    key1_blk = pl.BlockSpec(
        (HEADS, group, tn),
        lambda i, t: (0, jnp.clip(((t - 1) * nb) // group, 0, HEAD_DIM // group - 1), i))
    return pl.pallas_call(
        functools.partial(_peer_dense_kernel, final_norm=final_norm),
        grid=(N // tn, nblk + 1),
        in_specs=[pl.BlockSpec((D, tn), lambda i, t: (0, i)),
                  pl.BlockSpec((1, te, D), lambda i, t: (layer, jnp.minimum(t, nblk - 1), 0)),
                  pl.BlockSpec((1, 1, D, te),
                               lambda i, t: (layer, jnp.maximum(t - 1, 0), 0, 0)),
                  key2_blk, key2_blk, key1_blk, key1_blk,
                  pl.BlockSpec((tn, D), lambda i, t: (i, 0)),
                  pl.BlockSpec((1, D), lambda i, t: (0, 0))],
        out_specs=pl.BlockSpec((tn, D), lambda i, t: (i, 0)),
        out_shape=jax.ShapeDtypeStruct((N, D), F32),
        scratch_shapes=[pltpu.VMEM((te, tn), F32), pltpu.VMEM((te, tn), F32),
                        pltpu.VMEM((te, tn), BF16),
                        pltpu.VMEM((D, tn), F32),
                        pltpu.VMEM((2 * HEADS * HEAD_DIM * (tn // LANES), LANES), F32)],
        compiler_params=pltpu.CompilerParams(
            dimension_semantics=("arbitrary", "arbitrary"), vmem_limit_bytes=VMEM_LIMIT),
        name="peer_dense",
    )(xnt, u, vt, rank2, e2, c1, e1, x, gfin)


def _peer(x, g, wq_t, keys, u, vt, layer, gfin, tn_router, tn_dense, final_norm):
    xnt, rank2, e2, c1, e1 = _router(x, g, wq_t, keys, tn_router)
    return _peer_dense(xnt, u, vt, layer, rank2, e2, c1, e1, x, gfin, tn_dense, final_norm)


HGRN_CHUNK = 64
HGRN_SEQS = 2
HGRN_SAMPLE_BLOCK = 8
CONV_CHUNK = 512
CONV_SAMPLE_BLOCK = 32
PEER_ROUTER_TOKENS = 512
PEER_DENSE_TOKENS = 512
PEER_EXPERTS = 512


def kernel(x_prompt, x_sample, state_hgrn, state_conv, norm_mix, norm_ffn, norm_final, hg_w_in, hg_w_out, hg_gnorm, hg_lb_logits, cv_w_pw1, cv_b_pw1, cv_w_dw, cv_b_dw, cv_ln_g, cv_ln_b, cv_w_pw2, cv_b_pw2, peer_w_q, peer_sub_keys, peer_u, peer_v):
    B, T, D = x_prompt.shape
    NS = x_sample.shape[0]
    depth = norm_mix.shape[0]
    row = lambda a: a.reshape(1, -1)

    lb_all = jnp.cumsum(jax.nn.softmax(hg_lb_logits.astype(F32), axis=0), axis=0)

    u_all = peer_u.astype(BF16)
    vt_all = peer_v.reshape(depth, -1, PEER_EXPERTS, D).transpose(0, 1, 3, 2).astype(BF16)

    xp = x_prompt
    xs = x_sample.reshape(NS, D)
    hg_p, hg_s, cv_p, cv_s = [], [], [], []
    for i in range(depth):
        j = i // 2
        gmix = row(norm_mix[i])
        if i % 2 == 0:
            w_in = hg_w_in[j].astype(BF16)
            w_out = hg_w_out[j].astype(BF16)
            gn = row(hg_gnorm[j])
            lb = row(lb_all[i])
            xp, sp = _hgrn_prompt(xp, gmix, w_in, w_out, gn, lb, HGRN_CHUNK, HGRN_SEQS)
            xs, ss = _hgrn_sample(xs, gmix, w_in, w_out, gn, lb, state_hgrn[j],
                                  HGRN_SAMPLE_BLOCK)
            hg_p.append(sp)
            hg_s.append(ss)
        else:
            cw = (cv_w_pw1[j].astype(BF16), row(cv_b_pw1[j]), cv_w_dw[j], row(cv_b_dw[j]),
                  row(cv_ln_g[j]), row(cv_ln_b[j]), cv_w_pw2[j].astype(BF16), row(cv_b_pw2[j]))
            xp, bp = _conv_prompt(xp, gmix, *cw, CONV_CHUNK)
            xs, bs = _conv_sample(xs, gmix, *cw, state_conv[j], CONV_SAMPLE_BLOCK)
            cv_p.append(bp)
            cv_s.append(bs)
        wq_t = peer_w_q[i].T.astype(BF16)
        keys = peer_sub_keys[i].reshape(2 * HEADS, HEAD_DIM, HEAD_DIM).astype(BF16)
        gffn = row(norm_ffn[i])
        gfin = row(norm_final)
        last = i == depth - 1
        xp = _peer(xp.reshape(B * T, D), gffn, wq_t, keys, u_all, vt_all, i, gfin,
                   min(PEER_ROUTER_TOKENS, B * T), min(PEER_DENSE_TOKENS, B * T),
                   last).reshape(B, T, D)
        xs = _peer(xs, gffn, wq_t, keys, u_all, vt_all, i, gfin, min(LANES, NS), min(LANES, NS),
                   last)
    return (xp, xs.reshape(NS, 1, D), jnp.stack(hg_p), jnp.stack(hg_s),
            jnp.stack(cv_p), jnp.stack(cv_s))
```
